```python
import jax
import jax.numpy as jnp
from jax import lax
import numpy as np

D_MODEL = 1024
BATCH = 2
SEQ = 8192
DEPTH = 2
DEC_BATCH = 128
DEC_SEQ = 8
PAST_LEN = 16384
PAGE_SIZE = 128

HEAD_DIM = 64
N_Q_HEADS = D_MODEL // HEAD_DIM
N_KV_HEADS = 4
GQA_GROUP = N_Q_HEADS // N_KV_HEADS
Q_WIDTH = N_Q_HEADS * HEAD_DIM
KV_WIDTH = N_KV_HEADS * HEAD_DIM
QKV_WIDTH = Q_WIDTH + 2 * KV_WIDTH
ROT_DIM = HEAD_DIM // 4
ROPE_THETA = 500000.0
ATTN_SCALE = HEAD_DIM ** -0.5
WINDOW_A = 128
B_PATTERNS = ((128, 1), (512, 4), (2048, 16))
N_B_GROUPS = len(B_PATTERNS)
BAND_BLOCK = 128
N_EXPERT_GROUPS = 8
EXPERTS_PER_GROUP = 8
N_EXPERTS = N_EXPERT_GROUPS * EXPERTS_PER_GROUP
TOP_K_INNER = 2
D_FF_EXPERT = 512
MOE_BLOCK = 128
RMS_EPS = 1e-5
N_LAYERS_A = (DEPTH + 1) // 2
N_LAYERS_B = DEPTH // 2

kernel_name = 'hybrid_swa_sink_dilated_hmoe_step'


def rms_norm(x, g):
    xf = x.astype(jnp.float32)
    y = xf * lax.rsqrt(jnp.mean(xf * xf, axis=-1, keepdims=True) + RMS_EPS)
    return (y * g.astype(jnp.float32)).astype(x.dtype)


def rope_tables(positions):
    inv_freq = ROPE_THETA ** (-jnp.arange(0, ROT_DIM, 2, dtype=jnp.float32) / ROT_DIM)
    ang = positions.astype(jnp.float32)[:, None] * inv_freq[None, :]
    return jnp.cos(ang), jnp.sin(ang)


def apply_partial_rope(x, cos, sin):
    shape = (1, cos.shape[0]) + (1,) * (x.ndim - 3) + (cos.shape[1],)
    c, s = cos.reshape(shape), sin.reshape(shape)
    half = ROT_DIM // 2
    xr = x[..., :ROT_DIM].astype(jnp.float32)
    x1, x2 = xr[..., :half], xr[..., half:]
    rot = jnp.concatenate([x1 * c - x2 * s, x2 * c + x1 * s], axis=-1).astype(x.dtype)
    return jnp.concatenate([rot, x[..., ROT_DIM:]], axis=-1)


def project_qkv(h, w_qkv, positions, n_groups):
    b_, l_ = h.shape[0], h.shape[1]
    qkv = jnp.einsum('bld,dc->blc', h, w_qkv).reshape(b_, l_, n_groups, QKV_WIDTH)
    q = qkv[..., :Q_WIDTH].reshape(b_, l_, n_groups, N_KV_HEADS, GQA_GROUP, HEAD_DIM)
    k = qkv[..., Q_WIDTH:Q_WIDTH + KV_WIDTH].reshape(b_, l_, n_groups, N_KV_HEADS, HEAD_DIM)
    v = qkv[..., Q_WIDTH + KV_WIDTH:].reshape(b_, l_, n_groups, N_KV_HEADS, HEAD_DIM)
    cos, sin = rope_tables(positions)
    return apply_partial_rope(q, cos, sin), apply_partial_rope(k, cos, sin), v


def softmax_with_lse(s, sink):
    lse = jax.nn.logsumexp(s, axis=-1)
    if sink is not None:
        lse = jnp.logaddexp(lse, sink)
    return jnp.exp(s - lse[..., None]), lse


def band_attention(q, k, v, max_dist, sink):
    b_, l_ = q.shape[0], q.shape[1]
    nb = -(-l_ // BAND_BLOCK)
    pad = nb * BAND_BLOCK - l_
    qb = jnp.pad(q, ((0, 0), (0, pad), (0, 0), (0, 0), (0, 0))).reshape(
        b_, nb, BAND_BLOCK, N_KV_HEADS, GQA_GROUP, HEAD_DIM)

    def key_blocks(t):
        tb = jnp.pad(t, ((0, 0), (BAND_BLOCK, pad), (0, 0), (0, 0))).reshape(
            b_, nb + 1, BAND_BLOCK, N_KV_HEADS, HEAD_DIM)
        return jnp.concatenate([tb[:, :-1], tb[:, 1:]], axis=2)

    kw, vw = key_blocks(k), key_blocks(v)
    s = jnp.einsum('bnqhgd,bnmhd->bnhgqm', qb, kw, preferred_element_type=jnp.float32) * ATTN_SCALE
    qi = jnp.arange(BAND_BLOCK)[:, None]
    mi = jnp.arange(2 * BAND_BLOCK)[None, :]
    dist = BAND_BLOCK + qi - mi
    key_idx = jnp.arange(nb)[:, None, None] * BAND_BLOCK + mi[None] - BAND_BLOCK
    valid = (dist >= 0) & (dist <= max_dist) & (key_idx >= 0)
    s = jnp.where(valid[None, :, None, None], s, -jnp.inf)
    p, lse = softmax_with_lse(s, None if sink is None else sink[:, :, None])
    o = jnp.einsum('bnhgqm,bnmhd->bnqhgd', p, vw.astype(jnp.float32))
    o = o.reshape(b_, nb * BAND_BLOCK, N_KV_HEADS, GQA_GROUP, HEAD_DIM)[:, :l_]
    lse = jnp.transpose(lse, (0, 1, 4, 2, 3)).reshape(b_, nb * BAND_BLOCK, N_KV_HEADS, GQA_GROUP)[:, :l_]
    return o, lse


def dilated_band_attention(q, k, v, window, dilation):
    b_, l_ = q.shape[0], q.shape[1]

    def to_strided(t):
        t = t.reshape((b_, l_ // dilation, dilation) + t.shape[2:])
        return jnp.swapaxes(t, 1, 2).reshape((b_ * dilation, l_ // dilation) + t.shape[3:])

    def from_strided(t):
        t = t.reshape((b_, dilation, l_ // dilation) + t.shape[2:])
        return jnp.swapaxes(t, 1, 2).reshape((b_, l_) + t.shape[3:])

    o, lse = band_attention(to_strided(q), to_strided(k), to_strided(v), window // dilation, None)
    return from_strided(o), from_strided(lse)


def gather_attention(q, kv_past, k_new, v_new, n_keys, dilation, sink):
    t_, n_past = q.shape[1], kv_past.shape[1]
    k_all = jnp.concatenate([kv_past[:, :, 0].astype(k_new.dtype), k_new], axis=1)
    v_all = jnp.concatenate([kv_past[:, :, 1].astype(v_new.dtype), v_new], axis=1)
    idx = n_past + jnp.arange(t_)[:, None] - dilation * jnp.arange(n_keys)[None, :]
    valid = idx >= 0
    idx = jnp.maximum(idx, 0)
    kg, vg = k_all[:, idx], v_all[:, idx]
    s = jnp.einsum('bthgd,btmhd->bthgm', q, kg, preferred_element_type=jnp.float32) * ATTN_SCALE
    s = jnp.where(valid[None, :, None, None, :], s, -jnp.inf)
    p, lse = softmax_with_lse(s, sink)
    o = jnp.einsum('bthgm,btmhd->bthgd', p, vg.astype(jnp.float32))
    return o, lse


def window_rows(k, v, window):
    n = min(window, k.shape[1])
    return jnp.stack([k[:, k.shape[1] - n:], v[:, v.shape[1] - n:]], axis=2)


def out_proj(o, w_o, dtype):
    b_, l_ = o.shape[0], o.shape[1]
    return jnp.einsum('blc,cd->bld', o.reshape(b_, l_, Q_WIDTH).astype(dtype), w_o)


def mixer_a(h, positions, w_qkv, sinks, w_o, kv_past):
    q, k, v = project_qkv(h, w_qkv, positions, 1)
    q, k, v = q[:, :, 0], k[:, :, 0], v[:, :, 0]
    sink = sinks.reshape(N_KV_HEADS, GQA_GROUP).astype(jnp.float32)
    if kv_past is None:
        o, _ = band_attention(q, k, v, WINDOW_A - 1, sink)
        state = window_rows(k, v, WINDOW_A)
    else:
        o, _ = gather_attention(q, kv_past, k, v, WINDOW_A, 1, sink)
        state = jnp.stack([k, v], axis=2)
    return out_proj(o, w_o, h.dtype), state


def mixer_b(h, positions, w_qkv, w_o, kv_past):
    q, k, v = project_qkv(h, w_qkv, positions, N_B_GROUPS)
    outs, lses, states = [], [], []
    for g, (window, dil) in enumerate(B_PATTERNS):
        qg, kg, vg = q[:, :, g], k[:, :, g], v[:, :, g]
        if kv_past is None:
            o, lse = dilated_band_attention(qg, kg, vg, window, dil)
            states.append(window_rows(kg, vg, window))
        else:
            o, lse = gather_attention(qg, kv_past[g], kg, vg, window // dil + 1, dil, None)
            states.append(jnp.stack([kg, vg], axis=2))
        outs.append(o)
        lses.append(lse)
    wts = jax.nn.softmax(jnp.stack(lses), axis=0)
    o = jnp.einsum('nblhg,nblhgd->blhgd', wts, jnp.stack(outs))
    return out_proj(o, w_o, h.dtype), states


def expert_dispatch(x, expert_id, weight, w_gate, w_up, w_down):
    n_tok, d_ = x.shape
    n_asg = expert_id.size
    flat_e = expert_id.reshape(-1)
    order = jnp.argsort(flat_e)
    sorted_e = flat_e[order]
    token_of = order // TOP_K_INNER
    counts = jnp.bincount(flat_e, length=N_EXPERTS)
    padded = (counts + MOE_BLOCK - 1) // MOE_BLOCK * MOE_BLOCK
    pad_end = jnp.cumsum(padded)
    pad_start = pad_end - padded
    seg_start = jnp.cumsum(counts) - counts
    dest = pad_start[sorted_e] + jnp.arange(n_asg, dtype=jnp.int32) - seg_start[sorted_e]
    n_blocks = -(-n_asg // MOE_BLOCK) + N_EXPERTS
    xbuf = jnp.zeros((n_blocks * MOE_BLOCK, d_), x.dtype).at[dest].set(x[token_of])
    block_e = jnp.minimum(
        jnp.searchsorted(pad_end, jnp.arange(n_blocks, dtype=jnp.int32) * MOE_BLOCK, side='right'),
        N_EXPERTS - 1)

    def block_ffn(args):
        xb, e = args
        gate = xb @ w_gate[e]
        up = xb @ w_up[e]
        return (jax.nn.silu(gate) * up) @ w_down[e]

    ybuf = lax.map(block_ffn, (xbuf.reshape(n_blocks, MOE_BLOCK, d_), block_e)).reshape(-1, d_)
    y = ybuf[dest].astype(jnp.float32) * weight.reshape(-1)[order][:, None]
    return jnp.zeros((n_tok, d_), jnp.float32).at[token_of].add(y).astype(x.dtype)


def hierarchical_moe(x, wg, bg, we, be, w_gate, w_up, w_down):
    g_logits = jnp.einsum('nd,dg->ng', x, wg, preferred_element_type=jnp.float32) + bg.astype(jnp.float32)
    g_prob = jax.nn.softmax(g_logits, axis=-1)
    g_idx = jnp.argmax(g_logits, axis=-1).astype(jnp.int32)
    g_w = jnp.take_along_axis(g_prob, g_idx[:, None], axis=1)
    e_logits = (jnp.einsum('nd,de->ne', x, we, preferred_element_type=jnp.float32)
                + be.astype(jnp.float32)).reshape(-1, N_EXPERT_GROUPS, EXPERTS_PER_GROUP)
    e_logits = jnp.take_along_axis(e_logits, g_idx[:, None, None], axis=1)[:, 0]
    top_v, top_i = lax.top_k(e_logits, TOP_K_INNER)
    weight = g_w * jax.nn.softmax(top_v, axis=-1)
    expert_id = g_idx[:, None] * EXPERTS_PER_GROUP + top_i.astype(jnp.int32)
    return expert_dispatch(x, expert_id, weight, w_gate, w_up, w_down)


def setup_inputs(seed: int = 0) -> dict:
    key = jax.random.key(seed)
    ks = jax.random.split(key, 24)

    def nrm(k, shape, scale):
        return jax.random.normal(k, shape, jnp.float32) * scale

    lw_a = min(WINDOW_A, PAST_LEN)
    lw_b = [min(w, PAST_LEN) for (w, _) in B_PATTERNS]
    cache_shape = lambda n_l, rows: (n_l, DEC_BATCH, rows, 2, N_KV_HEADS, HEAD_DIM)
    return {
        'x_prompt': nrm(ks[0], (BATCH, SEQ, D_MODEL), 1.0),
        'x_sample': nrm(ks[1], (DEC_BATCH, DEC_SEQ, D_MODEL), 1.0),
        'cache_a_kv': nrm(ks[2], cache_shape(N_LAYERS_A, lw_a), 1.0),
        'cache_b_kv0': nrm(ks[3], cache_shape(N_LAYERS_B, lw_b[0]), 1.0),
        'cache_b_kv1': nrm(ks[4], cache_shape(N_LAYERS_B, lw_b[1]), 1.0),
        'cache_b_kv2': nrm(ks[5], cache_shape(N_LAYERS_B, lw_b[2]), 1.0),
        'norm_mix': 1.0 + nrm(ks[6], (DEPTH, D_MODEL), 0.02),
        'norm_ffn': 1.0 + nrm(ks[7], (DEPTH, D_MODEL), 0.02),
        'norm_final': 1.0 + nrm(ks[8], (D_MODEL,), 0.02),
        'a_w_qkv': nrm(ks[9], (N_LAYERS_A, D_MODEL, QKV_WIDTH), D_MODEL ** -0.5),
        'a_sinks': nrm(ks[10], (N_LAYERS_A, N_Q_HEADS), 0.5),
        'a_w_o': nrm(ks[11], (N_LAYERS_A, Q_WIDTH, D_MODEL), Q_WIDTH ** -0.5),
        'b_w_qkv': nrm(ks[12], (N_LAYERS_B, D_MODEL, N_B_GROUPS * QKV_WIDTH), D_MODEL ** -0.5),
        'b_w_o': nrm(ks[13], (N_LAYERS_B, Q_WIDTH, D_MODEL), Q_WIDTH ** -0.5),
        'moe_router_g_w': nrm(ks[14], (DEPTH, D_MODEL, N_EXPERT_GROUPS), D_MODEL ** -0.5),
        'moe_router_g_b': nrm(ks[15], (DEPTH, N_EXPERT_GROUPS), 0.01),
        'moe_router_e_w': nrm(ks[16], (DEPTH, D_MODEL, N_EXPERTS), D_MODEL ** -0.5),
        'moe_router_e_b': nrm(ks[17], (DEPTH, N_EXPERTS), 0.01),
        'moe_w_gate': nrm(ks[18], (DEPTH, N_EXPERTS, D_MODEL, D_FF_EXPERT), D_MODEL ** -0.5),
        'moe_w_up': nrm(ks[19], (DEPTH, N_EXPERTS, D_MODEL, D_FF_EXPERT), D_MODEL ** -0.5),
        'moe_w_down': nrm(ks[20], (DEPTH, N_EXPERTS, D_FF_EXPERT, D_MODEL), D_FF_EXPERT ** -0.5),
    }


def reference(x_prompt, x_sample, cache_a_kv, cache_b_kv0, cache_b_kv1, cache_b_kv2,
              norm_mix, norm_ffn, norm_final, a_w_qkv, a_sinks, a_w_o, b_w_qkv, b_w_o,
              moe_router_g_w, moe_router_g_b, moe_router_e_w, moe_router_e_b,
              moe_w_gate, moe_w_up, moe_w_down):
    pos_p = jnp.arange(SEQ, dtype=jnp.int32)
    pos_s = PAST_LEN + jnp.arange(DEC_SEQ, dtype=jnp.int32)
    hp, hs = x_prompt, x_sample
    a_p, a_s = [], []
    b_p = [[] for _ in B_PATTERNS]
    b_s = [[] for _ in B_PATTERNS]
    for i in range(DEPTH):
        li = i // 2
        up = rms_norm(hp, norm_mix[i])
        us = rms_norm(hs, norm_mix[i])
        if i % 2 == 0:
            mp, sp = mixer_a(up, pos_p, a_w_qkv[li], a_sinks[li], a_w_o[li], None)
            ms, ss = mixer_a(us, pos_s, a_w_qkv[li], a_sinks[li], a_w_o[li], cache_a_kv[li])
            a_p.append(sp)
            a_s.append(ss)
        else:
            mp, sp = mixer_b(up, pos_p, b_w_qkv[li], b_w_o[li], None)
            ms, ss = mixer_b(us, pos_s, b_w_qkv[li], b_w_o[li],
                             (cache_b_kv0[li], cache_b_kv1[li], cache_b_kv2[li]))
            for g in range(N_B_GROUPS):
                b_p[g].append(sp[g])
                b_s[g].append(ss[g])
        hp = hp + mp
        hs = hs + ms
        n_p = rms_norm(hp, norm_ffn[i]).reshape(-1, D_MODEL)
        n_s = rms_norm(hs, norm_ffn[i]).reshape(-1, D_MODEL)
        f = hierarchical_moe(jnp.concatenate([n_p, n_s], axis=0),
                             moe_router_g_w[i], moe_router_g_b[i], moe_router_e_w[i], moe_router_e_b[i],
                             moe_w_gate[i], moe_w_up[i], moe_w_down[i])
        hp = hp + f[:n_p.shape[0]].reshape(hp.shape)
        hs = hs + f[n_p.shape[0]:].reshape(hs.shape)
    y_prompt = rms_norm(hp, norm_final)
    y_sample = rms_norm(hs, norm_final)
    return (y_prompt, y_sample,
            jnp.stack(a_p), jnp.stack(a_s),
            jnp.stack(b_p[0]), jnp.stack(b_s[0]),
            jnp.stack(b_p[1]), jnp.stack(b_s[1]),
            jnp.stack(b_p[2]), jnp.stack(b_s[2]))
```

```python
import functools

import numpy as np
import jax
import jax.numpy as jnp
from jax import lax
from jax.experimental import pallas as pl
from jax.experimental.pallas import tpu as pltpu

D_MODEL = 1024
HEAD_DIM = 64
N_KV_HEADS = 4
GQA_GROUP = 4
N_Q_HEADS = N_KV_HEADS * GQA_GROUP
Q_WIDTH = N_Q_HEADS * HEAD_DIM
KV_WIDTH = N_KV_HEADS * HEAD_DIM
QKV_WIDTH = Q_WIDTH + 2 * KV_WIDTH
ROT_DIM = HEAD_DIM // 4
ROT_HALF = ROT_DIM // 2
ROPE_THETA = 500000.0
ATTN_SCALE = HEAD_DIM ** -0.5
WINDOW_A = 128
B_PATTERNS = ((128, 1), (512, 4), (2048, 16))
BAND_BLOCK = 128
N_EXPERT_GROUPS = 8
EXPERTS_PER_GROUP = 8
N_EXPERTS = N_EXPERT_GROUPS * EXPERTS_PER_GROUP
TOP_K_INNER = 2
D_FF_EXPERT = 512
MOE_BLOCK = 128
RMS_EPS = 1e-5
PAST_LEN = 16384

LANES = 128
SUBLANES = 8
ROW_TILE = 256
VMEM_LIMIT_BYTES = 56 * 1024 * 1024

BF16 = jnp.bfloat16
F32 = jnp.float32
NEG_INF = float("-inf")


def _params(n_axes, vmem=VMEM_LIMIT_BYTES):
    return pltpu.CompilerParams(dimension_semantics=("arbitrary",) * n_axes,
                                vmem_limit_bytes=vmem)


def _rms(x, g):
    var = jnp.mean(x * x, axis=-1, keepdims=True)
    return x * lax.rsqrt(var + RMS_EPS) * g


def _dot(a, b):
    return jnp.dot(a, b, preferred_element_type=F32)


def _dot_nt(a, b):
    return lax.dot_general(a, b, (((1,), (1,)), ((), ())), preferred_element_type=F32)


def _proj_body(n_groups, mode, n_ptiles, *refs):
    it = iter(refs)
    if mode == "split":
        xp_ref, xs_ref = next(it), next(it)
    else:
        h_ref, y0_ref, y1_ref = next(it), next(it), next(it)
    g_ref, w_ref, cos_ref, s1_ref, s2_ref = (next(it) for _ in range(5))
    hout_ref = next(it)
    outs = [(next(it), next(it), next(it)) for _ in range(n_groups)]

    if mode == "split":
        is_p = pl.program_id(0) < n_ptiles
        x = jnp.where(is_p, xp_ref[...], xs_ref[...])
    else:
        x = h_ref[...] + y0_ref[...] + y1_ref[...]
    hout_ref[...] = x
    xn = _rms(x, g_ref[...]).astype(BF16)
    c, s1, s2 = cos_ref[...], s1_ref[...], s2_ref[...]

    def rope(y):
        parts = []
        for j in range(y.shape[1] // LANES):
            yb = y[:, LANES * j:LANES * (j + 1)]
            parts.append(yb * c + pltpu.roll(yb, LANES - ROT_HALF, 1) * s1
                         + pltpu.roll(yb, ROT_HALF, 1) * s2)
        return jnp.concatenate(parts, axis=1)

    half = Q_WIDTH // 2
    for g in range(n_groups):
        base = g * QKV_WIDTH
        q_ref, k_ref, v_ref = outs[g]
        for hf in range(2):
            y = _dot(xn, w_ref[:, base + half * hf:base + half * (hf + 1)])
            q_ref[:, half * hf:half * (hf + 1)] = rope(y).astype(BF16)
        y = _dot(xn, w_ref[:, base + Q_WIDTH:base + Q_WIDTH + KV_WIDTH])
        k_ref[...] = rope(y)
        v_ref[...] = _dot(xn, w_ref[:, base + Q_WIDTH + KV_WIDTH:base + QKV_WIDTH])


def _project(mode, xs, gain, w, tables, n_groups, n_tok, n_ptiles, tiles_per_seq):
    tm = ROW_TILE
    n_tiles = n_tok // tm
    row = lambda i: (i, 0)
    const = lambda i: (0, 0)
    tab = lambda i: (jnp.where(i < n_ptiles, i % tiles_per_seq, tiles_per_seq), 0)
    if mode == "split":
        xp, xsamp = xs
        in_specs = [pl.BlockSpec((tm, D_MODEL), lambda i: (jnp.minimum(i, n_ptiles - 1), 0)),
                    pl.BlockSpec((tm, D_MODEL), lambda i: (jnp.maximum(i - n_ptiles, 0), 0))]
        args = [xp, xsamp]
    else:
        h, y2, y1_off = xs
        in_specs = [pl.BlockSpec((tm, D_MODEL), row),
                    pl.BlockSpec((tm, D_MODEL), row),
                    pl.BlockSpec((tm, D_MODEL), lambda i: (i + y1_off, 0))]
        args = [h, y2, y2]
    in_specs += [pl.BlockSpec((1, D_MODEL), const),
                 pl.BlockSpec(w.shape, const),
                 pl.BlockSpec((tm, LANES), tab), pl.BlockSpec((tm, LANES), tab),
                 pl.BlockSpec((tm, LANES), tab)]
    args += [gain, w, *tables]
    out_shape = [jax.ShapeDtypeStruct((n_tok, D_MODEL), F32)]
    out_specs = [pl.BlockSpec((tm, D_MODEL), row)]
    for _ in range(n_groups):
        out_shape += [jax.ShapeDtypeStruct((n_tok, Q_WIDTH), BF16),
                      jax.ShapeDtypeStruct((n_tok, KV_WIDTH), F32),
                      jax.ShapeDtypeStruct((n_tok, KV_WIDTH), F32)]
        out_specs += [pl.BlockSpec((tm, Q_WIDTH), row), pl.BlockSpec((tm, KV_WIDTH), row),
                      pl.BlockSpec((tm, KV_WIDTH), row)]
    res = pl.pallas_call(
        functools.partial(_proj_body, n_groups, mode, n_ptiles),
        grid=(n_tiles,), in_specs=in_specs, out_specs=out_specs, out_shape=out_shape,
        compiler_params=_params(1), name=f"proj_{mode}")(*args)
    h_out = res[0]
    groups = [tuple(res[1 + 3 * g:4 + 3 * g]) for g in range(n_groups)]
    return h_out, groups


def _band_body(has_sink, want_lse, *refs):
    it = iter(refs)
    sink_ref = next(it) if has_sink else None
    q_ref, kc_ref, kp_ref, vc_ref, vp_ref, bias_ref, hm_ref = (next(it) for _ in range(7))
    o_ref = next(it)
    lse_ref = next(it) if want_lse else None
    blk = BAND_BLOCK

    q = q_ref[...]
    k = jnp.concatenate([kp_ref[...], kc_ref[...]], axis=0).astype(BF16)
    v = jnp.concatenate([vp_ref[...], vc_ref[...]], axis=0).astype(BF16)
    vaug = jnp.concatenate([v, jnp.ones((2 * blk, LANES), BF16)], axis=1)
    bias = bias_ref[0]
    bias4 = jnp.concatenate([bias] * N_KV_HEADS, axis=0)
    lane_head = lax.broadcasted_iota(jnp.int32, (blk, KV_WIDTH), 1) // HEAD_DIM
    lane = lax.broadcasted_iota(jnp.int32, (blk, LANES), 1)
    lse_tile = jnp.zeros((blk, LANES), F32)
    row_head = lax.broadcasted_iota(jnp.int32, (N_KV_HEADS * blk, 1), 0) // blk

    for g4 in range(GQA_GROUP):
        qg = q[:, KV_WIDTH * g4:KV_WIDTH * (g4 + 1)]
        lhs = jnp.concatenate([qg * hm_ref[h] for h in range(N_KV_HEADS)], axis=0)
        s = _dot_nt(lhs, k) + bias4
        m = jnp.max(s, axis=1, keepdims=True)
        if has_sink:
            sink_col = jnp.zeros((N_KV_HEADS * blk, 1), F32)
            for h in range(N_KV_HEADS):
                sink_col = jnp.where(row_head == h, sink_ref[h * GQA_GROUP + g4], sink_col)
            m = jnp.maximum(m, sink_col)
        p = jnp.exp(s - m)
        pv = _dot(p.astype(BF16), vaug)
        l = pv[:, KV_WIDTH:]
        if has_sink:
            l = l + jnp.exp(sink_col - m)
        inv = 1.0 / l
        on = pv[:, :KV_WIDTH] * jnp.concatenate([inv, inv], axis=1)
        og = jnp.zeros((blk, KV_WIDTH), F32)
        for h in range(N_KV_HEADS):
            og = jnp.where(lane_head == h, on[blk * h:blk * (h + 1)], og)
            if want_lse:
                lse_h = m[blk * h:blk * (h + 1)] + jnp.log(l[blk * h:blk * (h + 1)])
                lse_tile = jnp.where(lane == g4 * N_KV_HEADS + h, lse_h, lse_tile)
        o_ref[:, KV_WIDTH * g4:KV_WIDTH * (g4 + 1)] = og.astype(o_ref.dtype)
    if want_lse:
        lse_ref[...] = lse_tile


def _band_bias(max_dist):
    qi = np.arange(BAND_BLOCK)[:, None]
    mi = np.arange(2 * BAND_BLOCK)[None, :]
    dist = BAND_BLOCK + qi - mi
    ok = (dist >= 0) & (dist <= max_dist)
    first = ok & (mi >= BAND_BLOCK)
    return np.where(np.stack([first, ok]), 0.0, NEG_INF).astype(np.float32)


def _head_mask(rows, dtype):
    lane_head = np.arange(KV_WIDTH)[None, None, :] // HEAD_DIM
    hm = (lane_head == np.arange(N_KV_HEADS)[:, None, None]).astype(np.float32)
    return jnp.asarray(np.broadcast_to(hm, (N_KV_HEADS, rows, KV_WIDTH)), dtype)


def _band_attention(q, k, v, sinks, n_seq, seq_len, dil, max_dist, out_dtype, want_lse):
    blk = BAND_BLOCK
    n_tok = q.shape[0]
    tp = n_seq * seq_len
    nb = seq_len // dil // blk
    q2 = q.reshape(n_tok // dil, dil * Q_WIDTH)
    k2 = k.reshape(n_tok // dil, dil * KV_WIDTH)
    v2 = v.reshape(n_tok // dil, dil * KV_WIDTH)
    cur = lambda b, r, n: (b * nb + n, r)
    prev = lambda b, r, n: (b * nb + jnp.maximum(n - 1, 0), r)
    in_specs, args = [], []
    if sinks is not None:
        in_specs.append(pl.BlockSpec(memory_space=pltpu.SMEM))
        args.append(sinks)
    in_specs += [pl.BlockSpec((blk, Q_WIDTH), cur),
                 pl.BlockSpec((blk, KV_WIDTH), cur), pl.BlockSpec((blk, KV_WIDTH), prev),
                 pl.BlockSpec((blk, KV_WIDTH), cur), pl.BlockSpec((blk, KV_WIDTH), prev),
                 pl.BlockSpec((1, blk, 2 * blk), lambda b, r, n: (jnp.minimum(n, 1), 0, 0)),
                 pl.BlockSpec((N_KV_HEADS, blk, KV_WIDTH), lambda b, r, n: (0, 0, 0))]
    args += [q2, k2, k2, v2, v2, jnp.asarray(_band_bias(max_dist)), _head_mask(blk, BF16)]
    out_shape = [jax.ShapeDtypeStruct((tp // dil, dil * Q_WIDTH), out_dtype)]
    out_specs = [pl.BlockSpec((blk, Q_WIDTH), cur)]
    if want_lse:
        out_shape.append(jax.ShapeDtypeStruct((tp // dil, dil * LANES), F32))
        out_specs.append(pl.BlockSpec((blk, LANES), cur))
    res = pl.pallas_call(
        functools.partial(_band_body, sinks is not None, want_lse),
        grid=(n_seq, dil, nb), in_specs=in_specs, out_specs=out_specs, out_shape=out_shape,
        compiler_params=_params(3), name=f"band_d{dil}")(*args)
    o = res[0].reshape(tp, Q_WIDTH)
    lse = res[1].reshape(tp, LANES) if want_lse else None
    return o, lse


def _sample_body(has_sink, want_lse, n_new, seqs, *refs):
    it = iter(refs)
    sink_ref = next(it) if has_sink else None
    q_ref, kn_ref, vn_ref, cache_ref, bp_ref, bn_ref, hm_ref = (next(it) for _ in range(7))
    o_ref = next(it)
    lse_ref = next(it) if want_lse else None
    n_past = cache_ref.shape[-1]
    rows = N_Q_HEADS * n_new
    lane_head = lax.broadcasted_iota(jnp.int32, (n_new, KV_WIDTH), 1) // HEAD_DIM
    lane = lax.broadcasted_iota(jnp.int32, (n_new, LANES), 1)
    zpad = jnp.zeros((LANES - n_new, KV_WIDTH), F32)
    row_head = lax.broadcasted_iota(jnp.int32, (rows, 1), 0) // n_new

    for sb in range(seqs):
        rs = slice(n_new * sb, n_new * (sb + 1))
        q = q_ref[rs, :]
        lhs = jnp.concatenate(
            [q[:, KV_WIDTH * g4:KV_WIDTH * (g4 + 1)] * hm_ref[h]
             for g4 in range(GQA_GROUP) for h in range(N_KV_HEADS)], axis=0).astype(BF16)
        kt = cache_ref[sb, 0].reshape(KV_WIDTH, n_past).astype(BF16)
        vt = cache_ref[sb, 1].reshape(KV_WIDTH, n_past).astype(BF16)
        kn = jnp.concatenate([kn_ref[rs, :], zpad], axis=0).astype(BF16)
        vn = jnp.concatenate([vn_ref[rs, :], zpad], axis=0).astype(BF16)
        s1 = _dot(lhs, kt) + bp_ref[...]
        s2 = _dot_nt(lhs, kn) + bn_ref[...]
        m = jnp.maximum(jnp.max(s1, axis=1, keepdims=True), jnp.max(s2, axis=1, keepdims=True))
        if has_sink:
            sink_col = jnp.zeros((rows, 1), F32)
            for g4 in range(GQA_GROUP):
                for h in range(N_KV_HEADS):
                    sink_col = jnp.where(row_head == g4 * N_KV_HEADS + h, sink_ref[h * GQA_GROUP + g4], sink_col)
            m = jnp.maximum(m, sink_col)
        p1 = jnp.exp(s1 - m)
        p2 = jnp.exp(s2 - m)
        l = jnp.sum(p1, axis=1, keepdims=True) + jnp.sum(p2, axis=1, keepdims=True)
        if has_sink:
            l = l + jnp.exp(sink_col - m)
        pv = _dot_nt(p1.astype(BF16), vt) + _dot(p2.astype(BF16), vn)
        on = pv * (1.0 / l)
        lse = m + jnp.log(l)
        lse_tile = jnp.zeros((n_new, LANES), F32)
        for g4 in range(GQA_GROUP):
            og = jnp.zeros((n_new, KV_WIDTH), F32)
            for h in range(N_KV_HEADS):
                r0 = (g4 * N_KV_HEADS + h) * n_new
                og = jnp.where(lane_head == h, on[r0:r0 + n_new], og)
                if want_lse:
                    lse_tile = jnp.where(lane == g4 * N_KV_HEADS + h, lse[r0:r0 + n_new], lse_tile)
            o_ref[rs, KV_WIDTH * g4:KV_WIDTH * (g4 + 1)] = og.astype(o_ref.dtype)
        if want_lse:
            lse_ref[rs, :] = lse_tile


def _sample_bias(n_new, n_past, n_keys, dil):
    j = (np.arange(N_Q_HEADS * n_new) % n_new)[:, None]
    span = dil * (n_keys - 1)

    def ok(delta):
        return (delta >= 0) & (delta % dil == 0) & (delta <= span)

    past = ok(n_past + j - np.arange(n_past)[None, :])
    i_new = np.arange(LANES)[None, :]
    new = ok(j - i_new) & (i_new < n_new)
    to_bias = lambda m: jnp.asarray(np.where(m, 0.0, NEG_INF).astype(np.float32))
    return to_bias(past), to_bias(new)


def _sample_attention(q, k_new, v_new, cache, sinks, n_new, n_keys, dil, out_dtype, want_lse):
    n_seq, n_past = cache.shape[0], cache.shape[-1]
    seqs = 4 if n_past <= 512 else 2
    rows = seqs * n_new
    bias_past, bias_new = _sample_bias(n_new, n_past, n_keys, dil)
    n_rows = N_Q_HEADS * n_new
    row = lambda i: (i, 0)
    const = lambda i: (0, 0)
    in_specs, args = [], []
    if sinks is not None:
        in_specs.append(pl.BlockSpec(memory_space=pltpu.SMEM))
        args.append(sinks)
    in_specs += [pl.BlockSpec((rows, Q_WIDTH), row), pl.BlockSpec((rows, KV_WIDTH), row),
                 pl.BlockSpec((rows, KV_WIDTH), row),
                 pl.BlockSpec((seqs, 2, N_KV_HEADS, HEAD_DIM, n_past), lambda i: (i, 0, 0, 0, 0)),
                 pl.BlockSpec((n_rows, n_past), const), pl.BlockSpec((n_rows, LANES), const),
                 pl.BlockSpec((N_KV_HEADS, n_new, KV_WIDTH), lambda i: (0, 0, 0))]
    args += [q, k_new, v_new, cache, bias_past, bias_new, _head_mask(n_new, F32)]
    out_shape = [jax.ShapeDtypeStruct((n_seq * n_new, Q_WIDTH), out_dtype)]
    out_specs = [pl.BlockSpec((rows, Q_WIDTH), row)]
    if want_lse:
        out_shape.append(jax.ShapeDtypeStruct((n_seq * n_new, LANES), F32))
        out_specs.append(pl.BlockSpec((rows, LANES), row))
    res = pl.pallas_call(
        functools.partial(_sample_body, sinks is not None, want_lse, n_new, seqs),
        grid=(n_seq // seqs,), in_specs=in_specs, out_specs=out_specs, out_shape=out_shape,
        compiler_params=_params(1), name=f"sample_d{dil}")(*args)
    return res[0], (res[1] if want_lse else None)


def _outproj_body(n_groups, n_ptiles, *refs):
    it = iter(refs)
    o_refs = [(next(it), next(it)) for _ in range(n_groups)]
    lse_refs = [(next(it), next(it)) for _ in range(n_groups)] if n_groups > 1 else None
    expand_ref = next(it) if n_groups > 1 else None
    h_ref, wo_ref, g_ref, wr_hi_ref, wr_lo_ref, br_ref = (next(it) for _ in range(6))
    hout_ref, xn_ref, logit_ref = next(it), next(it), next(it)

    is_p = pl.program_id(0) < n_ptiles
    pick = lambda pair: jnp.where(is_p, pair[0][...], pair[1][...])
    if n_groups == 1:
        o = pick(o_refs[0])
    else:
        lses = [pick(p) for p in lse_refs]
        mx = functools.reduce(jnp.maximum, lses)
        es = [jnp.exp(x - mx) for x in lses]
        inv = 1.0 / functools.reduce(lambda a, b: a + b, es)
        acc = None
        for g in range(n_groups):
            w = es[g] * inv
            w_hi = w.astype(BF16)
            w_lo = (w - w_hi.astype(F32)).astype(BF16)
            wide = _dot(w_hi, expand_ref[...]) + _dot(w_lo, expand_ref[...])
            term = wide * pick(o_refs[g])
            acc = term if acc is None else acc + term
        o = acc.astype(BF16)
    h_new = h_ref[...] + _dot(o, wo_ref[...])
    hout_ref[...] = h_new
    xn = _rms(h_new, g_ref[...])
    xn_ref[...] = xn
    x_hi = xn.astype(BF16)
    x_lo = (xn - x_hi.astype(F32)).astype(BF16)
    logit_ref[...] = (_dot(x_hi, wr_hi_ref[...]) + _dot(x_hi, wr_lo_ref[...])
                      + _dot(x_lo, wr_hi_ref[...]) + br_ref[...])


def _out_project(o_pairs, lse_pairs, h, wo, gain, wr_hi, wr_lo, br, n_ptiles):
    tm = ROW_TILE
    n_tok = h.shape[0]
    n_groups = len(o_pairs)
    row = lambda i: (i, 0)
    const = lambda i: (0, 0)
    prow = lambda i: (jnp.minimum(i, n_ptiles - 1), 0)
    srow = lambda i: (jnp.maximum(i - n_ptiles, 0), 0)
    in_specs, args = [], []
    for op, os_ in o_pairs:
        in_specs += [pl.BlockSpec((tm, Q_WIDTH), prow), pl.BlockSpec((tm, Q_WIDTH), srow)]
        args += [op, os_]
    if n_groups > 1:
        for lp, ls in lse_pairs:
            in_specs += [pl.BlockSpec((tm, LANES), prow), pl.BlockSpec((tm, LANES), srow)]
            args += [lp, ls]
        expand = (np.arange(LANES)[:, None] == np.arange(Q_WIDTH)[None, :] // HEAD_DIM)
        in_specs.append(pl.BlockSpec((LANES, Q_WIDTH), const))
        args.append(jnp.asarray(expand.astype(np.float32), BF16))
    in_specs += [pl.BlockSpec((tm, D_MODEL), row), pl.BlockSpec((Q_WIDTH, D_MODEL), const),
                 pl.BlockSpec((1, D_MODEL), const), pl.BlockSpec((D_MODEL, LANES), const),
                 pl.BlockSpec((D_MODEL, LANES), const), pl.BlockSpec((1, LANES), const)]
    args += [h, wo, gain, wr_hi, wr_lo, br]
    out_shape = [jax.ShapeDtypeStruct((n_tok, D_MODEL), F32), jax.ShapeDtypeStruct((n_tok, D_MODEL), F32),
                 jax.ShapeDtypeStruct((n_tok, LANES), F32)]
    out_specs = [pl.BlockSpec((tm, D_MODEL), row), pl.BlockSpec((tm, D_MODEL), row),
                 pl.BlockSpec((tm, LANES), row)]
    return pl.pallas_call(
        functools.partial(_outproj_body, n_groups, n_ptiles),
        grid=(n_tok // tm,), in_specs=in_specs, out_specs=out_specs, out_shape=out_shape,
        compiler_params=_params(1), name=f"outproj_{n_groups}")(*args)


def _moe_body(n_tok, block_e_ref, n_used_ref, n_valid_ref, dst_ref,
              xn_hbm, wslot_ref, wg_ref, wu_ref, wd_ref, y_hbm,
              xbuf, ybuf, wg_bf, wu_bf, wd_bf, sem_in, sem_out):
    i = pl.program_id(0)
    n_steps = pl.num_programs(0)
    n_used = n_used_ref[0]
    slot = i % 2
    blk = MOE_BLOCK

    def gather(block, sl):
        def body(r, carry):
            d = dst_ref[block * blk + r]
            tok = jnp.where(d >= n_tok, d - n_tok, d)
            pltpu.make_async_copy(xn_hbm.at[pl.ds(tok, 1)], xbuf.at[sl, pl.ds(r, 1)], sem_in.at[sl]).start()
            return carry
        lax.fori_loop(0, n_valid_ref[block], body, 0)

    def wait_rows(hbm, buf, sem, n):
        n_aligned = pl.multiple_of((n // SUBLANES) * SUBLANES, SUBLANES)

        @pl.when(n_aligned > 0)
        def _():
            pltpu.make_async_copy(hbm.at[pl.ds(0, n_aligned)], buf.at[pl.ds(0, n_aligned)], sem).wait()

        def one(r, carry):
            pltpu.make_async_copy(hbm.at[pl.ds(0, 1)], buf.at[pl.ds(0, 1)], sem).wait()
            return carry
        lax.fori_loop(0, n - n_aligned, one, 0)

    @pl.when(i == 0)
    def _():
        xbuf[...] = jnp.zeros(xbuf.shape, F32)

        @pl.when(n_used > 0)
        def _():
            gather(0, 0)

    @pl.when(i + 1 < n_used)
    def _():
        gather(i + 1, 1 - slot)

    @pl.when(i < n_used)
    def _():
        wait_rows(xn_hbm, xbuf.at[slot], sem_in.at[slot], n_valid_ref[i])
        changed = (i == 0) | (block_e_ref[i] != block_e_ref[jnp.maximum(i - 1, 0)])

        @pl.when(changed)
        def _():
            wg_bf[...] = wg_ref[...].astype(BF16)
            wu_bf[...] = wu_ref[...].astype(BF16)
            wd_bf[...] = wd_ref[...].astype(BF16)

        x = xbuf[slot].astype(BF16)
        gate = _dot(x, wg_bf[...])
        up = _dot(x, wu_bf[...])
        mid = (gate * jax.nn.sigmoid(gate) * up).astype(BF16)
        y = _dot(mid, wd_bf[...]) * wslot_ref[...]

        @pl.when(i >= 2)
        def _():
            wait_rows(y_hbm, ybuf.at[slot], sem_out.at[slot], n_valid_ref[jnp.maximum(i - 2, 0)])

        ybuf[slot] = y

        def body(r, carry):
            d = dst_ref[i * blk + r]
            pltpu.make_async_copy(ybuf.at[slot, pl.ds(r, 1)], y_hbm.at[pl.ds(d, 1)], sem_out.at[slot]).start()
            return carry
        lax.fori_loop(0, n_valid_ref[i], body, 0)

    @pl.when(i == n_steps - 1)
    def _():
        for back in (1, 2):
            @pl.when(n_used >= back)
            def _():
                last = jnp.maximum(n_used - back, 0)
                wait_rows(y_hbm, ybuf.at[last % 2], sem_out.at[last % 2], n_valid_ref[last])


def _moe_experts(xn, block_e, n_used, n_valid, dst, wslot, w_gate, w_up, w_down, layer):
    n_tok = xn.shape[0]
    n_slots = dst.shape[0]
    n_blocks = n_slots // MOE_BLOCK
    wmap = lambda i, be, nu, nv, ds: (layer, be[i], 0, 0)
    grid_spec = pltpu.PrefetchScalarGridSpec(
        num_scalar_prefetch=4, grid=(n_blocks,),
        in_specs=[pl.BlockSpec(memory_space=pl.ANY),
                  pl.BlockSpec((MOE_BLOCK, 1), lambda i, be, nu, nv, ds: (i, 0)),
                  pl.BlockSpec((None, None, D_MODEL, D_FF_EXPERT), wmap),
                  pl.BlockSpec((None, None, D_MODEL, D_FF_EXPERT), wmap),
                  pl.BlockSpec((None, None, D_FF_EXPERT, D_MODEL), wmap)],
        out_specs=pl.BlockSpec(memory_space=pl.ANY),
        scratch_shapes=[pltpu.VMEM((2, MOE_BLOCK, D_MODEL), F32),
                        pltpu.VMEM((2, MOE_BLOCK, D_MODEL), F32),
                        pltpu.VMEM((D_MODEL, D_FF_EXPERT), BF16),
                        pltpu.VMEM((D_MODEL, D_FF_EXPERT), BF16),
                        pltpu.VMEM((D_FF_EXPERT, D_MODEL), BF16),
                        pltpu.SemaphoreType.DMA((2,)), pltpu.SemaphoreType.DMA((2,))])
    return pl.pallas_call(
        functools.partial(_moe_body, n_tok),
        grid_spec=grid_spec, out_shape=jax.ShapeDtypeStruct((TOP_K_INNER * n_tok, D_MODEL), F32),
        compiler_params=_params(1), name=f"moe_{layer}")(
            block_e, n_used, n_valid, dst, xn, wslot, w_gate, w_up, w_down)


def _route(logits, n_slots):
    n_tok = logits.shape[0]
    g_logits = logits[:, :N_EXPERT_GROUPS]
    g_prob = jax.nn.softmax(g_logits, axis=-1)
    g_idx = jnp.argmax(g_logits, axis=-1).astype(jnp.int32)
    g_w = jnp.take_along_axis(g_prob, g_idx[:, None], axis=1)
    e_logits = logits[:, N_EXPERT_GROUPS:N_EXPERT_GROUPS + N_EXPERTS].reshape(
        n_tok, N_EXPERT_GROUPS, EXPERTS_PER_GROUP)
    e_logits = jnp.take_along_axis(e_logits, g_idx[:, None, None], axis=1)[:, 0]
    top_v, top_i = lax.top_k(e_logits, TOP_K_INNER)
    weight = (g_w * jax.nn.softmax(top_v, axis=-1)).reshape(-1)
    flat_e = (g_idx[:, None] * EXPERTS_PER_GROUP + top_i.astype(jnp.int32)).reshape(-1)

    n_asg = flat_e.shape[0]
    n_blocks = n_slots // MOE_BLOCK
    order = jnp.argsort(flat_e).astype(jnp.int32)
    counts = jnp.bincount(flat_e, length=N_EXPERTS).astype(jnp.int32)
    padded = (counts + MOE_BLOCK - 1) // MOE_BLOCK * MOE_BLOCK
    pad_end = jnp.cumsum(padded)
    pad_start = pad_end - padded
    seg_start = jnp.cumsum(counts) - counts
    block_e = jnp.minimum(
        jnp.searchsorted(pad_end, jnp.arange(n_blocks, dtype=jnp.int32) * MOE_BLOCK, side="right"),
        N_EXPERTS - 1).astype(jnp.int32)
    n_used = (pad_end[-1] // MOE_BLOCK).astype(jnp.int32).reshape(1)
    block_start = jnp.arange(n_blocks, dtype=jnp.int32) * MOE_BLOCK
    n_valid = jnp.clip(counts[block_e] - (block_start - pad_start[block_e]), 0, MOE_BLOCK).astype(jnp.int32)

    slot = jnp.arange(n_slots, dtype=jnp.int32)
    slot_e = block_e[slot // MOE_BLOCK]
    within = slot - pad_start[slot_e]
    valid = (within >= 0) & (within < counts[slot_e])
    asg = order[jnp.clip(seg_start[slot_e] + within, 0, n_asg - 1)]
    dst = jnp.where(valid, (asg % TOP_K_INNER) * n_tok + asg // TOP_K_INNER, 0).astype(jnp.int32)
    wslot = jnp.where(valid, weight[asg], 0.0).astype(F32).reshape(n_slots, 1)
    return block_e, n_used, n_valid, dst, wslot


def _final_body(h_ref, y0_ref, y1_ref, g_ref, o_ref):
    o_ref[...] = _rms(h_ref[...] + y0_ref[...] + y1_ref[...], g_ref[...])


def _final_norm(h, y2, y1_off, gain, tile0, n_tiles):
    tm = ROW_TILE
    return pl.pallas_call(
        _final_body, grid=(n_tiles,),
        in_specs=[pl.BlockSpec((tm, D_MODEL), lambda i: (i + tile0, 0)),
                  pl.BlockSpec((tm, D_MODEL), lambda i: (i + tile0, 0)),
                  pl.BlockSpec((tm, D_MODEL), lambda i: (i + tile0 + y1_off, 0)),
                  pl.BlockSpec((1, D_MODEL), lambda i: (0, 0))],
        out_specs=pl.BlockSpec((tm, D_MODEL), lambda i: (i, 0)),
        out_shape=jax.ShapeDtypeStruct((n_tiles * tm, D_MODEL), F32),
        compiler_params=_params(1), name="final_norm")(h, y2, y2, gain)


def _q_perm():
    g4, h, d = np.meshgrid(np.arange(GQA_GROUP), np.arange(N_KV_HEADS), np.arange(HEAD_DIM), indexing="ij")
    return ((h * GQA_GROUP + g4) * HEAD_DIM + d).reshape(-1)


def _prep_qkv_weight(w, n_groups):
    perm = _q_perm()
    cols = []
    for g in range(n_groups):
        base = g * QKV_WIDTH
        cols.append(w[:, base:base + Q_WIDTH][:, perm] * ATTN_SCALE)
        cols.append(w[:, base + Q_WIDTH:base + QKV_WIDTH])
    return jnp.concatenate(cols, axis=1).astype(BF16)


def _rope_tables(seq_len, past_len, n_new):
    inv_freq = ROPE_THETA ** (-jnp.arange(0, ROT_DIM, 2, dtype=F32) / ROT_DIM)
    pos_p = jnp.arange(seq_len, dtype=jnp.int32)
    pos_s = past_len + (jnp.arange(ROW_TILE, dtype=jnp.int32) % n_new)
    ang = jnp.concatenate([pos_p, pos_s]).astype(F32)[:, None] * inv_freq[None, :]
    cos, sin = jnp.cos(ang), jnp.sin(ang)
    rows = ang.shape[0]
    pad = jnp.zeros((rows, HEAD_DIM - ROT_DIM), F32)
    zero = jnp.zeros((rows, ROT_HALF), F32)
    c64 = jnp.concatenate([cos, cos, pad + 1.0], axis=1)
    s1 = jnp.concatenate([-sin, zero, pad], axis=1)
    s2 = jnp.concatenate([zero, sin, pad], axis=1)
    dup = lambda t: jnp.concatenate([t] * (LANES // HEAD_DIM), axis=1)
    return dup(c64), dup(s1), dup(s2)


def _split_hi_lo(w):
    hi = w.astype(BF16)
    return hi, (w - hi.astype(F32)).astype(BF16)


def _state_rows(k, v, lead):
    kv = jnp.stack([k, v], axis=-2)
    return kv.reshape(lead + (k.shape[-2], 2, N_KV_HEADS, HEAD_DIM))


def kernel(x_prompt, x_sample, cache_a_kv, cache_b_kv0, cache_b_kv1, cache_b_kv2, norm_mix, norm_ffn, norm_final, a_w_qkv, a_sinks, a_w_o, b_w_qkv, b_w_o, moe_router_g_w, moe_router_g_b, moe_router_e_w, moe_router_e_b, moe_w_gate, moe_w_up, moe_w_down):
    n_seq, seq_len, _ = x_prompt.shape
    n_dec, n_new, _ = x_sample.shape
    tp, ts = n_seq * seq_len, n_dec * n_new
    n_tok = tp + ts
    n_ptiles = tp // ROW_TILE
    tiles_per_seq = seq_len // ROW_TILE
    n_slots = (-(-(n_tok * TOP_K_INNER) // MOE_BLOCK) + N_EXPERTS) * MOE_BLOCK
    half_tiles = n_tok // ROW_TILE
    caches_b = (cache_b_kv0, cache_b_kv1, cache_b_kv2)

    tables = _rope_tables(seq_len, PAST_LEN, n_new)
    perm = _q_perm()
    xp = x_prompt.reshape(tp, D_MODEL)
    xs = x_sample.reshape(ts, D_MODEL)

    def keys_on_lanes(cache):
        return jnp.transpose(cache, (0, 2, 3, 4, 1))

    def router(i):
        w = jnp.concatenate([moe_router_g_w[i], moe_router_e_w[i],
                             jnp.zeros((D_MODEL, LANES - N_EXPERT_GROUPS - N_EXPERTS), F32)], axis=1)
        b = jnp.concatenate([moe_router_g_b[i], moe_router_e_b[i],
                             jnp.zeros((LANES - N_EXPERT_GROUPS - N_EXPERTS,), F32)]).reshape(1, LANES)
        return _split_hi_lo(w) + (b,)

    def moe(i, xn, logits):
        block_e, n_used, n_valid, dst, wslot = _route(logits, n_slots)
        return _moe_experts(xn, block_e, n_used, n_valid, dst, wslot, moe_w_gate, moe_w_up, moe_w_down, i)

    w_a = _prep_qkv_weight(a_w_qkv[0], 1)
    h0, [(q, k, v)] = _project("split", (xp, xs), norm_mix[0:1], w_a, tables, 1, n_tok, n_ptiles, tiles_per_seq)
    sinks = a_sinks[0].astype(F32)
    o_p, _ = _band_attention(q, k, v, sinks, n_seq, seq_len, 1, WINDOW_A - 1, BF16, False)
    o_s, _ = _sample_attention(q[tp:].astype(F32), k[tp:], v[tp:], keys_on_lanes(cache_a_kv[0]), sinks,
                               n_new, WINDOW_A, 1, BF16, False)
    new_a_p = _state_rows(k[:tp].reshape(n_seq, seq_len, KV_WIDTH)[:, seq_len - min(WINDOW_A, seq_len):],
                          v[:tp].reshape(n_seq, seq_len, KV_WIDTH)[:, seq_len - min(WINDOW_A, seq_len):],
                          (n_seq,))[None]
    new_a_s = _state_rows(k[tp:].reshape(n_dec, n_new, KV_WIDTH), v[tp:].reshape(n_dec, n_new, KV_WIDTH),
                          (n_dec,))[None]
    wr_hi, wr_lo, br = router(0)
    h1, xn, logits = _out_project([(o_p, o_s)], None, h0, a_w_o[0][perm].astype(BF16), norm_ffn[0:1],
                                  wr_hi, wr_lo, br, n_ptiles)
    y2 = moe(0, xn, logits)

    w_b = _prep_qkv_weight(b_w_qkv[0], len(B_PATTERNS))
    h2, groups = _project("combine", (h1, y2, half_tiles), norm_mix[1:2], w_b, tables, len(B_PATTERNS),
                          n_tok, n_ptiles, tiles_per_seq)
    o_pairs, lse_pairs, new_b = [], [], []
    for g, (window, dil) in enumerate(B_PATTERNS):
        q, k, v = groups[g]
        o_p, lse_p = _band_attention(q, k, v, None, n_seq, seq_len, dil, window // dil, F32, True)
        o_s, lse_s = _sample_attention(q[tp:].astype(F32), k[tp:], v[tp:], keys_on_lanes(caches_b[g][0]), None,
                                       n_new, window // dil + 1, dil, F32, True)
        o_pairs.append((o_p, o_s))
        lse_pairs.append((lse_p, lse_s))
        n_keep = min(window, seq_len)
        new_b.append(_state_rows(k[:tp].reshape(n_seq, seq_len, KV_WIDTH)[:, seq_len - n_keep:],
                                 v[:tp].reshape(n_seq, seq_len, KV_WIDTH)[:, seq_len - n_keep:], (n_seq,))[None])
        new_b.append(_state_rows(k[tp:].reshape(n_dec, n_new, KV_WIDTH), v[tp:].reshape(n_dec, n_new, KV_WIDTH),
                                 (n_dec,))[None])
    wr_hi, wr_lo, br = router(1)
    h3, xn, logits = _out_project(o_pairs, lse_pairs, h2, b_w_o[0][perm].astype(BF16), norm_ffn[1:2],
                                  wr_hi, wr_lo, br, n_ptiles)
    y2 = moe(1, xn, logits)

    gain = norm_final.reshape(1, D_MODEL)
    y_p = _final_norm(h3, y2, half_tiles, gain, 0, n_ptiles).reshape(n_seq, seq_len, D_MODEL)
    y_s = _final_norm(h3, y2, half_tiles, gain, n_ptiles, ts // ROW_TILE).reshape(n_dec, n_new, D_MODEL)
    return (y_p, y_s, new_a_p, new_a_s, *new_b)
```

```python
import functools

import numpy as np
import jax
import jax.numpy as jnp
from jax import lax
from jax.experimental import pallas as pl
from jax.experimental.pallas import tpu as pltpu

D_MODEL = 1024
HEAD_DIM = 64
N_KV_HEADS = 4
GQA_GROUP = 4
N_Q_HEADS = N_KV_HEADS * GQA_GROUP
Q_WIDTH = N_Q_HEADS * HEAD_DIM
KV_WIDTH = N_KV_HEADS * HEAD_DIM
QKV_WIDTH = Q_WIDTH + 2 * KV_WIDTH
ROT_DIM = HEAD_DIM // 4
ROT_HALF = ROT_DIM // 2
ROPE_THETA = 500000.0
ATTN_SCALE = HEAD_DIM ** -0.5
WINDOW_A = 128
B_PATTERNS = ((128, 1), (512, 4), (2048, 16))
BAND_BLOCK = 128
N_EXPERT_GROUPS = 8
EXPERTS_PER_GROUP = 8
N_EXPERTS = N_EXPERT_GROUPS * EXPERTS_PER_GROUP
TOP_K_INNER = 2
D_FF_EXPERT = 512
MOE_BLOCK = 128
RMS_EPS = 1e-5
PAST_LEN = 16384

LANES = 128
SUBLANES = 8
ROW_TILE = 256
VMEM_LIMIT_BYTES = 56 * 1024 * 1024

BF16 = jnp.bfloat16
F32 = jnp.float32
NEG_INF = float("-inf")


def _params(n_axes, vmem=VMEM_LIMIT_BYTES):
    return pltpu.CompilerParams(dimension_semantics=("arbitrary",) * n_axes,
                                vmem_limit_bytes=vmem)


def _rms(x, g):
    var = jnp.mean(x * x, axis=-1, keepdims=True)
    return x * lax.rsqrt(var + RMS_EPS) * g


def _dot(a, b):
    return jnp.dot(a, b, preferred_element_type=F32)


def _dot_nt(a, b):
    return lax.dot_general(a, b, (((1,), (1,)), ((), ())), preferred_element_type=F32)


def _proj_body(n_groups, mode, n_ptiles, *refs):
    it = iter(refs)
    if mode == "split":
        xp_ref, xs_ref = next(it), next(it)
    else:
        dest_ref, h_ref, info_ref, ys_hbm = next(it), next(it), next(it), next(it)
    g_ref, w_ref, cos_ref, s1_ref, s2_ref = (next(it) for _ in range(5))
    hout_ref = next(it)
    outs = [(next(it), next(it), next(it)) for _ in range(n_groups)]

    if mode == "split":
        is_p = pl.program_id(0) < n_ptiles
        x = jnp.where(is_p, xp_ref[...], xs_ref[...])
    else:
        ybuf, sem = next(it), next(it)
        x = h_ref[...] + _moe_output_tile(dest_ref, info_ref, ys_hbm, ybuf, sem, 0)
    hout_ref[...] = x
    xn = _rms(x, g_ref[...]).astype(BF16)
    c, s1, s2 = cos_ref[...], s1_ref[...], s2_ref[...]

    def rope(y):
        parts = []
        for j in range(y.shape[1] // LANES):
            yb = y[:, LANES * j:LANES * (j + 1)]
            parts.append(yb * c + pltpu.roll(yb, LANES - ROT_HALF, 1) * s1
                         + pltpu.roll(yb, ROT_HALF, 1) * s2)
        return jnp.concatenate(parts, axis=1)

    half = Q_WIDTH // 2
    for g in range(n_groups):
        base = g * QKV_WIDTH
        q_ref, k_ref, v_ref = outs[g]
        for hf in range(2):
            y = _dot(xn, w_ref[:, base + half * hf:base + half * (hf + 1)])
            q_ref[:, half * hf:half * (hf + 1)] = rope(y).astype(BF16)
        y = _dot(xn, w_ref[:, base + Q_WIDTH:base + Q_WIDTH + KV_WIDTH])
        k_ref[...] = rope(y)
        v_ref[...] = _dot(xn, w_ref[:, base + Q_WIDTH + KV_WIDTH:base + QKV_WIDTH])


def _project(mode, xs, gain, w, tables, n_groups, n_tok, n_ptiles, tiles_per_seq):
    tm = ROW_TILE
    n_tiles = n_tok // tm
    row = lambda i, *_: (i, 0)
    const = lambda i, *_: (0, 0)
    tab = lambda i, *_: (jnp.where(i < n_ptiles, i % tiles_per_seq, tiles_per_seq), 0)
    if mode == "split":
        xp, xsamp = xs
        in_specs = [pl.BlockSpec((tm, D_MODEL), lambda i: (jnp.minimum(i, n_ptiles - 1), 0)),
                    pl.BlockSpec((tm, D_MODEL), lambda i: (jnp.maximum(i - n_ptiles, 0), 0))]
        args = [xp, xsamp]
        prefetch, scratch = [], []
    else:
        h, info, ys, dest = xs
        in_specs = [pl.BlockSpec((tm, D_MODEL), row), pl.BlockSpec((tm, LANES), row),
                    pl.BlockSpec(memory_space=pl.ANY)]
        args = [h, info, ys]
        prefetch, scratch = [dest], _COMBINE_SCRATCH
    in_specs += [pl.BlockSpec((1, D_MODEL), const),
                 pl.BlockSpec(w.shape, const),
                 pl.BlockSpec((tm, LANES), tab), pl.BlockSpec((tm, LANES), tab),
                 pl.BlockSpec((tm, LANES), tab)]
    args += [gain, w, *tables]
    out_shape = [jax.ShapeDtypeStruct((n_tok, D_MODEL), F32)]
    out_specs = [pl.BlockSpec((tm, D_MODEL), row)]
    for _ in range(n_groups):
        out_shape += [jax.ShapeDtypeStruct((n_tok, Q_WIDTH), BF16),
                      jax.ShapeDtypeStruct((n_tok, KV_WIDTH), F32),
                      jax.ShapeDtypeStruct((n_tok, KV_WIDTH), F32)]
        out_specs += [pl.BlockSpec((tm, Q_WIDTH), row), pl.BlockSpec((tm, KV_WIDTH), row),
                      pl.BlockSpec((tm, KV_WIDTH), row)]
    grid_spec = pltpu.PrefetchScalarGridSpec(
        num_scalar_prefetch=len(prefetch), grid=(n_tiles,), in_specs=in_specs, out_specs=out_specs,
        scratch_shapes=scratch)
    res = pl.pallas_call(
        functools.partial(_proj_body, n_groups, mode, n_ptiles), grid_spec=grid_spec, out_shape=out_shape,
        compiler_params=_params(1), name=f"proj_{mode}")(*prefetch, *args)
    h_out = res[0]
    groups = [tuple(res[1 + 3 * g:4 + 3 * g]) for g in range(n_groups)]
    return h_out, groups


def _band_body(has_sink, want_lse, *refs):
    it = iter(refs)
    sink_ref = next(it) if has_sink else None
    q_ref, kc_ref, kp_ref, vc_ref, vp_ref, bias_ref, hm_ref = (next(it) for _ in range(7))
    o_ref = next(it)
    lse_ref = next(it) if want_lse else None
    blk = BAND_BLOCK

    q = q_ref[...]
    k = jnp.concatenate([kp_ref[...], kc_ref[...]], axis=0).astype(BF16)
    v = jnp.concatenate([vp_ref[...], vc_ref[...]], axis=0).astype(BF16)
    vaug = jnp.concatenate([v, jnp.ones((2 * blk, LANES), BF16)], axis=1)
    bias = bias_ref[0]
    bias4 = jnp.concatenate([bias] * N_KV_HEADS, axis=0)
    lane_head = lax.broadcasted_iota(jnp.int32, (blk, KV_WIDTH), 1) // HEAD_DIM
    lane = lax.broadcasted_iota(jnp.int32, (blk, LANES), 1)
    lse_tile = jnp.zeros((blk, LANES), F32)
    row_head = lax.broadcasted_iota(jnp.int32, (N_KV_HEADS * blk, 1), 0) // blk

    for g4 in range(GQA_GROUP):
        qg = q[:, KV_WIDTH * g4:KV_WIDTH * (g4 + 1)]
        lhs = jnp.concatenate([qg * hm_ref[h] for h in range(N_KV_HEADS)], axis=0)
        s = _dot_nt(lhs, k) + bias4
        m = jnp.max(s, axis=1, keepdims=True)
        if has_sink:
            sink_col = jnp.zeros((N_KV_HEADS * blk, 1), F32)
            for h in range(N_KV_HEADS):
                sink_col = jnp.where(row_head == h, sink_ref[h * GQA_GROUP + g4], sink_col)
            m = jnp.maximum(m, sink_col)
        p = jnp.exp(s - m)
        pv = _dot(p.astype(BF16), vaug)
        l = pv[:, KV_WIDTH:]
        if has_sink:
            l = l + jnp.exp(sink_col - m)
        inv = 1.0 / l
        on = pv[:, :KV_WIDTH] * jnp.concatenate([inv, inv], axis=1)
        og = jnp.zeros((blk, KV_WIDTH), F32)
        for h in range(N_KV_HEADS):
            og = jnp.where(lane_head == h, on[blk * h:blk * (h + 1)], og)
            if want_lse:
                lse_h = m[blk * h:blk * (h + 1)] + jnp.log(l[blk * h:blk * (h + 1)])
                lse_tile = jnp.where(lane == g4 * N_KV_HEADS + h, lse_h, lse_tile)
        o_ref[:, KV_WIDTH * g4:KV_WIDTH * (g4 + 1)] = og.astype(o_ref.dtype)
    if want_lse:
        lse_ref[...] = lse_tile


def _band_bias(max_dist):
    qi = np.arange(BAND_BLOCK)[:, None]
    mi = np.arange(2 * BAND_BLOCK)[None, :]
    dist = BAND_BLOCK + qi - mi
    ok = (dist >= 0) & (dist <= max_dist)
    first = ok & (mi >= BAND_BLOCK)
    return np.where(np.stack([first, ok]), 0.0, NEG_INF).astype(np.float32)


def _head_mask(rows, dtype):
    lane_head = np.arange(KV_WIDTH)[None, None, :] // HEAD_DIM
    hm = (lane_head == np.arange(N_KV_HEADS)[:, None, None]).astype(np.float32)
    return jnp.asarray(np.broadcast_to(hm, (N_KV_HEADS, rows, KV_WIDTH)), dtype)


def _band_attention(q, k, v, sinks, n_seq, seq_len, dil, max_dist, out_dtype, want_lse):
    blk = BAND_BLOCK
    n_tok = q.shape[0]
    tp = n_seq * seq_len
    nb = seq_len // dil // blk
    q2 = q.reshape(n_tok // dil, dil * Q_WIDTH)
    k2 = k.reshape(n_tok // dil, dil * KV_WIDTH)
    v2 = v.reshape(n_tok // dil, dil * KV_WIDTH)
    cur = lambda b, r, n: (b * nb + n, r)
    prev = lambda b, r, n: (b * nb + jnp.maximum(n - 1, 0), r)
    in_specs, args = [], []
    if sinks is not None:
        in_specs.append(pl.BlockSpec(memory_space=pltpu.SMEM))
        args.append(sinks)
    in_specs += [pl.BlockSpec((blk, Q_WIDTH), cur),
                 pl.BlockSpec((blk, KV_WIDTH), cur), pl.BlockSpec((blk, KV_WIDTH), prev),
                 pl.BlockSpec((blk, KV_WIDTH), cur), pl.BlockSpec((blk, KV_WIDTH), prev),
                 pl.BlockSpec((1, blk, 2 * blk), lambda b, r, n: (jnp.minimum(n, 1), 0, 0)),
                 pl.BlockSpec((N_KV_HEADS, blk, KV_WIDTH), lambda b, r, n: (0, 0, 0))]
    args += [q2, k2, k2, v2, v2, jnp.asarray(_band_bias(max_dist)), _head_mask(blk, BF16)]
    out_shape = [jax.ShapeDtypeStruct((tp // dil, dil * Q_WIDTH), out_dtype)]
    out_specs = [pl.BlockSpec((blk, Q_WIDTH), cur)]
    if want_lse:
        out_shape.append(jax.ShapeDtypeStruct((tp // dil, dil * LANES), F32))
        out_specs.append(pl.BlockSpec((blk, LANES), cur))
    res = pl.pallas_call(
        functools.partial(_band_body, sinks is not None, want_lse),
        grid=(n_seq, dil, nb), in_specs=in_specs, out_specs=out_specs, out_shape=out_shape,
        compiler_params=_params(3), name=f"band_d{dil}")(*args)
    o = res[0].reshape(tp, Q_WIDTH)
    lse = res[1].reshape(tp, LANES) if want_lse else None
    return o, lse


def _sample_body(has_sink, want_lse, n_new, seqs, *refs):
    it = iter(refs)
    sink_ref = next(it) if has_sink else None
    q_ref, kn_ref, vn_ref, cache_ref, bp_ref, bn_ref, hm_ref = (next(it) for _ in range(7))
    o_ref = next(it)
    lse_ref = next(it) if want_lse else None
    n_past = cache_ref.shape[-1]
    rows = N_Q_HEADS * n_new
    lane_head = lax.broadcasted_iota(jnp.int32, (n_new, KV_WIDTH), 1) // HEAD_DIM
    lane = lax.broadcasted_iota(jnp.int32, (n_new, LANES), 1)
    zpad = jnp.zeros((LANES - n_new, KV_WIDTH), F32)
    row_head = lax.broadcasted_iota(jnp.int32, (rows, 1), 0) // n_new

    for sb in range(seqs):
        rs = slice(n_new * sb, n_new * (sb + 1))
        q = q_ref[rs, :]
        lhs = jnp.concatenate(
            [q[:, KV_WIDTH * g4:KV_WIDTH * (g4 + 1)] * hm_ref[h]
             for g4 in range(GQA_GROUP) for h in range(N_KV_HEADS)], axis=0).astype(BF16)
        kt = cache_ref[sb, 0].reshape(KV_WIDTH, n_past).astype(BF16)
        vt = cache_ref[sb, 1].reshape(KV_WIDTH, n_past).astype(BF16)
        kn = jnp.concatenate([kn_ref[rs, :], zpad], axis=0).astype(BF16)
        vn = jnp.concatenate([vn_ref[rs, :], zpad], axis=0).astype(BF16)
        s1 = _dot(lhs, kt) + bp_ref[...]
        s2 = _dot_nt(lhs, kn) + bn_ref[...]
        m = jnp.maximum(jnp.max(s1, axis=1, keepdims=True), jnp.max(s2, axis=1, keepdims=True))
        if has_sink:
            sink_col = jnp.zeros((rows, 1), F32)
            for g4 in range(GQA_GROUP):
                for h in range(N_KV_HEADS):
                    sink_col = jnp.where(row_head == g4 * N_KV_HEADS + h, sink_ref[h * GQA_GROUP + g4], sink_col)
            m = jnp.maximum(m, sink_col)
        p1 = jnp.exp(s1 - m)
        p2 = jnp.exp(s2 - m)
        l = jnp.sum(p1, axis=1, keepdims=True) + jnp.sum(p2, axis=1, keepdims=True)
        if has_sink:
            l = l + jnp.exp(sink_col - m)
        pv = _dot_nt(p1.astype(BF16), vt) + _dot(p2.astype(BF16), vn)
        on = pv * (1.0 / l)
        lse = m + jnp.log(l)
        lse_tile = jnp.zeros((n_new, LANES), F32)
        for g4 in range(GQA_GROUP):
            og = jnp.zeros((n_new, KV_WIDTH), F32)
            for h in range(N_KV_HEADS):
                r0 = (g4 * N_KV_HEADS + h) * n_new
                og = jnp.where(lane_head == h, on[r0:r0 + n_new], og)
                if want_lse:
                    lse_tile = jnp.where(lane == g4 * N_KV_HEADS + h, lse[r0:r0 + n_new], lse_tile)
            o_ref[rs, KV_WIDTH * g4:KV_WIDTH * (g4 + 1)] = og.astype(o_ref.dtype)
        if want_lse:
            lse_ref[rs, :] = lse_tile


def _sample_bias(n_new, n_past, n_keys, dil):
    j = (np.arange(N_Q_HEADS * n_new) % n_new)[:, None]
    span = dil * (n_keys - 1)

    def ok(delta):
        return (delta >= 0) & (delta % dil == 0) & (delta <= span)

    past = ok(n_past + j - np.arange(n_past)[None, :])
    i_new = np.arange(LANES)[None, :]
    new = ok(j - i_new) & (i_new < n_new)
    to_bias = lambda m: jnp.asarray(np.where(m, 0.0, NEG_INF).astype(np.float32))
    return to_bias(past), to_bias(new)


def _sample_attention(q, k_new, v_new, cache, sinks, n_new, n_keys, dil, out_dtype, want_lse):
    n_seq, n_past = cache.shape[0], cache.shape[-1]
    seqs = 4 if n_past <= 512 else 2
    rows = seqs * n_new
    bias_past, bias_new = _sample_bias(n_new, n_past, n_keys, dil)
    n_rows = N_Q_HEADS * n_new
    row = lambda i: (i, 0)
    const = lambda i: (0, 0)
    in_specs, args = [], []
    if sinks is not None:
        in_specs.append(pl.BlockSpec(memory_space=pltpu.SMEM))
        args.append(sinks)
    in_specs += [pl.BlockSpec((rows, Q_WIDTH), row), pl.BlockSpec((rows, KV_WIDTH), row),
                 pl.BlockSpec((rows, KV_WIDTH), row),
                 pl.BlockSpec((seqs, 2, N_KV_HEADS, HEAD_DIM, n_past), lambda i: (i, 0, 0, 0, 0)),
                 pl.BlockSpec((n_rows, n_past), const), pl.BlockSpec((n_rows, LANES), const),
                 pl.BlockSpec((N_KV_HEADS, n_new, KV_WIDTH), lambda i: (0, 0, 0))]
    args += [q, k_new, v_new, cache, bias_past, bias_new, _head_mask(n_new, F32)]
    out_shape = [jax.ShapeDtypeStruct((n_seq * n_new, Q_WIDTH), out_dtype)]
    out_specs = [pl.BlockSpec((rows, Q_WIDTH), row)]
    if want_lse:
        out_shape.append(jax.ShapeDtypeStruct((n_seq * n_new, LANES), F32))
        out_specs.append(pl.BlockSpec((rows, LANES), row))
    res = pl.pallas_call(
        functools.partial(_sample_body, sinks is not None, want_lse, n_new, seqs),
        grid=(n_seq // seqs,), in_specs=in_specs, out_specs=out_specs, out_shape=out_shape,
        compiler_params=_params(1), name=f"sample_d{dil}")(*args)
    return res[0], (res[1] if want_lse else None)


def _route_tile(logits, tri_ref, counts_ref):
    tm = logits.shape[0]
    lane = lax.broadcasted_iota(jnp.int32, (tm, LANES), 1)
    lane_f = lane.astype(F32)
    none = float(LANES)
    first_lane = lambda hit: jnp.min(jnp.where(hit, lane_f, none), axis=1, keepdims=True)

    gl = jnp.where(lane < N_EXPERT_GROUPS, logits, NEG_INF)
    gmax = jnp.max(gl, axis=1, keepdims=True)
    g_idx = first_lane(gl == gmax)
    g_w = 1.0 / jnp.sum(jnp.exp(gl - gmax), axis=1, keepdims=True)

    e_lane = lane - N_EXPERT_GROUPS
    lane_group = jnp.where((e_lane >= 0) & (e_lane < N_EXPERTS), (e_lane // EXPERTS_PER_GROUP).astype(F32), none)
    el = jnp.where(lane_group == g_idx, logits, NEG_INF)
    v1 = jnp.max(el, axis=1, keepdims=True)
    i1 = first_lane(el == v1)
    el2 = jnp.where(lane_f == i1, NEG_INF, el)
    v2 = jnp.max(el2, axis=1, keepdims=True)
    i2 = first_lane(el2 == v2)
    t = jnp.exp(v2 - v1)
    den = 1.0 + t
    w1 = g_w * (1.0 / den)
    w2 = g_w * (t / den)
    e1 = i1 - float(N_EXPERT_GROUPS)
    e2 = i2 - float(N_EXPERT_GROUPS)

    hot1 = lane_f == e1
    hot2 = lane_f == e2
    both = jnp.where(hot1, 1.0, 0.0) + jnp.where(hot2, 1.0, 0.0)
    before = _dot(tri_ref[...], both.astype(BF16)) + counts_ref[0:1, :]
    r1 = jnp.sum(jnp.where(hot1, before, 0.0), axis=1, keepdims=True)
    r2 = jnp.sum(jnp.where(hot2, before, 0.0), axis=1, keepdims=True)
    counts_ref[...] = counts_ref[...] + jnp.sum(both, axis=0, keepdims=True)

    info = jnp.zeros((tm, LANES), F32)
    for j, col in enumerate((e1, e2, r1, r2, w1, w2)):
        info = jnp.where(lane == j, col, info)
    return info


def _outproj_body(n_groups, n_ptiles, *refs):
    it = iter(refs)
    o_refs = [(next(it), next(it)) for _ in range(n_groups)]
    lse_refs = [(next(it), next(it)) for _ in range(n_groups)] if n_groups > 1 else None
    expand_ref = next(it) if n_groups > 1 else None
    h_ref, wo_ref, g_ref, wr_hi_ref, wr_lo_ref, br_ref, tri_ref = (next(it) for _ in range(7))
    hout_ref, xn_ref, info_ref, counts_ref = next(it), next(it), next(it), next(it)

    @pl.when(pl.program_id(0) == 0)
    def _():
        counts_ref[...] = jnp.zeros(counts_ref.shape, F32)

    is_p = pl.program_id(0) < n_ptiles
    pick = lambda pair: jnp.where(is_p, pair[0][...], pair[1][...])
    if n_groups == 1:
        o = pick(o_refs[0])
    else:
        lses = [pick(p) for p in lse_refs]
        mx = functools.reduce(jnp.maximum, lses)
        es = [jnp.exp(x - mx) for x in lses]
        inv = 1.0 / functools.reduce(lambda a, b: a + b, es)
        acc = None
        for g in range(n_groups):
            w = es[g] * inv
            w_hi = w.astype(BF16)
            w_lo = (w - w_hi.astype(F32)).astype(BF16)
            wide = _dot(w_hi, expand_ref[...]) + _dot(w_lo, expand_ref[...])
            term = wide * pick(o_refs[g])
            acc = term if acc is None else acc + term
        o = acc.astype(BF16)
    h_new = h_ref[...] + _dot(o, wo_ref[...])
    hout_ref[...] = h_new
    xn = _rms(h_new, g_ref[...])
    xn_ref[...] = xn
    x_hi = xn.astype(BF16)
    x_lo = (xn - x_hi.astype(F32)).astype(BF16)
    logits = (_dot(x_hi, wr_hi_ref[...]) + _dot(x_hi, wr_lo_ref[...])
              + _dot(x_lo, wr_hi_ref[...]) + br_ref[...])
    info_ref[...] = _route_tile(logits, tri_ref, counts_ref)


def _out_project(o_pairs, lse_pairs, h, wo, gain, wr_hi, wr_lo, br, n_ptiles):
    tm = ROW_TILE
    n_tok = h.shape[0]
    n_groups = len(o_pairs)
    row = lambda i: (i, 0)
    const = lambda i: (0, 0)
    prow = lambda i: (jnp.minimum(i, n_ptiles - 1), 0)
    srow = lambda i: (jnp.maximum(i - n_ptiles, 0), 0)
    in_specs, args = [], []
    for op, os_ in o_pairs:
        in_specs += [pl.BlockSpec((tm, Q_WIDTH), prow), pl.BlockSpec((tm, Q_WIDTH), srow)]
        args += [op, os_]
    if n_groups > 1:
        for lp, ls in lse_pairs:
            in_specs += [pl.BlockSpec((tm, LANES), prow), pl.BlockSpec((tm, LANES), srow)]
            args += [lp, ls]
        expand = (np.arange(LANES)[:, None] == np.arange(Q_WIDTH)[None, :] // HEAD_DIM)
        in_specs.append(pl.BlockSpec((LANES, Q_WIDTH), const))
        args.append(jnp.asarray(expand.astype(np.float32), BF16))
    in_specs += [pl.BlockSpec((tm, D_MODEL), row), pl.BlockSpec((Q_WIDTH, D_MODEL), const),
                 pl.BlockSpec((1, D_MODEL), const), pl.BlockSpec((D_MODEL, LANES), const),
                 pl.BlockSpec((D_MODEL, LANES), const), pl.BlockSpec((1, LANES), const),
                 pl.BlockSpec((tm, tm), const)]
    strictly_lower = np.tril(np.ones((tm, tm), np.float32), -1)
    args += [h, wo, gain, wr_hi, wr_lo, br, jnp.asarray(strictly_lower, BF16)]
    out_shape = [jax.ShapeDtypeStruct((n_tok, D_MODEL), F32), jax.ShapeDtypeStruct((n_tok, D_MODEL), F32),
                 jax.ShapeDtypeStruct((n_tok, LANES), F32), jax.ShapeDtypeStruct((SUBLANES, LANES), F32)]
    out_specs = [pl.BlockSpec((tm, D_MODEL), row), pl.BlockSpec((tm, D_MODEL), row),
                 pl.BlockSpec((tm, LANES), row), pl.BlockSpec((SUBLANES, LANES), const)]
    return pl.pallas_call(
        functools.partial(_outproj_body, n_groups, n_ptiles),
        grid=(n_tok // tm,), in_specs=in_specs, out_specs=out_specs, out_shape=out_shape,
        compiler_params=_params(1), name=f"outproj_{n_groups}")(*args)


DISPATCH_CHUNK = 256


def _dispatch_body(n_tok, dest_ref, counts_ref, pad_start_ref, xn_hbm, xs_hbm, zero_ref, sem, zero_sem):
    chunk = DISPATCH_CHUNK

    def wait_chunk():
        for _ in range(TOP_K_INNER):
            pltpu.make_async_copy(xn_hbm.at[pl.ds(0, chunk)], xs_hbm.at[pl.ds(0, chunk)], sem).wait()

    def do_chunk(c, carry):
        def body(t, inner):
            tok = c * chunk + t
            for k in range(TOP_K_INNER):
                d = dest_ref[TOP_K_INNER * tok + k]
                pltpu.make_async_copy(xn_hbm.at[pl.ds(tok, 1)], xs_hbm.at[pl.ds(d, 1)], sem).start()
            return inner
        lax.fori_loop(0, chunk, body, 0, unroll=8)

        @pl.when(c > 0)
        def _():
            wait_chunk()
        return carry

    lax.fori_loop(0, n_tok // chunk, do_chunk, 0)
    wait_chunk()

    zero_ref[...] = jnp.zeros(zero_ref.shape, F32)

    def expert(e, carry):
        count = counts_ref[e]
        n_pad = (MOE_BLOCK - count % MOE_BLOCK) % MOE_BLOCK
        base = pad_start_ref[e] + count
        zero_row = lambda r: pltpu.make_async_copy(zero_ref.at[pl.ds(0, 1)], xs_hbm.at[pl.ds(base + r, 1)], zero_sem)

        def start(r, inner):
            zero_row(r).start()
            return inner
        lax.fori_loop(0, n_pad, start, 0)

        def wait(r, inner):
            zero_row(r).wait()
            return inner
        lax.fori_loop(0, n_pad, wait, 0)
        return carry

    lax.fori_loop(0, N_EXPERTS, expert, 0)

    n_blocks = xs_hbm.shape[0] // MOE_BLOCK
    first_free = (pad_start_ref[N_EXPERTS - 1] + counts_ref[N_EXPERTS - 1] + MOE_BLOCK - 1) // MOE_BLOCK
    zero_block = lambda b: pltpu.make_async_copy(
        zero_ref, xs_hbm.at[pl.ds(pl.multiple_of(b * MOE_BLOCK, MOE_BLOCK), MOE_BLOCK)], zero_sem)

    def start_block(b, carry):
        zero_block(b).start()
        return carry
    lax.fori_loop(first_free, n_blocks, start_block, 0)

    def wait_block(b, carry):
        zero_block(b).wait()
        return carry
    lax.fori_loop(first_free, n_blocks, wait_block, 0)


def _dispatch(xn, dest, counts, pad_start, n_slots):
    n_tok = xn.shape[0]
    grid_spec = pltpu.PrefetchScalarGridSpec(
        num_scalar_prefetch=3, grid=(1,),
        in_specs=[pl.BlockSpec(memory_space=pl.ANY)],
        out_specs=pl.BlockSpec(memory_space=pl.ANY),
        scratch_shapes=[pltpu.VMEM((MOE_BLOCK, D_MODEL), F32),
                        pltpu.SemaphoreType.DMA(()), pltpu.SemaphoreType.DMA(())])
    return pl.pallas_call(
        functools.partial(_dispatch_body, n_tok),
        grid_spec=grid_spec, out_shape=jax.ShapeDtypeStruct((n_slots, D_MODEL), F32),
        compiler_params=_params(1), name="moe_dispatch")(dest, counts, pad_start, xn)


def _moe_body(block_e_ref, n_used_ref, x_ref, wg_ref, wu_ref, wd_ref, y_ref, wg_bf, wu_bf, wd_bf):
    i = pl.program_id(0)
    n_used = n_used_ref[0]

    @pl.when(i < n_used)
    def _():
        changed = (i == 0) | (block_e_ref[i] != block_e_ref[jnp.maximum(i - 1, 0)])

        @pl.when(changed)
        def _():
            wg_bf[...] = wg_ref[...].astype(BF16)
            wu_bf[...] = wu_ref[...].astype(BF16)
            wd_bf[...] = wd_ref[...].astype(BF16)

        x = x_ref[...].astype(BF16)
        gate = _dot(x, wg_bf[...])
        up = _dot(x, wu_bf[...])
        mid = (gate * jax.nn.sigmoid(gate) * up).astype(BF16)
        y_ref[...] = _dot(mid, wd_bf[...])

    @pl.when(i >= n_used)
    def _():
        y_ref[...] = jnp.zeros(y_ref.shape, F32)


def _moe_experts(xs, block_e, n_used, w_gate, w_up, w_down, layer):
    n_slots = xs.shape[0]
    wmap = lambda i, be, nu: (layer, be[i], 0, 0)
    grid_spec = pltpu.PrefetchScalarGridSpec(
        num_scalar_prefetch=2, grid=(n_slots // MOE_BLOCK,),
        in_specs=[pl.BlockSpec((MOE_BLOCK, D_MODEL),
                               lambda i, be, nu: (jnp.minimum(i, jnp.maximum(nu[0] - 1, 0)), 0)),
                  pl.BlockSpec((None, None, D_MODEL, D_FF_EXPERT), wmap),
                  pl.BlockSpec((None, None, D_MODEL, D_FF_EXPERT), wmap),
                  pl.BlockSpec((None, None, D_FF_EXPERT, D_MODEL), wmap)],
        out_specs=pl.BlockSpec((MOE_BLOCK, D_MODEL), lambda i, be, nu: (i, 0)),
        scratch_shapes=[pltpu.VMEM((D_MODEL, D_FF_EXPERT), BF16),
                        pltpu.VMEM((D_MODEL, D_FF_EXPERT), BF16),
                        pltpu.VMEM((D_FF_EXPERT, D_MODEL), BF16)])
    return pl.pallas_call(
        _moe_body, grid_spec=grid_spec, out_shape=jax.ShapeDtypeStruct((n_slots, D_MODEL), F32),
        compiler_params=_params(1), name=f"moe_{layer}")(block_e, n_used, xs, w_gate, w_up, w_down)


def _slot_layout(info, counts_tile, n_slots):
    counts = counts_tile[0, :N_EXPERTS].astype(jnp.int32)
    expert = info[:, 0:TOP_K_INNER].astype(jnp.int32)
    rank = info[:, TOP_K_INNER:2 * TOP_K_INNER].astype(jnp.int32)
    padded = (counts + MOE_BLOCK - 1) // MOE_BLOCK * MOE_BLOCK
    pad_end = jnp.cumsum(padded)
    pad_start = pad_end - padded
    hot = expert[..., None] == jnp.arange(N_EXPERTS, dtype=jnp.int32)
    dest = (jnp.sum(jnp.where(hot, pad_start, 0), axis=-1) + rank).reshape(-1)
    block_start = jnp.arange(n_slots // MOE_BLOCK, dtype=jnp.int32) * MOE_BLOCK
    block_e = jnp.minimum(jnp.sum(pad_end[None, :] <= block_start[:, None], axis=1), N_EXPERTS - 1)
    n_used = (pad_end[-1] // MOE_BLOCK).reshape(1)
    return dest.astype(jnp.int32), counts, pad_start.astype(jnp.int32), block_e.astype(jnp.int32), n_used.astype(jnp.int32)


def _start_row_gather(dest_ref, ys_hbm, ybuf, sem, tile, sl):
    def body(t, carry):
        tok = tile * ROW_TILE + t
        for k in range(TOP_K_INNER):
            d = dest_ref[TOP_K_INNER * tok + k]
            pltpu.make_async_copy(ys_hbm.at[pl.ds(d, 1)], ybuf.at[sl, k, pl.ds(t, 1)], sem.at[sl]).start()
        return carry
    lax.fori_loop(0, ROW_TILE, body, 0, unroll=8)


def _moe_output_tile(dest_ref, info_ref, ys_hbm, ybuf, sem, tile0):
    i = pl.program_id(0)
    sl = i % 2

    @pl.when(i == 0)
    def _():
        _start_row_gather(dest_ref, ys_hbm, ybuf, sem, tile0, 0)

    @pl.when(i + 1 < pl.num_programs(0))
    def _():
        _start_row_gather(dest_ref, ys_hbm, ybuf, sem, tile0 + i + 1, 1 - sl)

    for k in range(TOP_K_INNER):
        pltpu.make_async_copy(ys_hbm.at[pl.ds(0, ROW_TILE)], ybuf.at[sl, k], sem.at[sl]).wait()
    info = info_ref[...]
    w_lane = 2 * TOP_K_INNER
    return ybuf[sl, 0] * info[:, w_lane:w_lane + 1] + ybuf[sl, 1] * info[:, w_lane + 1:w_lane + 2]


_COMBINE_SCRATCH = [pltpu.VMEM((2, TOP_K_INNER, ROW_TILE, D_MODEL), F32), pltpu.SemaphoreType.DMA((2,))]


def _final_body(tile0, dest_ref, h_ref, info_ref, ys_hbm, g_ref, o_ref, ybuf, sem):
    y = _moe_output_tile(dest_ref, info_ref, ys_hbm, ybuf, sem, tile0)
    o_ref[...] = _rms(h_ref[...] + y, g_ref[...])


def _final_norm(h, info, ys, dest, gain, tile0, n_tiles):
    tm = ROW_TILE
    grid_spec = pltpu.PrefetchScalarGridSpec(
        num_scalar_prefetch=1, grid=(n_tiles,),
        in_specs=[pl.BlockSpec((tm, D_MODEL), lambda i, d: (i + tile0, 0)),
                  pl.BlockSpec((tm, LANES), lambda i, d: (i + tile0, 0)),
                  pl.BlockSpec(memory_space=pl.ANY),
                  pl.BlockSpec((1, D_MODEL), lambda i, d: (0, 0))],
        out_specs=pl.BlockSpec((tm, D_MODEL), lambda i, d: (i, 0)),
        scratch_shapes=_COMBINE_SCRATCH)
    return pl.pallas_call(
        functools.partial(_final_body, tile0), grid_spec=grid_spec,
        out_shape=jax.ShapeDtypeStruct((n_tiles * tm, D_MODEL), F32),
        compiler_params=_params(1), name="final_norm")(dest, h, info, ys, gain)


def _q_perm():
    g4, h, d = np.meshgrid(np.arange(GQA_GROUP), np.arange(N_KV_HEADS), np.arange(HEAD_DIM), indexing="ij")
    return ((h * GQA_GROUP + g4) * HEAD_DIM + d).reshape(-1)


def _prep_qkv_weight(w, n_groups):
    perm = _q_perm()
    cols = []
    for g in range(n_groups):
        base = g * QKV_WIDTH
        cols.append(w[:, base:base + Q_WIDTH][:, perm] * ATTN_SCALE)
        cols.append(w[:, base + Q_WIDTH:base + QKV_WIDTH])
    return jnp.concatenate(cols, axis=1).astype(BF16)


def _rope_tables(seq_len, past_len, n_new):
    inv_freq = ROPE_THETA ** (-jnp.arange(0, ROT_DIM, 2, dtype=F32) / ROT_DIM)
    pos_p = jnp.arange(seq_len, dtype=jnp.int32)
    pos_s = past_len + (jnp.arange(ROW_TILE, dtype=jnp.int32) % n_new)
    ang = jnp.concatenate([pos_p, pos_s]).astype(F32)[:, None] * inv_freq[None, :]
    cos, sin = jnp.cos(ang), jnp.sin(ang)
    rows = ang.shape[0]
    pad = jnp.zeros((rows, HEAD_DIM - ROT_DIM), F32)
    zero = jnp.zeros((rows, ROT_HALF), F32)
    c64 = jnp.concatenate([cos, cos, pad + 1.0], axis=1)
    s1 = jnp.concatenate([-sin, zero, pad], axis=1)
    s2 = jnp.concatenate([zero, sin, pad], axis=1)
    dup = lambda t: jnp.concatenate([t] * (LANES // HEAD_DIM), axis=1)
    return dup(c64), dup(s1), dup(s2)


def _split_hi_lo(w):
    hi = w.astype(BF16)
    return hi, (w - hi.astype(F32)).astype(BF16)


def _state_rows(k, v, lead):
    kv = jnp.stack([k, v], axis=-2)
    return kv.reshape(lead + (k.shape[-2], 2, N_KV_HEADS, HEAD_DIM))


def kernel(x_prompt, x_sample, cache_a_kv, cache_b_kv0, cache_b_kv1, cache_b_kv2, norm_mix, norm_ffn, norm_final, a_w_qkv, a_sinks, a_w_o, b_w_qkv, b_w_o, moe_router_g_w, moe_router_g_b, moe_router_e_w, moe_router_e_b, moe_w_gate, moe_w_up, moe_w_down):
    n_seq, seq_len, _ = x_prompt.shape
    n_dec, n_new, _ = x_sample.shape
    tp, ts = n_seq * seq_len, n_dec * n_new
    n_tok = tp + ts
    n_ptiles = tp // ROW_TILE
    tiles_per_seq = seq_len // ROW_TILE
    n_slots = (-(-(n_tok * TOP_K_INNER) // MOE_BLOCK) + N_EXPERTS) * MOE_BLOCK
    caches_b = (cache_b_kv0, cache_b_kv1, cache_b_kv2)

    tables = _rope_tables(seq_len, PAST_LEN, n_new)
    perm = _q_perm()
    xp = x_prompt.reshape(tp, D_MODEL)
    xs = x_sample.reshape(ts, D_MODEL)

    def keys_on_lanes(cache):
        return jnp.transpose(cache, (0, 2, 3, 4, 1))

    def router(i):
        w = jnp.concatenate([moe_router_g_w[i], moe_router_e_w[i],
                             jnp.zeros((D_MODEL, LANES - N_EXPERT_GROUPS - N_EXPERTS), F32)], axis=1)
        b = jnp.concatenate([moe_router_g_b[i], moe_router_e_b[i],
                             jnp.zeros((LANES - N_EXPERT_GROUPS - N_EXPERTS,), F32)]).reshape(1, LANES)
        return _split_hi_lo(w) + (b,)

    def moe(i, xn, info, counts_tile):
        dest, counts, pad_start, block_e, n_used = _slot_layout(info, counts_tile, n_slots)
        xs_sorted = _dispatch(xn, dest, counts, pad_start, n_slots)
        return _moe_experts(xs_sorted, block_e, n_used, moe_w_gate, moe_w_up, moe_w_down, i), dest

    def window_state(k, v, n_keep):
        tail = lambda t: jnp.stack([t[(b + 1) * seq_len - n_keep:(b + 1) * seq_len] for b in range(n_seq)])
        return _state_rows(tail(k), tail(v), (n_seq,))[None]

    def new_rows_state(k, v):
        return _state_rows(k[tp:].reshape(n_dec, n_new, KV_WIDTH), v[tp:].reshape(n_dec, n_new, KV_WIDTH),
                           (n_dec,))[None]

    w_a = _prep_qkv_weight(a_w_qkv[0], 1)
    h0, [(q, k, v)] = _project("split", (xp, xs), norm_mix[0:1], w_a, tables, 1, n_tok, n_ptiles, tiles_per_seq)
    sinks = a_sinks[0].astype(F32)
    o_p, _ = _band_attention(q, k, v, sinks, n_seq, seq_len, 1, WINDOW_A - 1, BF16, False)
    o_s, _ = _sample_attention(q[tp:].astype(F32), k[tp:], v[tp:], keys_on_lanes(cache_a_kv[0]), sinks,
                               n_new, WINDOW_A, 1, BF16, False)
    new_a_p = window_state(k, v, min(WINDOW_A, seq_len))
    new_a_s = new_rows_state(k, v)
    wr_hi, wr_lo, br = router(0)
    h1, xn, info, counts_tile = _out_project([(o_p, o_s)], None, h0, a_w_o[0][perm].astype(BF16), norm_ffn[0:1],
                                             wr_hi, wr_lo, br, n_ptiles)
    ys, dest = moe(0, xn, info, counts_tile)

    w_b = _prep_qkv_weight(b_w_qkv[0], len(B_PATTERNS))
    h2, groups = _project("combine", (h1, info, ys, dest), norm_mix[1:2], w_b, tables, len(B_PATTERNS),
                          n_tok, n_ptiles, tiles_per_seq)
    o_pairs, lse_pairs, new_b = [], [], []
    for g, (window, dil) in enumerate(B_PATTERNS):
        q, k, v = groups[g]
        o_p, lse_p = _band_attention(q, k, v, None, n_seq, seq_len, dil, window // dil, F32, True)
        o_s, lse_s = _sample_attention(q[tp:].astype(F32), k[tp:], v[tp:], keys_on_lanes(caches_b[g][0]), None,
                                       n_new, window // dil + 1, dil, F32, True)
        o_pairs.append((o_p, o_s))
        lse_pairs.append((lse_p, lse_s))
        new_b.append(window_state(k, v, min(window, seq_len)))
        new_b.append(new_rows_state(k, v))
    wr_hi, wr_lo, br = router(1)
    h3, xn, info, counts_tile = _out_project(o_pairs, lse_pairs, h2, b_w_o[0][perm].astype(BF16), norm_ffn[1:2],
                                             wr_hi, wr_lo, br, n_ptiles)
    ys, dest = moe(1, xn, info, counts_tile)

    gain = norm_final.reshape(1, D_MODEL)
    y_p = _final_norm(h3, info, ys, dest, gain, 0, n_ptiles).reshape(n_seq, seq_len, D_MODEL)
    y_s = _final_norm(h3, info, ys, dest, gain, n_ptiles, ts // ROW_TILE).reshape(n_dec, n_new, D_MODEL)
    return (y_p, y_s, new_a_p, new_a_s, *new_b)
```

```python
import functools

import numpy as np
import jax
import jax.numpy as jnp
from jax import lax
from jax.experimental import pallas as pl
from jax.experimental.pallas import tpu as pltpu

D_MODEL = 1024
HEAD_DIM = 64
N_KV_HEADS = 4
GQA_GROUP = 4
N_Q_HEADS = N_KV_HEADS * GQA_GROUP
Q_WIDTH = N_Q_HEADS * HEAD_DIM
KV_WIDTH = N_KV_HEADS * HEAD_DIM
QKV_WIDTH = Q_WIDTH + 2 * KV_WIDTH
ROT_DIM = HEAD_DIM // 4
ROT_HALF = ROT_DIM // 2
ROPE_THETA = 500000.0
ATTN_SCALE = HEAD_DIM ** -0.5
WINDOW_A = 128
B_PATTERNS = ((128, 1), (512, 4), (2048, 16))
BAND_BLOCK = 128
N_EXPERT_GROUPS = 8
EXPERTS_PER_GROUP = 8
N_EXPERTS = N_EXPERT_GROUPS * EXPERTS_PER_GROUP
TOP_K_INNER = 2
D_FF_EXPERT = 512
RMS_EPS = 1e-5
PAST_LEN = 16384

LANES = 128
SUBLANES = 8
ROW_TILE = 256
EXPERT_BLOCK = 256
VMEM_LIMIT_BYTES = 56 * 1024 * 1024

BF16 = jnp.bfloat16
F32 = jnp.float32
NEG_INF = float("-inf")


def _params(n_axes, vmem=VMEM_LIMIT_BYTES):
    return pltpu.CompilerParams(dimension_semantics=("arbitrary",) * n_axes,
                                vmem_limit_bytes=vmem)


def _rms(x, g):
    var = jnp.mean(x * x, axis=-1, keepdims=True)
    return x * lax.rsqrt(var + RMS_EPS) * g


def _dot(a, b):
    return jnp.dot(a, b, preferred_element_type=F32)


def _dot_nt(a, b):
    return lax.dot_general(a, b, (((1,), (1,)), ((), ())), preferred_element_type=F32)


def _proj_body(n_groups, mode, n_ptiles, *refs):
    it = iter(refs)
    if mode == "split":
        xp_ref, xs_ref = next(it), next(it)
    else:
        dest_ref, h_ref, info_ref, ys_hbm = next(it), next(it), next(it), next(it)
    g_ref, w_ref, cos_ref, s1_ref, s2_ref = (next(it) for _ in range(5))
    hout_ref = next(it)
    outs = [(next(it), next(it), next(it)) for _ in range(n_groups)]

    if mode == "split":
        is_p = pl.program_id(0) < n_ptiles
        x = jnp.where(is_p, xp_ref[...], xs_ref[...])
    else:
        ybuf, sem = next(it), next(it)
        x = h_ref[...] + _moe_output_tile(dest_ref, info_ref, ys_hbm, ybuf, sem, 0)
    hout_ref[...] = x
    xn = _rms(x, g_ref[...]).astype(BF16)
    c, s1, s2 = cos_ref[...], s1_ref[...], s2_ref[...]

    def rope(y):
        parts = []
        for j in range(y.shape[1] // LANES):
            yb = y[:, LANES * j:LANES * (j + 1)]
            parts.append(yb * c + pltpu.roll(yb, LANES - ROT_HALF, 1) * s1
                         + pltpu.roll(yb, ROT_HALF, 1) * s2)
        return jnp.concatenate(parts, axis=1)

    half = Q_WIDTH // 2
    for g in range(n_groups):
        base = g * QKV_WIDTH
        q_ref, k_ref, v_ref = outs[g]
        for hf in range(2):
            y = _dot(xn, w_ref[:, base + half * hf:base + half * (hf + 1)])
            q_ref[:, half * hf:half * (hf + 1)] = rope(y).astype(BF16)
        y = _dot(xn, w_ref[:, base + Q_WIDTH:base + Q_WIDTH + KV_WIDTH])
        k_ref[...] = rope(y)
        v_ref[...] = _dot(xn, w_ref[:, base + Q_WIDTH + KV_WIDTH:base + QKV_WIDTH])


def _project(mode, xs, gain, w, tables, n_groups, n_tok, n_ptiles, tiles_per_seq):
    tm = ROW_TILE
    n_tiles = n_tok // tm
    row = lambda i, *_: (i, 0)
    const = lambda i, *_: (0, 0)
    tab = lambda i, *_: (jnp.where(i < n_ptiles, i % tiles_per_seq, tiles_per_seq), 0)
    if mode == "split":
        xp, xsamp = xs
        in_specs = [pl.BlockSpec((tm, D_MODEL), lambda i: (jnp.minimum(i, n_ptiles - 1), 0)),
                    pl.BlockSpec((tm, D_MODEL), lambda i: (jnp.maximum(i - n_ptiles, 0), 0))]
        args = [xp, xsamp]
        prefetch, scratch = [], []
    else:
        h, info, ys, dest = xs
        in_specs = [pl.BlockSpec((tm, D_MODEL), row), pl.BlockSpec((tm, LANES), row),
                    pl.BlockSpec(memory_space=pl.ANY)]
        args = [h, info, ys]
        prefetch, scratch = [dest], _COMBINE_SCRATCH
    in_specs += [pl.BlockSpec((1, D_MODEL), const),
                 pl.BlockSpec(w.shape, const),
                 pl.BlockSpec((tm, LANES), tab), pl.BlockSpec((tm, LANES), tab),
                 pl.BlockSpec((tm, LANES), tab)]
    args += [gain, w, *tables]
    out_shape = [jax.ShapeDtypeStruct((n_tok, D_MODEL), F32)]
    out_specs = [pl.BlockSpec((tm, D_MODEL), row)]
    for _ in range(n_groups):
        out_shape += [jax.ShapeDtypeStruct((n_tok, Q_WIDTH), BF16),
                      jax.ShapeDtypeStruct((n_tok, KV_WIDTH), F32),
                      jax.ShapeDtypeStruct((n_tok, KV_WIDTH), F32)]
        out_specs += [pl.BlockSpec((tm, Q_WIDTH), row), pl.BlockSpec((tm, KV_WIDTH), row),
                      pl.BlockSpec((tm, KV_WIDTH), row)]
    grid_spec = pltpu.PrefetchScalarGridSpec(
        num_scalar_prefetch=len(prefetch), grid=(n_tiles,), in_specs=in_specs, out_specs=out_specs,
        scratch_shapes=scratch)
    res = pl.pallas_call(
        functools.partial(_proj_body, n_groups, mode, n_ptiles), grid_spec=grid_spec, out_shape=out_shape,
        compiler_params=_params(1), name=f"proj_{mode}")(*prefetch, *args)
    h_out = res[0]
    groups = [tuple(res[1 + 3 * g:4 + 3 * g]) for g in range(n_groups)]
    return h_out, groups


def _band_body(has_sink, want_lse, *refs):
    it = iter(refs)
    sink_ref = next(it) if has_sink else None
    q_ref, kc_ref, kp_ref, vc_ref, vp_ref, bias_ref, hm_ref = (next(it) for _ in range(7))
    o_ref = next(it)
    lse_ref = next(it) if want_lse else None
    blk = BAND_BLOCK

    q = q_ref[...]
    k = jnp.concatenate([kp_ref[...], kc_ref[...]], axis=0).astype(BF16)
    v = jnp.concatenate([vp_ref[...], vc_ref[...]], axis=0).astype(BF16)
    vaug = jnp.concatenate([v, jnp.ones((2 * blk, LANES), BF16)], axis=1)
    bias = bias_ref[0]
    bias4 = jnp.concatenate([bias] * N_KV_HEADS, axis=0)
    lane_head = lax.broadcasted_iota(jnp.int32, (blk, KV_WIDTH), 1) // HEAD_DIM
    lane = lax.broadcasted_iota(jnp.int32, (blk, LANES), 1)
    lse_tile = jnp.zeros((blk, LANES), F32)
    row_head = lax.broadcasted_iota(jnp.int32, (N_KV_HEADS * blk, 1), 0) // blk

    for g4 in range(GQA_GROUP):
        qg = q[:, KV_WIDTH * g4:KV_WIDTH * (g4 + 1)]
        lhs = jnp.concatenate([qg * hm_ref[h] for h in range(N_KV_HEADS)], axis=0)
        s = _dot_nt(lhs, k) + bias4
        m = jnp.max(s, axis=1, keepdims=True)
        if has_sink:
            sink_col = jnp.zeros((N_KV_HEADS * blk, 1), F32)
            for h in range(N_KV_HEADS):
                sink_col = jnp.where(row_head == h, sink_ref[h * GQA_GROUP + g4], sink_col)
            m = jnp.maximum(m, sink_col)
        p = jnp.exp(s - m)
        pv = _dot(p.astype(BF16), vaug)
        l = pv[:, KV_WIDTH:]
        if has_sink:
            l = l + jnp.exp(sink_col - m)
        inv = 1.0 / l
        on = pv[:, :KV_WIDTH] * jnp.concatenate([inv, inv], axis=1)
        og = jnp.zeros((blk, KV_WIDTH), F32)
        for h in range(N_KV_HEADS):
            og = jnp.where(lane_head == h, on[blk * h:blk * (h + 1)], og)
            if want_lse:
                lse_h = m[blk * h:blk * (h + 1)] + jnp.log(l[blk * h:blk * (h + 1)])
                lse_tile = jnp.where(lane == g4 * N_KV_HEADS + h, lse_h, lse_tile)
        o_ref[:, KV_WIDTH * g4:KV_WIDTH * (g4 + 1)] = og.astype(o_ref.dtype)
    if want_lse:
        lse_ref[...] = lse_tile


def _band_bias(max_dist):
    qi = np.arange(BAND_BLOCK)[:, None]
    mi = np.arange(2 * BAND_BLOCK)[None, :]
    dist = BAND_BLOCK + qi - mi
    ok = (dist >= 0) & (dist <= max_dist)
    first = ok & (mi >= BAND_BLOCK)
    return np.where(np.stack([first, ok]), 0.0, NEG_INF).astype(np.float32)


def _head_mask(rows, dtype):
    lane_head = np.arange(KV_WIDTH)[None, None, :] // HEAD_DIM
    hm = (lane_head == np.arange(N_KV_HEADS)[:, None, None]).astype(np.float32)
    return jnp.asarray(np.broadcast_to(hm, (N_KV_HEADS, rows, KV_WIDTH)), dtype)


def _band_attention(q, k, v, sinks, n_seq, seq_len, dil, max_dist, out_dtype, want_lse):
    blk = BAND_BLOCK
    n_tok = q.shape[0]
    tp = n_seq * seq_len
    nb = seq_len // dil // blk
    q2 = q.reshape(n_tok // dil, dil * Q_WIDTH)
    k2 = k.reshape(n_tok // dil, dil * KV_WIDTH)
    v2 = v.reshape(n_tok // dil, dil * KV_WIDTH)
    cur = lambda b, r, n: (b * nb + n, r)
    prev = lambda b, r, n: (b * nb + jnp.maximum(n - 1, 0), r)
    in_specs, args = [], []
    if sinks is not None:
        in_specs.append(pl.BlockSpec(memory_space=pltpu.SMEM))
        args.append(sinks)
    in_specs += [pl.BlockSpec((blk, Q_WIDTH), cur),
                 pl.BlockSpec((blk, KV_WIDTH), cur), pl.BlockSpec((blk, KV_WIDTH), prev),
                 pl.BlockSpec((blk, KV_WIDTH), cur), pl.BlockSpec((blk, KV_WIDTH), prev),
                 pl.BlockSpec((1, blk, 2 * blk), lambda b, r, n: (jnp.minimum(n, 1), 0, 0)),
                 pl.BlockSpec((N_KV_HEADS, blk, KV_WIDTH), lambda b, r, n: (0, 0, 0))]
    args += [q2, k2, k2, v2, v2, jnp.asarray(_band_bias(max_dist)), _head_mask(blk, BF16)]
    out_shape = [jax.ShapeDtypeStruct((tp // dil, dil * Q_WIDTH), out_dtype)]
    out_specs = [pl.BlockSpec((blk, Q_WIDTH), cur)]
    if want_lse:
        out_shape.append(jax.ShapeDtypeStruct((tp // dil, dil * LANES), F32))
        out_specs.append(pl.BlockSpec((blk, LANES), cur))
    res = pl.pallas_call(
        functools.partial(_band_body, sinks is not None, want_lse),
        grid=(n_seq, dil, nb), in_specs=in_specs, out_specs=out_specs, out_shape=out_shape,
        compiler_params=_params(3), name=f"band_d{dil}")(*args)
    o = res[0].reshape(tp, Q_WIDTH)
    lse = res[1].reshape(tp, LANES) if want_lse else None
    return o, lse


def _sample_body(has_sink, want_lse, n_new, seqs, *refs):
    it = iter(refs)
    sink_ref = next(it) if has_sink else None
    q_ref, kn_ref, vn_ref, cache_ref, bp_ref, bn_ref, hm_ref = (next(it) for _ in range(7))
    o_ref = next(it)
    lse_ref = next(it) if want_lse else None
    n_past = cache_ref.shape[-1]
    rows = N_Q_HEADS * n_new
    lane_head = lax.broadcasted_iota(jnp.int32, (n_new, KV_WIDTH), 1) // HEAD_DIM
    lane = lax.broadcasted_iota(jnp.int32, (n_new, LANES), 1)
    zpad = jnp.zeros((LANES - n_new, KV_WIDTH), F32)
    row_head = lax.broadcasted_iota(jnp.int32, (rows, 1), 0) // n_new

    for sb in range(seqs):
        rs = slice(n_new * sb, n_new * (sb + 1))
        q = q_ref[rs, :]
        lhs = jnp.concatenate(
            [q[:, KV_WIDTH * g4:KV_WIDTH * (g4 + 1)] * hm_ref[h]
             for g4 in range(GQA_GROUP) for h in range(N_KV_HEADS)], axis=0).astype(BF16)
        kt = cache_ref[sb, 0].reshape(KV_WIDTH, n_past).astype(BF16)
        vt = cache_ref[sb, 1].reshape(KV_WIDTH, n_past).astype(BF16)
        kn = jnp.concatenate([kn_ref[rs, :], zpad], axis=0).astype(BF16)
        vn = jnp.concatenate([vn_ref[rs, :], zpad], axis=0).astype(BF16)
        s1 = _dot(lhs, kt) + bp_ref[...]
        s2 = _dot_nt(lhs, kn) + bn_ref[...]
        m = jnp.maximum(jnp.max(s1, axis=1, keepdims=True), jnp.max(s2, axis=1, keepdims=True))
        if has_sink:
            sink_col = jnp.zeros((rows, 1), F32)
            for g4 in range(GQA_GROUP):
                for h in range(N_KV_HEADS):
                    sink_col = jnp.where(row_head == g4 * N_KV_HEADS + h, sink_ref[h * GQA_GROUP + g4], sink_col)
            m = jnp.maximum(m, sink_col)
        p1 = jnp.exp(s1 - m)
        p2 = jnp.exp(s2 - m)
        l = jnp.sum(p1, axis=1, keepdims=True) + jnp.sum(p2, axis=1, keepdims=True)
        if has_sink:
            l = l + jnp.exp(sink_col - m)
        pv = _dot_nt(p1.astype(BF16), vt) + _dot(p2.astype(BF16), vn)
        on = pv * (1.0 / l)
        lse = m + jnp.log(l)
        lse_tile = jnp.zeros((n_new, LANES), F32)
        for g4 in range(GQA_GROUP):
            og = jnp.zeros((n_new, KV_WIDTH), F32)
            for h in range(N_KV_HEADS):
                r0 = (g4 * N_KV_HEADS + h) * n_new
                og = jnp.where(lane_head == h, on[r0:r0 + n_new], og)
                if want_lse:
                    lse_tile = jnp.where(lane == g4 * N_KV_HEADS + h, lse[r0:r0 + n_new], lse_tile)
            o_ref[rs, KV_WIDTH * g4:KV_WIDTH * (g4 + 1)] = og.astype(o_ref.dtype)
        if want_lse:
            lse_ref[rs, :] = lse_tile


def _sample_bias(n_new, n_past, n_keys, dil):
    j = (np.arange(N_Q_HEADS * n_new) % n_new)[:, None]
    span = dil * (n_keys - 1)

    def ok(delta):
        return (delta >= 0) & (delta % dil == 0) & (delta <= span)

    past = ok(n_past + j - np.arange(n_past)[None, :])
    i_new = np.arange(LANES)[None, :]
    new = ok(j - i_new) & (i_new < n_new)
    to_bias = lambda m: jnp.asarray(np.where(m, 0.0, NEG_INF).astype(np.float32))
    return to_bias(past), to_bias(new)


def _sample_attention(q, k_new, v_new, cache, sinks, n_new, n_keys, dil, out_dtype, want_lse):
    n_seq, n_past = cache.shape[0], cache.shape[-1]
    seqs = 4 if n_past <= 512 else 2
    rows = seqs * n_new
    bias_past, bias_new = _sample_bias(n_new, n_past, n_keys, dil)
    n_rows = N_Q_HEADS * n_new
    row = lambda i: (i, 0)
    const = lambda i: (0, 0)
    in_specs, args = [], []
    if sinks is not None:
        in_specs.append(pl.BlockSpec(memory_space=pltpu.SMEM))
        args.append(sinks)
    in_specs += [pl.BlockSpec((rows, Q_WIDTH), row), pl.BlockSpec((rows, KV_WIDTH), row),
                 pl.BlockSpec((rows, KV_WIDTH), row),
                 pl.BlockSpec((seqs, 2, N_KV_HEADS, HEAD_DIM, n_past), lambda i: (i, 0, 0, 0, 0)),
                 pl.BlockSpec((n_rows, n_past), const), pl.BlockSpec((n_rows, LANES), const),
                 pl.BlockSpec((N_KV_HEADS, n_new, KV_WIDTH), lambda i: (0, 0, 0))]
    args += [q, k_new, v_new, cache, bias_past, bias_new, _head_mask(n_new, F32)]
    out_shape = [jax.ShapeDtypeStruct((n_seq * n_new, Q_WIDTH), out_dtype)]
    out_specs = [pl.BlockSpec((rows, Q_WIDTH), row)]
    if want_lse:
        out_shape.append(jax.ShapeDtypeStruct((n_seq * n_new, LANES), F32))
        out_specs.append(pl.BlockSpec((rows, LANES), row))
    res = pl.pallas_call(
        functools.partial(_sample_body, sinks is not None, want_lse, n_new, seqs),
        grid=(n_seq // seqs,), in_specs=in_specs, out_specs=out_specs, out_shape=out_shape,
        compiler_params=_params(1), name=f"sample_d{dil}")(*args)
    return res[0], (res[1] if want_lse else None)


def _route_tile(logits, tri_ref, counts_ref):
    tm = logits.shape[0]
    lane = lax.broadcasted_iota(jnp.int32, (tm, LANES), 1)
    lane_f = lane.astype(F32)
    none = float(LANES)
    first_lane = lambda hit: jnp.min(jnp.where(hit, lane_f, none), axis=1, keepdims=True)

    gl = jnp.where(lane < N_EXPERT_GROUPS, logits, NEG_INF)
    gmax = jnp.max(gl, axis=1, keepdims=True)
    g_idx = first_lane(gl == gmax)
    g_w = 1.0 / jnp.sum(jnp.exp(gl - gmax), axis=1, keepdims=True)

    e_lane = lane - N_EXPERT_GROUPS
    lane_group = jnp.where((e_lane >= 0) & (e_lane < N_EXPERTS), (e_lane // EXPERTS_PER_GROUP).astype(F32), none)
    el = jnp.where(lane_group == g_idx, logits, NEG_INF)
    v1 = jnp.max(el, axis=1, keepdims=True)
    i1 = first_lane(el == v1)
    el2 = jnp.where(lane_f == i1, NEG_INF, el)
    v2 = jnp.max(el2, axis=1, keepdims=True)
    i2 = first_lane(el2 == v2)
    t = jnp.exp(v2 - v1)
    den = 1.0 + t
    w1 = g_w * (1.0 / den)
    w2 = g_w * (t / den)
    e1 = i1 - float(N_EXPERT_GROUPS)
    e2 = i2 - float(N_EXPERT_GROUPS)

    hot1 = lane_f == e1
    hot2 = lane_f == e2
    both = jnp.where(hot1, 1.0, 0.0) + jnp.where(hot2, 1.0, 0.0)
    before = _dot(tri_ref[...], both.astype(BF16)) + counts_ref[0:1, :]
    r1 = jnp.sum(jnp.where(hot1, before, 0.0), axis=1, keepdims=True)
    r2 = jnp.sum(jnp.where(hot2, before, 0.0), axis=1, keepdims=True)
    counts_ref[...] = counts_ref[...] + jnp.sum(both, axis=0, keepdims=True)

    info = jnp.zeros((tm, LANES), F32)
    for j, col in enumerate((e1, e2, r1, r2, w1, w2)):
        info = jnp.where(lane == j, col, info)
    return info


def _outproj_body(n_groups, n_ptiles, *refs):
    it = iter(refs)
    o_refs = [(next(it), next(it)) for _ in range(n_groups)]
    lse_refs = [(next(it), next(it)) for _ in range(n_groups)] if n_groups > 1 else None
    expand_ref = next(it) if n_groups > 1 else None
    h_ref, wo_ref, g_ref, wr_hi_ref, wr_lo_ref, br_ref, tri_ref = (next(it) for _ in range(7))
    hout_ref, xn_ref, info_ref, counts_ref = next(it), next(it), next(it), next(it)

    @pl.when(pl.program_id(0) == 0)
    def _():
        counts_ref[...] = jnp.zeros(counts_ref.shape, F32)

    is_p = pl.program_id(0) < n_ptiles
    pick = lambda pair: jnp.where(is_p, pair[0][...], pair[1][...])
    if n_groups == 1:
        o = pick(o_refs[0])
    else:
        lses = [pick(p) for p in lse_refs]
        mx = functools.reduce(jnp.maximum, lses)
        es = [jnp.exp(x - mx) for x in lses]
        inv = 1.0 / functools.reduce(lambda a, b: a + b, es)
        acc = None
        for g in range(n_groups):
            w = es[g] * inv
            w_hi = w.astype(BF16)
            w_lo = (w - w_hi.astype(F32)).astype(BF16)
            wide = _dot(w_hi, expand_ref[...]) + _dot(w_lo, expand_ref[...])
            term = wide * pick(o_refs[g])
            acc = term if acc is None else acc + term
        o = acc.astype(BF16)
    h_new = h_ref[...] + _dot(o, wo_ref[...])
    hout_ref[...] = h_new
    xn = _rms(h_new, g_ref[...])
    xn_ref[...] = xn
    x_hi = xn.astype(BF16)
    x_lo = (xn - x_hi.astype(F32)).astype(BF16)
    logits = (_dot(x_hi, wr_hi_ref[...]) + _dot(x_hi, wr_lo_ref[...])
              + _dot(x_lo, wr_hi_ref[...]) + br_ref[...])
    info_ref[...] = _route_tile(logits, tri_ref, counts_ref)


def _out_project(o_pairs, lse_pairs, h, wo, gain, wr_hi, wr_lo, br, n_ptiles):
    tm = ROW_TILE
    n_tok = h.shape[0]
    n_groups = len(o_pairs)
    row = lambda i: (i, 0)
    const = lambda i: (0, 0)
    prow = lambda i: (jnp.minimum(i, n_ptiles - 1), 0)
    srow = lambda i: (jnp.maximum(i - n_ptiles, 0), 0)
    in_specs, args = [], []
    for op, os_ in o_pairs:
        in_specs += [pl.BlockSpec((tm, Q_WIDTH), prow), pl.BlockSpec((tm, Q_WIDTH), srow)]
        args += [op, os_]
    if n_groups > 1:
        for lp, ls in lse_pairs:
            in_specs += [pl.BlockSpec((tm, LANES), prow), pl.BlockSpec((tm, LANES), srow)]
            args += [lp, ls]
        expand = (np.arange(LANES)[:, None] == np.arange(Q_WIDTH)[None, :] // HEAD_DIM)
        in_specs.append(pl.BlockSpec((LANES, Q_WIDTH), const))
        args.append(jnp.asarray(expand.astype(np.float32), BF16))
    in_specs += [pl.BlockSpec((tm, D_MODEL), row), pl.BlockSpec((Q_WIDTH, D_MODEL), const),
                 pl.BlockSpec((1, D_MODEL), const), pl.BlockSpec((D_MODEL, LANES), const),
                 pl.BlockSpec((D_MODEL, LANES), const), pl.BlockSpec((1, LANES), const),
                 pl.BlockSpec((tm, tm), const)]
    strictly_lower = np.tril(np.ones((tm, tm), np.float32), -1)
    args += [h, wo, gain, wr_hi, wr_lo, br, jnp.asarray(strictly_lower, BF16)]
    out_shape = [jax.ShapeDtypeStruct((n_tok, D_MODEL), F32), jax.ShapeDtypeStruct((n_tok, D_MODEL), F32),
                 jax.ShapeDtypeStruct((n_tok, LANES), F32), jax.ShapeDtypeStruct((SUBLANES, LANES), F32)]
    out_specs = [pl.BlockSpec((tm, D_MODEL), row), pl.BlockSpec((tm, D_MODEL), row),
                 pl.BlockSpec((tm, LANES), row), pl.BlockSpec((SUBLANES, LANES), const)]
    return pl.pallas_call(
        functools.partial(_outproj_body, n_groups, n_ptiles),
        grid=(n_tok // tm,), in_specs=in_specs, out_specs=out_specs, out_shape=out_shape,
        compiler_params=_params(1), name=f"outproj_{n_groups}")(*args)


DISPATCH_CHUNK = 256


DISPATCH_BUFFERS = 3


def _dispatch_body(n_tok, dest_ref, counts_ref, pad_start_ref, xn_hbm, xs_hbm, buf, zero_ref,
                   sem_in, sem_out, zero_sem):
    chunk = DISPATCH_CHUNK
    n_chunks = n_tok // chunk

    def load(c, sl):
        rows = pl.ds(pl.multiple_of(c * chunk, chunk), chunk)
        return pltpu.make_async_copy(xn_hbm.at[rows], buf.at[sl], sem_in.at[sl])

    def wait_scatter(sl):
        for _ in range(TOP_K_INNER):
            pltpu.make_async_copy(buf.at[sl], xs_hbm.at[pl.ds(0, chunk)], sem_out.at[sl]).wait()

    load(0, 0).start()

    def do_chunk(c, carry):
        sl = c % DISPATCH_BUFFERS
        nxt = (c + 1) % DISPATCH_BUFFERS

        @pl.when(c >= DISPATCH_BUFFERS - 1)
        def _():
            wait_scatter(nxt)

        @pl.when(c + 1 < n_chunks)
        def _():
            load(c + 1, nxt).start()

        load(c, sl).wait()
        for t in range(chunk):
            for k in range(TOP_K_INNER):
                d = dest_ref[TOP_K_INNER * (c * chunk + t) + k]
                pltpu.make_async_copy(buf.at[sl, pl.ds(t, 1)], xs_hbm.at[pl.ds(d, 1)], sem_out.at[sl]).start()
        return carry

    lax.fori_loop(0, n_chunks, do_chunk, 0)
    for c in range(max(n_chunks - DISPATCH_BUFFERS + 1, 0), n_chunks):
        wait_scatter(c % DISPATCH_BUFFERS)

    zero_ref[...] = jnp.zeros(zero_ref.shape, F32)

    def expert(e, carry):
        count = counts_ref[e]
        n_pad = (EXPERT_BLOCK - count % EXPERT_BLOCK) % EXPERT_BLOCK
        base = pad_start_ref[e] + count
        n_single = jnp.minimum((SUBLANES - base % SUBLANES) % SUBLANES, n_pad)
        n_groups = (n_pad - n_single) // SUBLANES
        zero_row = lambda r: pltpu.make_async_copy(zero_ref.at[pl.ds(0, 1)], xs_hbm.at[pl.ds(base + r, 1)], zero_sem)
        zero_group = lambda j: pltpu.make_async_copy(
            zero_ref.at[pl.ds(0, SUBLANES)],
            xs_hbm.at[pl.ds(pl.multiple_of(base + n_single + j * SUBLANES, SUBLANES), SUBLANES)], zero_sem)

        for n, make in ((n_single, zero_row), (n_groups, zero_group)):
            def start(r, inner, make=make):
                make(r).start()
                return inner
            lax.fori_loop(0, n, start, 0)

            def wait(r, inner, make=make):
                make(r).wait()
                return inner
            lax.fori_loop(0, n, wait, 0)
        return carry

    lax.fori_loop(0, N_EXPERTS, expert, 0)

    n_blocks = xs_hbm.shape[0] // EXPERT_BLOCK
    first_free = (pad_start_ref[N_EXPERTS - 1] + counts_ref[N_EXPERTS - 1] + EXPERT_BLOCK - 1) // EXPERT_BLOCK
    zero_block = lambda b: pltpu.make_async_copy(
        zero_ref, xs_hbm.at[pl.ds(pl.multiple_of(b * EXPERT_BLOCK, EXPERT_BLOCK), EXPERT_BLOCK)], zero_sem)

    def start_block(b, carry):
        zero_block(b).start()
        return carry
    lax.fori_loop(first_free, n_blocks, start_block, 0)

    def wait_block(b, carry):
        zero_block(b).wait()
        return carry
    lax.fori_loop(first_free, n_blocks, wait_block, 0)


def _dispatch(xn, dest, counts, pad_start, n_slots):
    n_tok = xn.shape[0]
    grid_spec = pltpu.PrefetchScalarGridSpec(
        num_scalar_prefetch=3, grid=(1,),
        in_specs=[pl.BlockSpec(memory_space=pl.ANY)],
        out_specs=pl.BlockSpec(memory_space=pl.ANY),
        scratch_shapes=[pltpu.VMEM((DISPATCH_BUFFERS, DISPATCH_CHUNK, D_MODEL), F32),
                        pltpu.VMEM((EXPERT_BLOCK, D_MODEL), F32),
                        pltpu.SemaphoreType.DMA((DISPATCH_BUFFERS,)), pltpu.SemaphoreType.DMA((DISPATCH_BUFFERS,)),
                        pltpu.SemaphoreType.DMA(())])
    return pl.pallas_call(
        functools.partial(_dispatch_body, n_tok),
        grid_spec=grid_spec, out_shape=jax.ShapeDtypeStruct((n_slots, D_MODEL), F32),
        compiler_params=_params(1), name="moe_dispatch")(dest, counts, pad_start, xn)


def _moe_body(block_e_ref, n_used_ref, x_ref, wg_ref, wu_ref, wd_ref, y_ref, wg_bf, wu_bf, wd_bf):
    i = pl.program_id(0)
    n_used = n_used_ref[0]

    @pl.when(i < n_used)
    def _():
        changed = (i == 0) | (block_e_ref[i] != block_e_ref[jnp.maximum(i - 1, 0)])

        @pl.when(changed)
        def _():
            wg_bf[...] = wg_ref[...].astype(BF16)
            wu_bf[...] = wu_ref[...].astype(BF16)
            wd_bf[...] = wd_ref[...].astype(BF16)

        x = x_ref[...].astype(BF16)
        gate = _dot(x, wg_bf[...])
        up = _dot(x, wu_bf[...])
        mid = (gate * jax.nn.sigmoid(gate) * up).astype(BF16)
        y_ref[...] = _dot(mid, wd_bf[...])

    @pl.when(i >= n_used)
    def _():
        y_ref[...] = jnp.zeros(y_ref.shape, F32)


def _moe_experts(xs, block_e, n_used, w_gate, w_up, w_down, layer):
    n_slots = xs.shape[0]
    wmap = lambda i, be, nu: (layer, be[i], 0, 0)
    grid_spec = pltpu.PrefetchScalarGridSpec(
        num_scalar_prefetch=2, grid=(n_slots // EXPERT_BLOCK,),
        in_specs=[pl.BlockSpec((EXPERT_BLOCK, D_MODEL),
                               lambda i, be, nu: (jnp.minimum(i, jnp.maximum(nu[0] - 1, 0)), 0)),
                  pl.BlockSpec((None, None, D_MODEL, D_FF_EXPERT), wmap),
                  pl.BlockSpec((None, None, D_MODEL, D_FF_EXPERT), wmap),
                  pl.BlockSpec((None, None, D_FF_EXPERT, D_MODEL), wmap)],
        out_specs=pl.BlockSpec((EXPERT_BLOCK, D_MODEL), lambda i, be, nu: (i, 0)),
        scratch_shapes=[pltpu.VMEM((D_MODEL, D_FF_EXPERT), BF16),
                        pltpu.VMEM((D_MODEL, D_FF_EXPERT), BF16),
                        pltpu.VMEM((D_FF_EXPERT, D_MODEL), BF16)])
    return pl.pallas_call(
        _moe_body, grid_spec=grid_spec, out_shape=jax.ShapeDtypeStruct((n_slots, D_MODEL), F32),
        compiler_params=_params(1), name=f"moe_{layer}")(block_e, n_used, xs, w_gate, w_up, w_down)


def _slot_layout(info, counts_tile, n_slots):
    counts = counts_tile[0, :N_EXPERTS].astype(jnp.int32)
    expert = info[:, 0:TOP_K_INNER].astype(jnp.int32)
    rank = info[:, TOP_K_INNER:2 * TOP_K_INNER].astype(jnp.int32)
    padded = (counts + EXPERT_BLOCK - 1) // EXPERT_BLOCK * EXPERT_BLOCK
    pad_end = jnp.cumsum(padded)
    pad_start = pad_end - padded
    hot = expert[..., None] == jnp.arange(N_EXPERTS, dtype=jnp.int32)
    dest = (jnp.sum(jnp.where(hot, pad_start, 0), axis=-1) + rank).reshape(-1)
    block_start = jnp.arange(n_slots // EXPERT_BLOCK, dtype=jnp.int32) * EXPERT_BLOCK
    block_e = jnp.minimum(jnp.sum(pad_end[None, :] <= block_start[:, None], axis=1), N_EXPERTS - 1)
    n_used = (pad_end[-1] // EXPERT_BLOCK).reshape(1)
    return dest.astype(jnp.int32), counts, pad_start.astype(jnp.int32), block_e.astype(jnp.int32), n_used.astype(jnp.int32)


def _start_row_gather(dest_ref, ys_hbm, ybuf, sem, tile, sl):
    for t in range(ROW_TILE):
        for k in range(TOP_K_INNER):
            d = dest_ref[TOP_K_INNER * (tile * ROW_TILE + t) + k]
            pltpu.make_async_copy(ys_hbm.at[pl.ds(d, 1)], ybuf.at[sl, k, pl.ds(t, 1)], sem.at[sl]).start()


def _moe_output_tile(dest_ref, info_ref, ys_hbm, ybuf, sem, tile0):
    i = pl.program_id(0)
    sl = i % 2

    @pl.when(i == 0)
    def _():
        _start_row_gather(dest_ref, ys_hbm, ybuf, sem, tile0, 0)

    @pl.when(i + 1 < pl.num_programs(0))
    def _():
        _start_row_gather(dest_ref, ys_hbm, ybuf, sem, tile0 + i + 1, 1 - sl)

    for k in range(TOP_K_INNER):
        pltpu.make_async_copy(ys_hbm.at[pl.ds(0, ROW_TILE)], ybuf.at[sl, k], sem.at[sl]).wait()
    info = info_ref[...]
    w_lane = 2 * TOP_K_INNER
    return ybuf[sl, 0] * info[:, w_lane:w_lane + 1] + ybuf[sl, 1] * info[:, w_lane + 1:w_lane + 2]


_COMBINE_SCRATCH = [pltpu.VMEM((2, TOP_K_INNER, ROW_TILE, D_MODEL), F32), pltpu.SemaphoreType.DMA((2,))]


def _final_body(tile0, dest_ref, h_ref, info_ref, ys_hbm, g_ref, o_ref, ybuf, sem):
    y = _moe_output_tile(dest_ref, info_ref, ys_hbm, ybuf, sem, tile0)
    o_ref[...] = _rms(h_ref[...] + y, g_ref[...])


def _final_norm(h, info, ys, dest, gain, tile0, n_tiles):
    tm = ROW_TILE
    grid_spec = pltpu.PrefetchScalarGridSpec(
        num_scalar_prefetch=1, grid=(n_tiles,),
        in_specs=[pl.BlockSpec((tm, D_MODEL), lambda i, d: (i + tile0, 0)),
                  pl.BlockSpec((tm, LANES), lambda i, d: (i + tile0, 0)),
                  pl.BlockSpec(memory_space=pl.ANY),
                  pl.BlockSpec((1, D_MODEL), lambda i, d: (0, 0))],
        out_specs=pl.BlockSpec((tm, D_MODEL), lambda i, d: (i, 0)),
        scratch_shapes=_COMBINE_SCRATCH)
    return pl.pallas_call(
        functools.partial(_final_body, tile0), grid_spec=grid_spec,
        out_shape=jax.ShapeDtypeStruct((n_tiles * tm, D_MODEL), F32),
        compiler_params=_params(1), name="final_norm")(dest, h, info, ys, gain)


def _q_perm():
    g4, h, d = np.meshgrid(np.arange(GQA_GROUP), np.arange(N_KV_HEADS), np.arange(HEAD_DIM), indexing="ij")
    return ((h * GQA_GROUP + g4) * HEAD_DIM + d).reshape(-1)


def _prep_qkv_weight(w, n_groups):
    perm = _q_perm()
    cols = []
    for g in range(n_groups):
        base = g * QKV_WIDTH
        cols.append(w[:, base:base + Q_WIDTH][:, perm] * ATTN_SCALE)
        cols.append(w[:, base + Q_WIDTH:base + QKV_WIDTH])
    return jnp.concatenate(cols, axis=1).astype(BF16)


def _rope_tables(seq_len, past_len, n_new):
    inv_freq = ROPE_THETA ** (-jnp.arange(0, ROT_DIM, 2, dtype=F32) / ROT_DIM)
    pos_p = jnp.arange(seq_len, dtype=jnp.int32)
    pos_s = past_len + (jnp.arange(ROW_TILE, dtype=jnp.int32) % n_new)
    ang = jnp.concatenate([pos_p, pos_s]).astype(F32)[:, None] * inv_freq[None, :]
    cos, sin = jnp.cos(ang), jnp.sin(ang)
    rows = ang.shape[0]
    pad = jnp.zeros((rows, HEAD_DIM - ROT_DIM), F32)
    zero = jnp.zeros((rows, ROT_HALF), F32)
    c64 = jnp.concatenate([cos, cos, pad + 1.0], axis=1)
    s1 = jnp.concatenate([-sin, zero, pad], axis=1)
    s2 = jnp.concatenate([zero, sin, pad], axis=1)
    dup = lambda t: jnp.concatenate([t] * (LANES // HEAD_DIM), axis=1)
    return dup(c64), dup(s1), dup(s2)


def _split_hi_lo(w):
    hi = w.astype(BF16)
    return hi, (w - hi.astype(F32)).astype(BF16)


def _state_rows(k, v, lead):
    kv = jnp.stack([k, v], axis=-2)
    return kv.reshape(lead + (k.shape[-2], 2, N_KV_HEADS, HEAD_DIM))


def kernel(x_prompt, x_sample, cache_a_kv, cache_b_kv0, cache_b_kv1, cache_b_kv2, norm_mix, norm_ffn, norm_final, a_w_qkv, a_sinks, a_w_o, b_w_qkv, b_w_o, moe_router_g_w, moe_router_g_b, moe_router_e_w, moe_router_e_b, moe_w_gate, moe_w_up, moe_w_down):
    n_seq, seq_len, _ = x_prompt.shape
    n_dec, n_new, _ = x_sample.shape
    tp, ts = n_seq * seq_len, n_dec * n_new
    n_tok = tp + ts
    n_ptiles = tp // ROW_TILE
    tiles_per_seq = seq_len // ROW_TILE
    n_slots = (-(-(n_tok * TOP_K_INNER) // EXPERT_BLOCK) + N_EXPERTS) * EXPERT_BLOCK
    caches_b = (cache_b_kv0, cache_b_kv1, cache_b_kv2)

    tables = _rope_tables(seq_len, PAST_LEN, n_new)
    perm = _q_perm()
    xp = x_prompt.reshape(tp, D_MODEL)
    xs = x_sample.reshape(ts, D_MODEL)

    def keys_on_lanes(cache):
        return jnp.transpose(cache, (0, 2, 3, 4, 1))

    def router(i):
        w = jnp.concatenate([moe_router_g_w[i], moe_router_e_w[i],
                             jnp.zeros((D_MODEL, LANES - N_EXPERT_GROUPS - N_EXPERTS), F32)], axis=1)
        b = jnp.concatenate([moe_router_g_b[i], moe_router_e_b[i],
                             jnp.zeros((LANES - N_EXPERT_GROUPS - N_EXPERTS,), F32)]).reshape(1, LANES)
        return _split_hi_lo(w) + (b,)

    def moe(i, xn, info, counts_tile):
        dest, counts, pad_start, block_e, n_used = _slot_layout(info, counts_tile, n_slots)
        xs_sorted = _dispatch(xn, dest, counts, pad_start, n_slots)
        return _moe_experts(xs_sorted, block_e, n_used, moe_w_gate, moe_w_up, moe_w_down, i), dest

    def window_state(k, v, n_keep):
        tail = lambda t: jnp.stack([t[(b + 1) * seq_len - n_keep:(b + 1) * seq_len] for b in range(n_seq)])
        return _state_rows(tail(k), tail(v), (n_seq,))[None]

    def new_rows_state(k, v):
        return _state_rows(k[tp:].reshape(n_dec, n_new, KV_WIDTH), v[tp:].reshape(n_dec, n_new, KV_WIDTH),
                           (n_dec,))[None]

    w_a = _prep_qkv_weight(a_w_qkv[0], 1)
    h0, [(q, k, v)] = _project("split", (xp, xs), norm_mix[0:1], w_a, tables, 1, n_tok, n_ptiles, tiles_per_seq)
    sinks = a_sinks[0].astype(F32)
    o_p, _ = _band_attention(q, k, v, sinks, n_seq, seq_len, 1, WINDOW_A - 1, BF16, False)
    o_s, _ = _sample_attention(q[tp:].astype(F32), k[tp:], v[tp:], keys_on_lanes(cache_a_kv[0]), sinks,
                               n_new, WINDOW_A, 1, BF16, False)
    new_a_p = window_state(k, v, min(WINDOW_A, seq_len))
    new_a_s = new_rows_state(k, v)
    wr_hi, wr_lo, br = router(0)
    h1, xn, info, counts_tile = _out_project([(o_p, o_s)], None, h0, a_w_o[0][perm].astype(BF16), norm_ffn[0:1],
                                             wr_hi, wr_lo, br, n_ptiles)
    ys, dest = moe(0, xn, info, counts_tile)

    w_b = _prep_qkv_weight(b_w_qkv[0], len(B_PATTERNS))
    h2, groups = _project("combine", (h1, info, ys, dest), norm_mix[1:2], w_b, tables, len(B_PATTERNS),
                          n_tok, n_ptiles, tiles_per_seq)
    o_pairs, lse_pairs, new_b = [], [], []
    for g, (window, dil) in enumerate(B_PATTERNS):
        q, k, v = groups[g]
        o_p, lse_p = _band_attention(q, k, v, None, n_seq, seq_len, dil, window // dil, F32, True)
        o_s, lse_s = _sample_attention(q[tp:].astype(F32), k[tp:], v[tp:], keys_on_lanes(caches_b[g][0]), None,
                                       n_new, window // dil + 1, dil, F32, True)
        o_pairs.append((o_p, o_s))
        lse_pairs.append((lse_p, lse_s))
        new_b.append(window_state(k, v, min(window, seq_len)))
        new_b.append(new_rows_state(k, v))
    wr_hi, wr_lo, br = router(1)
    h3, xn, info, counts_tile = _out_project(o_pairs, lse_pairs, h2, b_w_o[0][perm].astype(BF16), norm_ffn[1:2],
                                             wr_hi, wr_lo, br, n_ptiles)
    ys, dest = moe(1, xn, info, counts_tile)

    gain = norm_final.reshape(1, D_MODEL)
    y_p = _final_norm(h3, info, ys, dest, gain, 0, n_ptiles).reshape(n_seq, seq_len, D_MODEL)
    y_s = _final_norm(h3, info, ys, dest, gain, n_ptiles, ts // ROW_TILE).reshape(n_dec, n_new, D_MODEL)
    return (y_p, y_s, new_a_p, new_a_s, *new_b)
```

```python
import functools

import numpy as np
import jax
import jax.numpy as jnp
from jax import lax
from jax.experimental import pallas as pl
from jax.experimental.pallas import tpu as pltpu

D_MODEL = 1024
HEAD_DIM = 64
N_KV_HEADS = 4
GQA_GROUP = 4
N_Q_HEADS = N_KV_HEADS * GQA_GROUP
Q_WIDTH = N_Q_HEADS * HEAD_DIM
KV_WIDTH = N_KV_HEADS * HEAD_DIM
QKV_WIDTH = Q_WIDTH + 2 * KV_WIDTH
ROT_DIM = HEAD_DIM // 4
ROT_HALF = ROT_DIM // 2
ROPE_THETA = 500000.0
ATTN_SCALE = HEAD_DIM ** -0.5
WINDOW_A = 128
B_PATTERNS = ((128, 1), (512, 4), (2048, 16))
BAND_BLOCK = 128
N_EXPERT_GROUPS = 8
EXPERTS_PER_GROUP = 8
N_EXPERTS = N_EXPERT_GROUPS * EXPERTS_PER_GROUP
TOP_K_INNER = 2
D_FF_EXPERT = 512
RMS_EPS = 1e-5
PAST_LEN = 16384

LANES = 128
SUBLANES = 8
ROW_TILE = 256
EXPERT_BLOCK = 256
VMEM_LIMIT_BYTES = 56 * 1024 * 1024

BF16 = jnp.bfloat16
F32 = jnp.float32
NEG_INF = float("-inf")


def _params(n_axes, vmem=VMEM_LIMIT_BYTES):
    return pltpu.CompilerParams(dimension_semantics=("arbitrary",) * n_axes,
                                vmem_limit_bytes=vmem)


def _rms(x, g):
    var = jnp.mean(x * x, axis=-1, keepdims=True)
    return x * lax.rsqrt(var + RMS_EPS) * g


def _dot(a, b):
    return jnp.dot(a, b, preferred_element_type=F32)


def _dot_nt(a, b):
    return lax.dot_general(a, b, (((1,), (1,)), ((), ())), preferred_element_type=F32)


def _proj_body(n_groups, mode, n_ptiles, *refs):
    it = iter(refs)
    if mode == "split":
        xp_ref, xs_ref = next(it), next(it)
    else:
        dest_ref, h_ref, info_ref, ys_hbm = next(it), next(it), next(it), next(it)
    g_ref, w_ref, cos_ref, s1_ref, s2_ref = (next(it) for _ in range(5))
    hout_ref = next(it)
    outs = [(next(it), next(it), next(it)) for _ in range(n_groups)]

    if mode == "split":
        is_p = pl.program_id(0) < n_ptiles
        x = jnp.where(is_p, xp_ref[...], xs_ref[...])
    else:
        ybuf, sem = next(it), next(it)
        x = h_ref[...] + _moe_output_tile(dest_ref, info_ref, ys_hbm, ybuf, sem, 0)
    hout_ref[...] = x
    xn = _rms(x, g_ref[...]).astype(BF16)
    c, s1, s2 = cos_ref[...], s1_ref[...], s2_ref[...]

    def rope(y):
        parts = []
        for j in range(y.shape[1] // LANES):
            yb = y[:, LANES * j:LANES * (j + 1)]
            parts.append(yb * c + pltpu.roll(yb, LANES - ROT_HALF, 1) * s1
                         + pltpu.roll(yb, ROT_HALF, 1) * s2)
        return jnp.concatenate(parts, axis=1)

    half = Q_WIDTH // 2
    for g in range(n_groups):
        base = g * QKV_WIDTH
        q_ref, k_ref, v_ref = outs[g]
        for hf in range(2):
            y = _dot(xn, w_ref[:, base + half * hf:base + half * (hf + 1)])
            q_ref[:, half * hf:half * (hf + 1)] = rope(y).astype(BF16)
        y = _dot(xn, w_ref[:, base + Q_WIDTH:base + Q_WIDTH + KV_WIDTH])
        k_ref[...] = rope(y)
        v_ref[...] = _dot(xn, w_ref[:, base + Q_WIDTH + KV_WIDTH:base + QKV_WIDTH])


def _project(mode, xs, gain, w, tables, n_groups, n_tok, n_ptiles, tiles_per_seq):
    tm = ROW_TILE
    n_tiles = n_tok // tm
    row = lambda i, *_: (i, 0)
    const = lambda i, *_: (0, 0)
    tab = lambda i, *_: (jnp.where(i < n_ptiles, i % tiles_per_seq, tiles_per_seq), 0)
    if mode == "split":
        xp, xsamp = xs
        in_specs = [pl.BlockSpec((tm, D_MODEL), lambda i: (jnp.minimum(i, n_ptiles - 1), 0)),
                    pl.BlockSpec((tm, D_MODEL), lambda i: (jnp.maximum(i - n_ptiles, 0), 0))]
        args = [xp, xsamp]
        prefetch, scratch = [], []
    else:
        h, info, ys, dest = xs
        in_specs = [pl.BlockSpec((tm, D_MODEL), row), pl.BlockSpec((tm, LANES), row),
                    pl.BlockSpec(memory_space=pl.ANY)]
        args = [h, info, ys]
        prefetch, scratch = [dest], _COMBINE_SCRATCH
    in_specs += [pl.BlockSpec((1, D_MODEL), const),
                 pl.BlockSpec(w.shape, const),
                 pl.BlockSpec((tm, LANES), tab), pl.BlockSpec((tm, LANES), tab),
                 pl.BlockSpec((tm, LANES), tab)]
    args += [gain, w, *tables]
    out_shape = [jax.ShapeDtypeStruct((n_tok, D_MODEL), F32)]
    out_specs = [pl.BlockSpec((tm, D_MODEL), row)]
    for _ in range(n_groups):
        out_shape += [jax.ShapeDtypeStruct((n_tok, Q_WIDTH), BF16),
                      jax.ShapeDtypeStruct((n_tok, KV_WIDTH), F32),
                      jax.ShapeDtypeStruct((n_tok, KV_WIDTH), F32)]
        out_specs += [pl.BlockSpec((tm, Q_WIDTH), row), pl.BlockSpec((tm, KV_WIDTH), row),
                      pl.BlockSpec((tm, KV_WIDTH), row)]
    grid_spec = pltpu.PrefetchScalarGridSpec(
        num_scalar_prefetch=len(prefetch), grid=(n_tiles,), in_specs=in_specs, out_specs=out_specs,
        scratch_shapes=scratch)
    res = pl.pallas_call(
        functools.partial(_proj_body, n_groups, mode, n_ptiles), grid_spec=grid_spec, out_shape=out_shape,
        compiler_params=_params(1), name=f"proj_{mode}")(*prefetch, *args)
    h_out = res[0]
    groups = [tuple(res[1 + 3 * g:4 + 3 * g]) for g in range(n_groups)]
    return h_out, groups


def _band_body(has_sink, want_lse, *refs):
    it = iter(refs)
    sink_ref = next(it) if has_sink else None
    q_ref, kc_ref, kp_ref, vc_ref, vp_ref, bias_ref, hm_ref = (next(it) for _ in range(7))
    o_ref = next(it)
    lse_ref = next(it) if want_lse else None
    blk = BAND_BLOCK

    q = q_ref[...]
    k = jnp.concatenate([kp_ref[...], kc_ref[...]], axis=0).astype(BF16)
    v = jnp.concatenate([vp_ref[...], vc_ref[...]], axis=0).astype(BF16)
    bias = bias_ref[0]
    bias4 = jnp.concatenate([bias] * N_KV_HEADS, axis=0)
    lane_head = lax.broadcasted_iota(jnp.int32, (blk, KV_WIDTH), 1) // HEAD_DIM
    lane = lax.broadcasted_iota(jnp.int32, (blk, LANES), 1)
    lse_tile = jnp.zeros((blk, LANES), F32)
    row_head = lax.broadcasted_iota(jnp.int32, (N_KV_HEADS * blk, 1), 0) // blk

    for g4 in range(GQA_GROUP):
        qg = q[:, KV_WIDTH * g4:KV_WIDTH * (g4 + 1)]
        lhs = jnp.concatenate([qg * hm_ref[h] for h in range(N_KV_HEADS)], axis=0)
        s = _dot_nt(lhs, k) + bias4
        m = jnp.max(s, axis=1, keepdims=True)
        if has_sink:
            sink_col = jnp.zeros((N_KV_HEADS * blk, 1), F32)
            for h in range(N_KV_HEADS):
                sink_col = jnp.where(row_head == h, sink_ref[h * GQA_GROUP + g4], sink_col)
            m = jnp.maximum(m, sink_col)
        p = jnp.exp(s - m)
        pv = _dot(p.astype(BF16), v)
        l = jnp.sum(p, axis=1, keepdims=True)
        if has_sink:
            l = l + jnp.exp(sink_col - m)
        on = pv * (1.0 / l)
        og = jnp.zeros((blk, KV_WIDTH), F32)
        for h in range(N_KV_HEADS):
            og = jnp.where(lane_head == h, on[blk * h:blk * (h + 1)], og)
            if want_lse:
                lse_h = m[blk * h:blk * (h + 1)] + jnp.log(l[blk * h:blk * (h + 1)])
                lse_tile = jnp.where(lane == g4 * N_KV_HEADS + h, lse_h, lse_tile)
        o_ref[:, KV_WIDTH * g4:KV_WIDTH * (g4 + 1)] = og.astype(o_ref.dtype)
    if want_lse:
        lse_ref[...] = lse_tile


def _band_bias(max_dist):
    qi = np.arange(BAND_BLOCK)[:, None]
    mi = np.arange(2 * BAND_BLOCK)[None, :]
    dist = BAND_BLOCK + qi - mi
    ok = (dist >= 0) & (dist <= max_dist)
    first = ok & (mi >= BAND_BLOCK)
    return np.where(np.stack([first, ok]), 0.0, NEG_INF).astype(np.float32)


def _head_mask(rows, dtype):
    lane_head = np.arange(KV_WIDTH)[None, None, :] // HEAD_DIM
    hm = (lane_head == np.arange(N_KV_HEADS)[:, None, None]).astype(np.float32)
    return jnp.asarray(np.broadcast_to(hm, (N_KV_HEADS, rows, KV_WIDTH)), dtype)


def _band_attention(q, k, v, sinks, n_seq, seq_len, dil, max_dist, out_dtype, want_lse):
    blk = BAND_BLOCK
    n_tok = q.shape[0]
    tp = n_seq * seq_len
    nb = seq_len // dil // blk
    q2 = q.reshape(n_tok // dil, dil * Q_WIDTH)
    k2 = k.reshape(n_tok // dil, dil * KV_WIDTH)
    v2 = v.reshape(n_tok // dil, dil * KV_WIDTH)
    cur = lambda b, r, n: (b * nb + n, r)
    prev = lambda b, r, n: (b * nb + jnp.maximum(n - 1, 0), r)
    in_specs, args = [], []
    if sinks is not None:
        in_specs.append(pl.BlockSpec(memory_space=pltpu.SMEM))
        args.append(sinks)
    in_specs += [pl.BlockSpec((blk, Q_WIDTH), cur),
                 pl.BlockSpec((blk, KV_WIDTH), cur), pl.BlockSpec((blk, KV_WIDTH), prev),
                 pl.BlockSpec((blk, KV_WIDTH), cur), pl.BlockSpec((blk, KV_WIDTH), prev),
                 pl.BlockSpec((1, blk, 2 * blk), lambda b, r, n: (jnp.minimum(n, 1), 0, 0)),
                 pl.BlockSpec((N_KV_HEADS, blk, KV_WIDTH), lambda b, r, n: (0, 0, 0))]
    args += [q2, k2, k2, v2, v2, jnp.asarray(_band_bias(max_dist)), _head_mask(blk, BF16)]
    out_shape = [jax.ShapeDtypeStruct((tp // dil, dil * Q_WIDTH), out_dtype)]
    out_specs = [pl.BlockSpec((blk, Q_WIDTH), cur)]
    if want_lse:
        out_shape.append(jax.ShapeDtypeStruct((tp // dil, dil * LANES), F32))
        out_specs.append(pl.BlockSpec((blk, LANES), cur))
    res = pl.pallas_call(
        functools.partial(_band_body, sinks is not None, want_lse),
        grid=(n_seq, dil, nb), in_specs=in_specs, out_specs=out_specs, out_shape=out_shape,
        compiler_params=_params(3), name=f"band_d{dil}")(*args)
    o = res[0].reshape(tp, Q_WIDTH)
    lse = res[1].reshape(tp, LANES) if want_lse else None
    return o, lse


def _sample_body(has_sink, want_lse, n_new, seqs, *refs):
    it = iter(refs)
    sink_ref = next(it) if has_sink else None
    q_ref, kn_ref, vn_ref, cache_ref, bp_ref, bn_ref, hm_ref = (next(it) for _ in range(7))
    o_ref = next(it)
    lse_ref = next(it) if want_lse else None
    n_past = cache_ref.shape[-1]
    rows = N_Q_HEADS * n_new
    lane_head = lax.broadcasted_iota(jnp.int32, (n_new, KV_WIDTH), 1) // HEAD_DIM
    lane = lax.broadcasted_iota(jnp.int32, (n_new, LANES), 1)
    zpad = jnp.zeros((LANES - n_new, KV_WIDTH), F32)
    row_head = lax.broadcasted_iota(jnp.int32, (rows, 1), 0) // n_new

    for sb in range(seqs):
        rs = slice(n_new * sb, n_new * (sb + 1))
        q = q_ref[rs, :]
        lhs = jnp.concatenate(
            [q[:, KV_WIDTH * g4:KV_WIDTH * (g4 + 1)] * hm_ref[h]
             for g4 in range(GQA_GROUP) for h in range(N_KV_HEADS)], axis=0).astype(BF16)
        kt = cache_ref[sb, 0].reshape(KV_WIDTH, n_past).astype(BF16)
        vt = cache_ref[sb, 1].reshape(KV_WIDTH, n_past).astype(BF16)
        kn = jnp.concatenate([kn_ref[rs, :], zpad], axis=0).astype(BF16)
        vn = jnp.concatenate([vn_ref[rs, :], zpad], axis=0).astype(BF16)
        s1 = _dot(lhs, kt) + bp_ref[...]
        s2 = _dot_nt(lhs, kn) + bn_ref[...]
        m = jnp.maximum(jnp.max(s1, axis=1, keepdims=True), jnp.max(s2, axis=1, keepdims=True))
        if has_sink:
            sink_col = jnp.zeros((rows, 1), F32)
            for g4 in range(GQA_GROUP):
                for h in range(N_KV_HEADS):
                    sink_col = jnp.where(row_head == g4 * N_KV_HEADS + h, sink_ref[h * GQA_GROUP + g4], sink_col)
            m = jnp.maximum(m, sink_col)
        p1 = jnp.exp(s1 - m)
        p2 = jnp.exp(s2 - m)
        l = jnp.sum(p1, axis=1, keepdims=True) + jnp.sum(p2, axis=1, keepdims=True)
        if has_sink:
            l = l + jnp.exp(sink_col - m)
        pv = _dot_nt(p1.astype(BF16), vt) + _dot(p2.astype(BF16), vn)
        on = pv * (1.0 / l)
        lse = m + jnp.log(l)
        lse_tile = jnp.zeros((n_new, LANES), F32)
        for g4 in range(GQA_GROUP):
            og = jnp.zeros((n_new, KV_WIDTH), F32)
            for h in range(N_KV_HEADS):
                r0 = (g4 * N_KV_HEADS + h) * n_new
                og = jnp.where(lane_head == h, on[r0:r0 + n_new], og)
                if want_lse:
                    lse_tile = jnp.where(lane == g4 * N_KV_HEADS + h, lse[r0:r0 + n_new], lse_tile)
            o_ref[rs, KV_WIDTH * g4:KV_WIDTH * (g4 + 1)] = og.astype(o_ref.dtype)
        if want_lse:
            lse_ref[rs, :] = lse_tile


def _sample_bias(n_new, n_past, n_keys, dil):
    j = (np.arange(N_Q_HEADS * n_new) % n_new)[:, None]
    span = dil * (n_keys - 1)

    def ok(delta):
        return (delta >= 0) & (delta % dil == 0) & (delta <= span)

    past = ok(n_past + j - np.arange(n_past)[None, :])
    i_new = np.arange(LANES)[None, :]
    new = ok(j - i_new) & (i_new < n_new)
    to_bias = lambda m: jnp.asarray(np.where(m, 0.0, NEG_INF).astype(np.float32))
    return to_bias(past), to_bias(new)


def _sample_attention(q, k_new, v_new, cache, sinks, n_new, n_keys, dil, out_dtype, want_lse):
    n_seq, n_past = cache.shape[0], cache.shape[-1]
    seqs = 4 if n_past <= 512 else 2
    rows = seqs * n_new
    bias_past, bias_new = _sample_bias(n_new, n_past, n_keys, dil)
    n_rows = N_Q_HEADS * n_new
    row = lambda i: (i, 0)
    const = lambda i: (0, 0)
    in_specs, args = [], []
    if sinks is not None:
        in_specs.append(pl.BlockSpec(memory_space=pltpu.SMEM))
        args.append(sinks)
    in_specs += [pl.BlockSpec((rows, Q_WIDTH), row), pl.BlockSpec((rows, KV_WIDTH), row),
                 pl.BlockSpec((rows, KV_WIDTH), row),
                 pl.BlockSpec((seqs, 2, N_KV_HEADS, HEAD_DIM, n_past), lambda i: (i, 0, 0, 0, 0)),
                 pl.BlockSpec((n_rows, n_past), const), pl.BlockSpec((n_rows, LANES), const),
                 pl.BlockSpec((N_KV_HEADS, n_new, KV_WIDTH), lambda i: (0, 0, 0))]
    args += [q, k_new, v_new, cache, bias_past, bias_new, _head_mask(n_new, F32)]
    out_shape = [jax.ShapeDtypeStruct((n_seq * n_new, Q_WIDTH), out_dtype)]
    out_specs = [pl.BlockSpec((rows, Q_WIDTH), row)]
    if want_lse:
        out_shape.append(jax.ShapeDtypeStruct((n_seq * n_new, LANES), F32))
        out_specs.append(pl.BlockSpec((rows, LANES), row))
    res = pl.pallas_call(
        functools.partial(_sample_body, sinks is not None, want_lse, n_new, seqs),
        grid=(n_seq // seqs,), in_specs=in_specs, out_specs=out_specs, out_shape=out_shape,
        compiler_params=_params(1), name=f"sample_d{dil}")(*args)
    return res[0], (res[1] if want_lse else None)


def _route_tile(logits, tri_ref, counts_ref):
    tm = logits.shape[0]
    lane = lax.broadcasted_iota(jnp.int32, (tm, LANES), 1)
    lane_f = lane.astype(F32)
    none = float(LANES)
    first_lane = lambda hit: jnp.min(jnp.where(hit, lane_f, none), axis=1, keepdims=True)

    gl = jnp.where(lane < N_EXPERT_GROUPS, logits, NEG_INF)
    gmax = jnp.max(gl, axis=1, keepdims=True)
    g_idx = first_lane(gl == gmax)
    g_w = 1.0 / jnp.sum(jnp.exp(gl - gmax), axis=1, keepdims=True)

    e_lane = lane - N_EXPERT_GROUPS
    lane_group = jnp.where((e_lane >= 0) & (e_lane < N_EXPERTS), (e_lane // EXPERTS_PER_GROUP).astype(F32), none)
    el = jnp.where(lane_group == g_idx, logits, NEG_INF)
    v1 = jnp.max(el, axis=1, keepdims=True)
    i1 = first_lane(el == v1)
    el2 = jnp.where(lane_f == i1, NEG_INF, el)
    v2 = jnp.max(el2, axis=1, keepdims=True)
    i2 = first_lane(el2 == v2)
    t = jnp.exp(v2 - v1)
    den = 1.0 + t
    w1 = g_w * (1.0 / den)
    w2 = g_w * (t / den)
    e1 = i1 - float(N_EXPERT_GROUPS)
    e2 = i2 - float(N_EXPERT_GROUPS)

    hot1 = lane_f == e1
    hot2 = lane_f == e2
    both = jnp.where(hot1, 1.0, 0.0) + jnp.where(hot2, 1.0, 0.0)
    before = _dot(tri_ref[...], both.astype(BF16)) + counts_ref[0:1, :]
    r1 = jnp.sum(jnp.where(hot1, before, 0.0), axis=1, keepdims=True)
    r2 = jnp.sum(jnp.where(hot2, before, 0.0), axis=1, keepdims=True)
    counts_ref[...] = counts_ref[...] + jnp.sum(both, axis=0, keepdims=True)

    info = jnp.zeros((tm, LANES), F32)
    for j, col in enumerate((e1, e2, r1, r2, w1, w2)):
        info = jnp.where(lane == j, col, info)
    return info


def _outproj_body(n_groups, n_ptiles, *refs):
    it = iter(refs)
    o_refs = [(next(it), next(it)) for _ in range(n_groups)]
    lse_refs = [(next(it), next(it)) for _ in range(n_groups)] if n_groups > 1 else None
    expand_ref = next(it) if n_groups > 1 else None
    h_ref, wo_ref, g_ref, wr_hi_ref, wr_lo_ref, br_ref, tri_ref = (next(it) for _ in range(7))
    hout_ref, xn_ref, info_ref, counts_ref = next(it), next(it), next(it), next(it)

    @pl.when(pl.program_id(0) == 0)
    def _():
        counts_ref[...] = jnp.zeros(counts_ref.shape, F32)

    is_p = pl.program_id(0) < n_ptiles
    pick = lambda pair: jnp.where(is_p, pair[0][...], pair[1][...])
    if n_groups == 1:
        o = pick(o_refs[0])
    else:
        lses = [pick(p) for p in lse_refs]
        mx = functools.reduce(jnp.maximum, lses)
        es = [jnp.exp(x - mx) for x in lses]
        inv = 1.0 / functools.reduce(lambda a, b: a + b, es)
        acc = None
        for g in range(n_groups):
            w = es[g] * inv
            w_hi = w.astype(BF16)
            w_lo = (w - w_hi.astype(F32)).astype(BF16)
            wide = _dot(w_hi, expand_ref[...]) + _dot(w_lo, expand_ref[...])
            term = wide * pick(o_refs[g])
            acc = term if acc is None else acc + term
        o = acc.astype(BF16)
    h_new = h_ref[...] + _dot(o, wo_ref[...])
    hout_ref[...] = h_new
    xn = _rms(h_new, g_ref[...])
    xn_ref[...] = xn
    x_hi = xn.astype(BF16)
    x_lo = (xn - x_hi.astype(F32)).astype(BF16)
    logits = (_dot(x_hi, wr_hi_ref[...]) + _dot(x_hi, wr_lo_ref[...])
              + _dot(x_lo, wr_hi_ref[...]) + br_ref[...])
    info_ref[...] = _route_tile(logits, tri_ref, counts_ref)


def _out_project(o_pairs, lse_pairs, h, wo, gain, wr_hi, wr_lo, br, n_ptiles):
    tm = ROW_TILE
    n_tok = h.shape[0]
    n_groups = len(o_pairs)
    row = lambda i: (i, 0)
    const = lambda i: (0, 0)
    prow = lambda i: (jnp.minimum(i, n_ptiles - 1), 0)
    srow = lambda i: (jnp.maximum(i - n_ptiles, 0), 0)
    in_specs, args = [], []
    for op, os_ in o_pairs:
        in_specs += [pl.BlockSpec((tm, Q_WIDTH), prow), pl.BlockSpec((tm, Q_WIDTH), srow)]
        args += [op, os_]
    if n_groups > 1:
        for lp, ls in lse_pairs:
            in_specs += [pl.BlockSpec((tm, LANES), prow), pl.BlockSpec((tm, LANES), srow)]
            args += [lp, ls]
        expand = (np.arange(LANES)[:, None] == np.arange(Q_WIDTH)[None, :] // HEAD_DIM)
        in_specs.append(pl.BlockSpec((LANES, Q_WIDTH), const))
        args.append(jnp.asarray(expand.astype(np.float32), BF16))
    in_specs += [pl.BlockSpec((tm, D_MODEL), row), pl.BlockSpec((Q_WIDTH, D_MODEL), const),
                 pl.BlockSpec((1, D_MODEL), const), pl.BlockSpec((D_MODEL, LANES), const),
                 pl.BlockSpec((D_MODEL, LANES), const), pl.BlockSpec((1, LANES), const),
                 pl.BlockSpec((tm, tm), const)]
    strictly_lower = np.tril(np.ones((tm, tm), np.float32), -1)
    args += [h, wo, gain, wr_hi, wr_lo, br, jnp.asarray(strictly_lower, BF16)]
    out_shape = [jax.ShapeDtypeStruct((n_tok, D_MODEL), F32), jax.ShapeDtypeStruct((n_tok, D_MODEL), F32),
                 jax.ShapeDtypeStruct((n_tok, LANES), F32), jax.ShapeDtypeStruct((SUBLANES, LANES), F32)]
    out_specs = [pl.BlockSpec((tm, D_MODEL), row), pl.BlockSpec((tm, D_MODEL), row),
                 pl.BlockSpec((tm, LANES), row), pl.BlockSpec((SUBLANES, LANES), const)]
    return pl.pallas_call(
        functools.partial(_outproj_body, n_groups, n_ptiles),
        grid=(n_tok // tm,), in_specs=in_specs, out_specs=out_specs, out_shape=out_shape,
        compiler_params=_params(1), name=f"outproj_{n_groups}")(*args)


DISPATCH_CHUNK = 256


DISPATCH_BUFFERS = 3


def _dispatch_body(n_tok, dest_ref, counts_ref, pad_start_ref, xn_hbm, xs_hbm, buf, zero_ref,
                   sem_in, sem_out, zero_sem):
    chunk = DISPATCH_CHUNK
    n_chunks = n_tok // chunk

    def load(c, sl):
        rows = pl.ds(pl.multiple_of(c * chunk, chunk), chunk)
        return pltpu.make_async_copy(xn_hbm.at[rows], buf.at[sl], sem_in.at[sl])

    def wait_scatter(sl):
        for _ in range(TOP_K_INNER):
            pltpu.make_async_copy(buf.at[sl], xs_hbm.at[pl.ds(0, chunk)], sem_out.at[sl]).wait()

    load(0, 0).start()

    def do_chunk(c, carry):
        sl = c % DISPATCH_BUFFERS
        nxt = (c + 1) % DISPATCH_BUFFERS

        @pl.when(c >= DISPATCH_BUFFERS - 1)
        def _():
            wait_scatter(nxt)

        @pl.when(c + 1 < n_chunks)
        def _():
            load(c + 1, nxt).start()

        load(c, sl).wait()
        for t in range(chunk):
            for k in range(TOP_K_INNER):
                d = dest_ref[TOP_K_INNER * (c * chunk + t) + k]
                pltpu.make_async_copy(buf.at[sl, pl.ds(t, 1)], xs_hbm.at[pl.ds(d, 1)], sem_out.at[sl]).start()
        return carry

    lax.fori_loop(0, n_chunks, do_chunk, 0)
    for c in range(max(n_chunks - DISPATCH_BUFFERS + 1, 0), n_chunks):
        wait_scatter(c % DISPATCH_BUFFERS)

    zero_ref[...] = jnp.zeros(zero_ref.shape, F32)

    def expert(e, carry):
        count = counts_ref[e]
        n_pad = (EXPERT_BLOCK - count % EXPERT_BLOCK) % EXPERT_BLOCK
        base = pad_start_ref[e] + count
        n_single = jnp.minimum((SUBLANES - base % SUBLANES) % SUBLANES, n_pad)
        n_groups = (n_pad - n_single) // SUBLANES
        zero_row = lambda r: pltpu.make_async_copy(zero_ref.at[pl.ds(0, 1)], xs_hbm.at[pl.ds(base + r, 1)], zero_sem)
        zero_group = lambda j: pltpu.make_async_copy(
            zero_ref.at[pl.ds(0, SUBLANES)],
            xs_hbm.at[pl.ds(pl.multiple_of(base + n_single + j * SUBLANES, SUBLANES), SUBLANES)], zero_sem)

        for n, make in ((n_single, zero_row), (n_groups, zero_group)):
            def start(r, inner, make=make):
                make(r).start()
                return inner
            lax.fori_loop(0, n, start, 0)

            def wait(r, inner, make=make):
                make(r).wait()
                return inner
            lax.fori_loop(0, n, wait, 0)
        return carry

    lax.fori_loop(0, N_EXPERTS, expert, 0)

    n_blocks = xs_hbm.shape[0] // EXPERT_BLOCK
    first_free = (pad_start_ref[N_EXPERTS - 1] + counts_ref[N_EXPERTS - 1] + EXPERT_BLOCK - 1) // EXPERT_BLOCK
    zero_block = lambda b: pltpu.make_async_copy(
        zero_ref, xs_hbm.at[pl.ds(pl.multiple_of(b * EXPERT_BLOCK, EXPERT_BLOCK), EXPERT_BLOCK)], zero_sem)

    def start_block(b, carry):
        zero_block(b).start()
        return carry
    lax.fori_loop(first_free, n_blocks, start_block, 0)

    def wait_block(b, carry):
        zero_block(b).wait()
        return carry
    lax.fori_loop(first_free, n_blocks, wait_block, 0)


def _dispatch(xn, dest, counts, pad_start, n_slots):
    n_tok = xn.shape[0]
    grid_spec = pltpu.PrefetchScalarGridSpec(
        num_scalar_prefetch=3, grid=(1,),
        in_specs=[pl.BlockSpec(memory_space=pl.ANY)],
        out_specs=pl.BlockSpec(memory_space=pl.ANY),
        scratch_shapes=[pltpu.VMEM((DISPATCH_BUFFERS, DISPATCH_CHUNK, D_MODEL), F32),
                        pltpu.VMEM((EXPERT_BLOCK, D_MODEL), F32),
                        pltpu.SemaphoreType.DMA((DISPATCH_BUFFERS,)), pltpu.SemaphoreType.DMA((DISPATCH_BUFFERS,)),
                        pltpu.SemaphoreType.DMA(())])
    return pl.pallas_call(
        functools.partial(_dispatch_body, n_tok),
        grid_spec=grid_spec, out_shape=jax.ShapeDtypeStruct((n_slots, D_MODEL), F32),
        compiler_params=_params(1), name="moe_dispatch")(dest, counts, pad_start, xn)


def _moe_body(layer, block_e_ref, next_e_ref, parity_ref, n_used_ref, x_ref, wg_hbm, wu_hbm, wd_hbm, y_ref,
              wg_f32, wu_f32, wd_f32, wg_bf, wu_bf, wd_bf, sem):
    i = pl.program_id(0)
    n_used = n_used_ref[0]

    def fetch(e, sl):
        return [pltpu.make_async_copy(src.at[layer, e], dst.at[sl], sem.at[sl])
                for src, dst in ((wg_hbm, wg_f32), (wu_hbm, wu_f32), (wd_hbm, wd_f32))]

    @pl.when((i == 0) & (n_used > 0))
    def _():
        for copy in fetch(block_e_ref[0], parity_ref[0]):
            copy.start()

    @pl.when(i < n_used)
    def _():
        changed = (i == 0) | (block_e_ref[i] != block_e_ref[jnp.maximum(i - 1, 0)])

        @pl.when(changed)
        def _():
            sl = parity_ref[i]
            for copy in fetch(block_e_ref[i], sl):
                copy.wait()

            @pl.when(next_e_ref[i] >= 0)
            def _():
                for copy in fetch(next_e_ref[i], 1 - sl):
                    copy.start()

            wg_bf[...] = wg_f32[sl].astype(BF16)
            wu_bf[...] = wu_f32[sl].astype(BF16)
            wd_bf[...] = wd_f32[sl].astype(BF16)

        x = x_ref[...].astype(BF16)
        gate = _dot(x, wg_bf[...])
        up = _dot(x, wu_bf[...])
        mid = (gate * jax.nn.sigmoid(gate) * up).astype(BF16)
        y_ref[...] = _dot(mid, wd_bf[...])

    @pl.when(i >= n_used)
    def _():
        y_ref[...] = jnp.zeros(y_ref.shape, F32)


def _moe_experts(xs, block_e, next_e, parity, n_used, w_gate, w_up, w_down, layer):
    n_slots = xs.shape[0]
    grid_spec = pltpu.PrefetchScalarGridSpec(
        num_scalar_prefetch=4, grid=(n_slots // EXPERT_BLOCK,),
        in_specs=[pl.BlockSpec((EXPERT_BLOCK, D_MODEL),
                               lambda i, be, ne, pa, nu: (jnp.minimum(i, jnp.maximum(nu[0] - 1, 0)), 0)),
                  pl.BlockSpec(memory_space=pl.ANY), pl.BlockSpec(memory_space=pl.ANY),
                  pl.BlockSpec(memory_space=pl.ANY)],
        out_specs=pl.BlockSpec((EXPERT_BLOCK, D_MODEL), lambda i, be, ne, pa, nu: (i, 0)),
        scratch_shapes=[pltpu.VMEM((2, D_MODEL, D_FF_EXPERT), F32),
                        pltpu.VMEM((2, D_MODEL, D_FF_EXPERT), F32),
                        pltpu.VMEM((2, D_FF_EXPERT, D_MODEL), F32),
                        pltpu.VMEM((D_MODEL, D_FF_EXPERT), BF16),
                        pltpu.VMEM((D_MODEL, D_FF_EXPERT), BF16),
                        pltpu.VMEM((D_FF_EXPERT, D_MODEL), BF16),
                        pltpu.SemaphoreType.DMA((2,))])
    return pl.pallas_call(
        functools.partial(_moe_body, layer), grid_spec=grid_spec,
        out_shape=jax.ShapeDtypeStruct((n_slots, D_MODEL), F32),
        compiler_params=_params(1), name=f"moe_{layer}")(
            block_e, next_e, parity, n_used, xs, w_gate, w_up, w_down)


def _slot_layout(info, counts_tile, n_slots):
    counts = counts_tile[0, :N_EXPERTS].astype(jnp.int32)
    expert = info[:, 0:TOP_K_INNER].astype(jnp.int32)
    rank = info[:, TOP_K_INNER:2 * TOP_K_INNER].astype(jnp.int32)
    padded = (counts + EXPERT_BLOCK - 1) // EXPERT_BLOCK * EXPERT_BLOCK
    pad_end = jnp.cumsum(padded)
    pad_start = pad_end - padded
    hot = expert[..., None] == jnp.arange(N_EXPERTS, dtype=jnp.int32)
    dest = (jnp.sum(jnp.where(hot, pad_start, 0), axis=-1) + rank).reshape(-1)
    block_start = jnp.arange(n_slots // EXPERT_BLOCK, dtype=jnp.int32) * EXPERT_BLOCK
    block_e = jnp.minimum(jnp.sum(pad_end[None, :] <= block_start[:, None], axis=1), N_EXPERTS - 1)
    n_used = pad_end[-1] // EXPERT_BLOCK
    own_end = jnp.sum(jnp.where(block_e[:, None] == jnp.arange(N_EXPERTS), pad_end, 0), axis=1)
    after = own_end // EXPERT_BLOCK
    next_e = jnp.where(after < n_used, jnp.sum(pad_end[None, :] <= (after * EXPERT_BLOCK)[:, None], axis=1), -1)
    changed = jnp.concatenate([jnp.zeros((1,), jnp.int32), (block_e[1:] != block_e[:-1]).astype(jnp.int32)])
    parity = jnp.cumsum(changed) % 2
    i32 = lambda a: a.astype(jnp.int32)
    return (i32(dest), counts, i32(pad_start), i32(block_e), i32(next_e), i32(parity), i32(n_used).reshape(1))


def _start_row_gather(dest_ref, ys_hbm, ybuf, sem, tile, sl):
    for t in range(ROW_TILE):
        for k in range(TOP_K_INNER):
            d = dest_ref[TOP_K_INNER * (tile * ROW_TILE + t) + k]
            pltpu.make_async_copy(ys_hbm.at[pl.ds(d, 1)], ybuf.at[sl, k, pl.ds(t, 1)], sem.at[sl]).start()


def _moe_output_tile(dest_ref, info_ref, ys_hbm, ybuf, sem, tile0):
    i = pl.program_id(0)
    sl = i % 2

    @pl.when(i == 0)
    def _():
        _start_row_gather(dest_ref, ys_hbm, ybuf, sem, tile0, 0)

    @pl.when(i + 1 < pl.num_programs(0))
    def _():
        _start_row_gather(dest_ref, ys_hbm, ybuf, sem, tile0 + i + 1, 1 - sl)

    for k in range(TOP_K_INNER):
        pltpu.make_async_copy(ys_hbm.at[pl.ds(0, ROW_TILE)], ybuf.at[sl, k], sem.at[sl]).wait()
    info = info_ref[...]
    w_lane = 2 * TOP_K_INNER
    return ybuf[sl, 0] * info[:, w_lane:w_lane + 1] + ybuf[sl, 1] * info[:, w_lane + 1:w_lane + 2]


_COMBINE_SCRATCH = [pltpu.VMEM((2, TOP_K_INNER, ROW_TILE, D_MODEL), F32), pltpu.SemaphoreType.DMA((2,))]


def _final_body(tile0, dest_ref, h_ref, info_ref, ys_hbm, g_ref, o_ref, ybuf, sem):
    y = _moe_output_tile(dest_ref, info_ref, ys_hbm, ybuf, sem, tile0)
    o_ref[...] = _rms(h_ref[...] + y, g_ref[...])


def _final_norm(h, info, ys, dest, gain, tile0, n_tiles):
    tm = ROW_TILE
    grid_spec = pltpu.PrefetchScalarGridSpec(
        num_scalar_prefetch=1, grid=(n_tiles,),
        in_specs=[pl.BlockSpec((tm, D_MODEL), lambda i, d: (i + tile0, 0)),
                  pl.BlockSpec((tm, LANES), lambda i, d: (i + tile0, 0)),
                  pl.BlockSpec(memory_space=pl.ANY),
                  pl.BlockSpec((1, D_MODEL), lambda i, d: (0, 0))],
        out_specs=pl.BlockSpec((tm, D_MODEL), lambda i, d: (i, 0)),
        scratch_shapes=_COMBINE_SCRATCH)
    return pl.pallas_call(
        functools.partial(_final_body, tile0), grid_spec=grid_spec,
        out_shape=jax.ShapeDtypeStruct((n_tiles * tm, D_MODEL), F32),
        compiler_params=_params(1), name="final_norm")(dest, h, info, ys, gain)


def _q_perm():
    g4, h, d = np.meshgrid(np.arange(GQA_GROUP), np.arange(N_KV_HEADS), np.arange(HEAD_DIM), indexing="ij")
    return ((h * GQA_GROUP + g4) * HEAD_DIM + d).reshape(-1)


def _prep_qkv_weight(w, n_groups):
    perm = _q_perm()
    cols = []
    for g in range(n_groups):
        base = g * QKV_WIDTH
        cols.append(w[:, base:base + Q_WIDTH][:, perm] * ATTN_SCALE)
        cols.append(w[:, base + Q_WIDTH:base + QKV_WIDTH])
    return jnp.concatenate(cols, axis=1).astype(BF16)


def _rope_tables(seq_len, past_len, n_new):
    inv_freq = ROPE_THETA ** (-jnp.arange(0, ROT_DIM, 2, dtype=F32) / ROT_DIM)
    pos_p = jnp.arange(seq_len, dtype=jnp.int32)
    pos_s = past_len + (jnp.arange(ROW_TILE, dtype=jnp.int32) % n_new)
    ang = jnp.concatenate([pos_p, pos_s]).astype(F32)[:, None] * inv_freq[None, :]
    cos, sin = jnp.cos(ang), jnp.sin(ang)
    rows = ang.shape[0]
    pad = jnp.zeros((rows, HEAD_DIM - ROT_DIM), F32)
    zero = jnp.zeros((rows, ROT_HALF), F32)
    c64 = jnp.concatenate([cos, cos, pad + 1.0], axis=1)
    s1 = jnp.concatenate([-sin, zero, pad], axis=1)
    s2 = jnp.concatenate([zero, sin, pad], axis=1)
    dup = lambda t: jnp.concatenate([t] * (LANES // HEAD_DIM), axis=1)
    return dup(c64), dup(s1), dup(s2)


def _split_hi_lo(w):
    hi = w.astype(BF16)
    return hi, (w - hi.astype(F32)).astype(BF16)


def _state_rows(k, v, lead):
    kv = jnp.stack([k, v], axis=-2)
    return kv.reshape(lead + (k.shape[-2], 2, N_KV_HEADS, HEAD_DIM))


def kernel(x_prompt, x_sample, cache_a_kv, cache_b_kv0, cache_b_kv1, cache_b_kv2, norm_mix, norm_ffn, norm_final, a_w_qkv, a_sinks, a_w_o, b_w_qkv, b_w_o, moe_router_g_w, moe_router_g_b, moe_router_e_w, moe_router_e_b, moe_w_gate, moe_w_up, moe_w_down):
    n_seq, seq_len, _ = x_prompt.shape
    n_dec, n_new, _ = x_sample.shape
    tp, ts = n_seq * seq_len, n_dec * n_new
    n_tok = tp + ts
    n_ptiles = tp // ROW_TILE
    tiles_per_seq = seq_len // ROW_TILE
    n_slots = (-(-(n_tok * TOP_K_INNER) // EXPERT_BLOCK) + N_EXPERTS) * EXPERT_BLOCK
    caches_b = (cache_b_kv0, cache_b_kv1, cache_b_kv2)

    tables = _rope_tables(seq_len, PAST_LEN, n_new)
    perm = _q_perm()
    xp = x_prompt.reshape(tp, D_MODEL)
    xs = x_sample.reshape(ts, D_MODEL)

    def keys_on_lanes(cache):
        return jnp.transpose(cache, (0, 2, 3, 4, 1))

    def router(i):
        w = jnp.concatenate([moe_router_g_w[i], moe_router_e_w[i],
                             jnp.zeros((D_MODEL, LANES - N_EXPERT_GROUPS - N_EXPERTS), F32)], axis=1)
        b = jnp.concatenate([moe_router_g_b[i], moe_router_e_b[i],
                             jnp.zeros((LANES - N_EXPERT_GROUPS - N_EXPERTS,), F32)]).reshape(1, LANES)
        return _split_hi_lo(w) + (b,)

    def moe(i, xn, info, counts_tile):
        dest, counts, pad_start, block_e, next_e, parity, n_used = _slot_layout(info, counts_tile, n_slots)
        xs_sorted = _dispatch(xn, dest, counts, pad_start, n_slots)
        return _moe_experts(xs_sorted, block_e, next_e, parity, n_used, moe_w_gate, moe_w_up, moe_w_down, i), dest

    def window_state(k, v, n_keep):
        tail = lambda t: jnp.stack([t[(b + 1) * seq_len - n_keep:(b + 1) * seq_len] for b in range(n_seq)])
        return _state_rows(tail(k), tail(v), (n_seq,))[None]

    def new_rows_state(k, v):
        return _state_rows(k[tp:].reshape(n_dec, n_new, KV_WIDTH), v[tp:].reshape(n_dec, n_new, KV_WIDTH),
                           (n_dec,))[None]

    w_a = _prep_qkv_weight(a_w_qkv[0], 1)
    h0, [(q, k, v)] = _project("split", (xp, xs), norm_mix[0:1], w_a, tables, 1, n_tok, n_ptiles, tiles_per_seq)
    sinks = a_sinks[0].astype(F32)
    o_p, _ = _band_attention(q, k, v, sinks, n_seq, seq_len, 1, WINDOW_A - 1, BF16, False)
    o_s, _ = _sample_attention(q[tp:].astype(F32), k[tp:], v[tp:], keys_on_lanes(cache_a_kv[0]), sinks,
                               n_new, WINDOW_A, 1, BF16, False)
    new_a_p = window_state(k, v, min(WINDOW_A, seq_len))
    new_a_s = new_rows_state(k, v)
    wr_hi, wr_lo, br = router(0)
    h1, xn, info, counts_tile = _out_project([(o_p, o_s)], None, h0, a_w_o[0][perm].astype(BF16), norm_ffn[0:1],
                                             wr_hi, wr_lo, br, n_ptiles)
    ys, dest = moe(0, xn, info, counts_tile)

    w_b = _prep_qkv_weight(b_w_qkv[0], len(B_PATTERNS))
    h2, groups = _project("combine", (h1, info, ys, dest), norm_mix[1:2], w_b, tables, len(B_PATTERNS),
                          n_tok, n_ptiles, tiles_per_seq)
    o_pairs, lse_pairs, new_b = [], [], []
    for g, (window, dil) in enumerate(B_PATTERNS):
        q, k, v = groups[g]
        o_p, lse_p = _band_attention(q, k, v, None, n_seq, seq_len, dil, window // dil, F32, True)
        o_s, lse_s = _sample_attention(q[tp:].astype(F32), k[tp:], v[tp:], keys_on_lanes(caches_b[g][0]), None,
                                       n_new, window // dil + 1, dil, F32, True)
        o_pairs.append((o_p, o_s))
        lse_pairs.append((lse_p, lse_s))
        new_b.append(window_state(k, v, min(window, seq_len)))
        new_b.append(new_rows_state(k, v))
    wr_hi, wr_lo, br = router(1)
    h3, xn, info, counts_tile = _out_project(o_pairs, lse_pairs, h2, b_w_o[0][perm].astype(BF16), norm_ffn[1:2],
                                             wr_hi, wr_lo, br, n_ptiles)
    ys, dest = moe(1, xn, info, counts_tile)

    gain = norm_final.reshape(1, D_MODEL)
    y_p = _final_norm(h3, info, ys, dest, gain, 0, n_ptiles).reshape(n_seq, seq_len, D_MODEL)
    y_s = _final_norm(h3, info, ys, dest, gain, n_ptiles, ts // ROW_TILE).reshape(n_dec, n_new, D_MODEL)
    return (y_p, y_s, new_a_p, new_a_s, *new_b)
```

```python
import functools

import numpy as np
import jax
import jax.numpy as jnp
from jax import lax
from jax.experimental import pallas as pl
from jax.experimental.pallas import tpu as pltpu

D_MODEL = 1024
HEAD_DIM = 64
N_KV_HEADS = 4
GQA_GROUP = 4
N_Q_HEADS = N_KV_HEADS * GQA_GROUP
Q_WIDTH = N_Q_HEADS * HEAD_DIM
KV_WIDTH = N_KV_HEADS * HEAD_DIM
QKV_WIDTH = Q_WIDTH + 2 * KV_WIDTH
ROT_DIM = HEAD_DIM // 4
ROT_HALF = ROT_DIM // 2
ROPE_THETA = 500000.0
ATTN_SCALE = HEAD_DIM ** -0.5
WINDOW_A = 128
B_PATTERNS = ((128, 1), (512, 4), (2048, 16))
BAND_BLOCK = 128
N_EXPERT_GROUPS = 8
EXPERTS_PER_GROUP = 8
N_EXPERTS = N_EXPERT_GROUPS * EXPERTS_PER_GROUP
TOP_K_INNER = 2
D_FF_EXPERT = 512
RMS_EPS = 1e-5
PAST_LEN = 16384

LANES = 128
SUBLANES = 8
ROW_TILE = 256
EXPERT_BLOCK = 256
VMEM_LIMIT_BYTES = 56 * 1024 * 1024

BF16 = jnp.bfloat16
F32 = jnp.float32
NEG_INF = float("-inf")


def _params(n_axes, vmem=VMEM_LIMIT_BYTES):
    return pltpu.CompilerParams(dimension_semantics=("arbitrary",) * n_axes,
                                vmem_limit_bytes=vmem)


def _rms(x, g):
    var = jnp.mean(x * x, axis=-1, keepdims=True)
    return x * lax.rsqrt(var + RMS_EPS) * g


def _dot(a, b):
    return jnp.dot(a, b, preferred_element_type=F32)


def _dot_nt(a, b):
    return lax.dot_general(a, b, (((1,), (1,)), ((), ())), preferred_element_type=F32)


def _proj_body(dils, mode, n_ptiles, *refs):
    n_groups = len(dils)
    it = iter(refs)
    if mode == "split":
        xp_ref, xs_ref = next(it), next(it)
    else:
        dest_ref, h_ref, info_ref, ys_hbm = next(it), next(it), next(it), next(it)
    g_ref, w_ref, cos_ref, s1_ref, s2_ref = (next(it) for _ in range(5))
    hout_ref = next(it)
    outs = [(next(it), next(it), next(it)) for _ in range(n_groups)]
    stage = next(it) if max(dils) > 1 else None

    if mode == "split":
        is_p = pl.program_id(0) < n_ptiles
        x = jnp.where(is_p, xp_ref[...], xs_ref[...])
    else:
        ybuf, sem = next(it), next(it)
        x = h_ref[...] + _moe_output_tile(dest_ref, info_ref, ys_hbm, ybuf, sem, 0)
    hout_ref[...] = x
    xn = _rms(x, g_ref[...]).astype(BF16)
    c, s1, s2 = cos_ref[...], s1_ref[...], s2_ref[...]

    def rope(y):
        parts = []
        for j in range(y.shape[1] // LANES):
            yb = y[:, LANES * j:LANES * (j + 1)]
            parts.append(yb * c + pltpu.roll(yb, LANES - ROT_HALF, 1) * s1
                         + pltpu.roll(yb, ROT_HALF, 1) * s2)
        return jnp.concatenate(parts, axis=1)

    def emit(out_ref, y, dil, width, col):
        n = y.shape[1]
        if dil == 1:
            out_ref[:, col:col + n] = y.astype(out_ref.dtype)
            return
        rows = y.shape[0] // dil
        for j in range(n // LANES):
            stage[j] = y[:, LANES * j:LANES * (j + 1)]
        for r in range(dil):
            for j in range(n // LANES):
                c0 = r * width + col + LANES * j
                out_ref[:, c0:c0 + LANES] = stage[j, pl.ds(r, rows, stride=dil), :].astype(out_ref.dtype)

    half = Q_WIDTH // 2
    for g in range(n_groups):
        base = g * QKV_WIDTH
        q_ref, k_ref, v_ref = outs[g]
        for hf in range(2):
            y = _dot(xn, w_ref[:, base + half * hf:base + half * (hf + 1)])
            emit(q_ref, rope(y), dils[g], Q_WIDTH, half * hf)
        y = _dot(xn, w_ref[:, base + Q_WIDTH:base + Q_WIDTH + KV_WIDTH])
        emit(k_ref, rope(y), dils[g], KV_WIDTH, 0)
        y = _dot(xn, w_ref[:, base + Q_WIDTH + KV_WIDTH:base + QKV_WIDTH])
        emit(v_ref, y, dils[g], KV_WIDTH, 0)


def _project(mode, xs, gain, w, tables, dils, n_tok, n_ptiles, tiles_per_seq):
    tm = ROW_TILE
    n_tiles = n_tok // tm
    row = lambda i, *_: (i, 0)
    const = lambda i, *_: (0, 0)
    tab = lambda i, *_: (jnp.where(i < n_ptiles, i % tiles_per_seq, tiles_per_seq), 0)
    if mode == "split":
        xp, xsamp = xs
        in_specs = [pl.BlockSpec((tm, D_MODEL), lambda i: (jnp.minimum(i, n_ptiles - 1), 0)),
                    pl.BlockSpec((tm, D_MODEL), lambda i: (jnp.maximum(i - n_ptiles, 0), 0))]
        args = [xp, xsamp]
        prefetch, scratch = [], []
    else:
        h, info, ys, dest = xs
        in_specs = [pl.BlockSpec((tm, D_MODEL), row), pl.BlockSpec((tm, LANES), row),
                    pl.BlockSpec(memory_space=pl.ANY)]
        args = [h, info, ys]
        prefetch, scratch = [dest], _COMBINE_SCRATCH
    if max(dils) > 1:
        scratch = [pltpu.VMEM((Q_WIDTH // 2 // LANES, tm, LANES), F32)] + list(scratch)
    in_specs += [pl.BlockSpec((1, D_MODEL), const),
                 pl.BlockSpec(w.shape, const),
                 pl.BlockSpec((tm, LANES), tab), pl.BlockSpec((tm, LANES), tab),
                 pl.BlockSpec((tm, LANES), tab)]
    args += [gain, w, *tables]
    out_shape = [jax.ShapeDtypeStruct((n_tok, D_MODEL), F32)]
    out_specs = [pl.BlockSpec((tm, D_MODEL), row)]
    for d in dils:
        for width, dtype in ((Q_WIDTH, BF16), (KV_WIDTH, F32), (KV_WIDTH, F32)):
            out_shape.append(jax.ShapeDtypeStruct((n_tok // d, d * width), dtype))
            out_specs.append(pl.BlockSpec((tm // d, d * width), row))
    grid_spec = pltpu.PrefetchScalarGridSpec(
        num_scalar_prefetch=len(prefetch), grid=(n_tiles,), in_specs=in_specs, out_specs=out_specs,
        scratch_shapes=scratch)
    res = pl.pallas_call(
        functools.partial(_proj_body, tuple(dils), mode, n_ptiles), grid_spec=grid_spec, out_shape=out_shape,
        compiler_params=_params(1), name=f"proj_{mode}")(*prefetch, *args)
    h_out = res[0]
    groups = [tuple(res[1 + 3 * g:4 + 3 * g]) for g in range(len(dils))]
    return h_out, groups


def _band_body(has_sink, want_lse, *refs):
    it = iter(refs)
    sink_ref = next(it) if has_sink else None
    q_ref, kc_ref, kp_ref, vc_ref, vp_ref, bias_ref, hm_ref = (next(it) for _ in range(7))
    o_ref = next(it)
    lse_ref = next(it) if want_lse else None
    blk = BAND_BLOCK

    q = q_ref[...]
    k = jnp.concatenate([kp_ref[...], kc_ref[...]], axis=0).astype(BF16)
    v = jnp.concatenate([vp_ref[...], vc_ref[...]], axis=0).astype(BF16)
    bias = bias_ref[0]
    bias4 = jnp.concatenate([bias] * N_KV_HEADS, axis=0)
    lane_head = lax.broadcasted_iota(jnp.int32, (blk, KV_WIDTH), 1) // HEAD_DIM
    lane = lax.broadcasted_iota(jnp.int32, (blk, LANES), 1)
    lse_tile = jnp.zeros((blk, LANES), F32)
    row_head = lax.broadcasted_iota(jnp.int32, (N_KV_HEADS * blk, 1), 0) // blk

    for g4 in range(GQA_GROUP):
        qg = q[:, KV_WIDTH * g4:KV_WIDTH * (g4 + 1)]
        lhs = jnp.concatenate([qg * hm_ref[h] for h in range(N_KV_HEADS)], axis=0)
        s = _dot_nt(lhs, k) + bias4
        m = jnp.max(s, axis=1, keepdims=True)
        if has_sink:
            sink_col = jnp.zeros((N_KV_HEADS * blk, 1), F32)
            for h in range(N_KV_HEADS):
                sink_col = jnp.where(row_head == h, sink_ref[h * GQA_GROUP + g4], sink_col)
            m = jnp.maximum(m, sink_col)
        p = jnp.exp(s - m)
        pv = _dot(p.astype(BF16), v)
        l = jnp.sum(p, axis=1, keepdims=True)
        if has_sink:
            l = l + jnp.exp(sink_col - m)
        on = pv * (1.0 / l)
        og = jnp.zeros((blk, KV_WIDTH), F32)
        for h in range(N_KV_HEADS):
            og = jnp.where(lane_head == h, on[blk * h:blk * (h + 1)], og)
            if want_lse:
                lse_h = m[blk * h:blk * (h + 1)] + jnp.log(l[blk * h:blk * (h + 1)])
                lse_tile = jnp.where(lane == g4 * N_KV_HEADS + h, lse_h, lse_tile)
        o_ref[:, KV_WIDTH * g4:KV_WIDTH * (g4 + 1)] = og.astype(o_ref.dtype)
    if want_lse:
        lse_ref[...] = lse_tile


def _band_bias(max_dist):
    qi = np.arange(BAND_BLOCK)[:, None]
    mi = np.arange(2 * BAND_BLOCK)[None, :]
    dist = BAND_BLOCK + qi - mi
    ok = (dist >= 0) & (dist <= max_dist)
    first = ok & (mi >= BAND_BLOCK)
    return np.where(np.stack([first, ok]), 0.0, NEG_INF).astype(np.float32)


def _head_mask(rows, dtype):
    lane_head = np.arange(KV_WIDTH)[None, None, :] // HEAD_DIM
    hm = (lane_head == np.arange(N_KV_HEADS)[:, None, None]).astype(np.float32)
    return jnp.asarray(np.broadcast_to(hm, (N_KV_HEADS, rows, KV_WIDTH)), dtype)


def _band_attention(q, k, v, sinks, n_seq, seq_len, dil, max_dist, out_dtype, want_lse):
    blk = BAND_BLOCK
    tp = n_seq * seq_len
    nb = seq_len // dil // blk
    q2, k2, v2 = q, k, v
    cur = lambda b, r, n: (b * nb + n, r)
    prev = lambda b, r, n: (b * nb + jnp.maximum(n - 1, 0), r)
    in_specs, args = [], []
    if sinks is not None:
        in_specs.append(pl.BlockSpec(memory_space=pltpu.SMEM))
        args.append(sinks)
    in_specs += [pl.BlockSpec((blk, Q_WIDTH), cur),
                 pl.BlockSpec((blk, KV_WIDTH), cur), pl.BlockSpec((blk, KV_WIDTH), prev),
                 pl.BlockSpec((blk, KV_WIDTH), cur), pl.BlockSpec((blk, KV_WIDTH), prev),
                 pl.BlockSpec((1, blk, 2 * blk), lambda b, r, n: (jnp.minimum(n, 1), 0, 0)),
                 pl.BlockSpec((N_KV_HEADS, blk, KV_WIDTH), lambda b, r, n: (0, 0, 0))]
    args += [q2, k2, k2, v2, v2, jnp.asarray(_band_bias(max_dist)), _head_mask(blk, BF16)]
    out_shape = [jax.ShapeDtypeStruct((tp // dil, dil * Q_WIDTH), out_dtype)]
    out_specs = [pl.BlockSpec((blk, Q_WIDTH), cur)]
    if want_lse:
        out_shape.append(jax.ShapeDtypeStruct((tp // dil, dil * LANES), F32))
        out_specs.append(pl.BlockSpec((blk, LANES), cur))
    res = pl.pallas_call(
        functools.partial(_band_body, sinks is not None, want_lse),
        grid=(n_seq, dil, nb), in_specs=in_specs, out_specs=out_specs, out_shape=out_shape,
        compiler_params=_params(3), name=f"band_d{dil}")(*args)
    return res[0], (res[1] if want_lse else None)


def _sample_body(has_sink, want_lse, n_new, seqs, *refs):
    it = iter(refs)
    sink_ref = next(it) if has_sink else None
    q_ref, kn_ref, vn_ref, cache_ref, bp_ref, bn_ref, hm_ref = (next(it) for _ in range(7))
    o_ref = next(it)
    lse_ref = next(it) if want_lse else None
    n_past = cache_ref.shape[-1]
    rows = N_Q_HEADS * n_new
    lane_head = lax.broadcasted_iota(jnp.int32, (n_new, KV_WIDTH), 1) // HEAD_DIM
    lane = lax.broadcasted_iota(jnp.int32, (n_new, LANES), 1)
    zpad = jnp.zeros((LANES - n_new, KV_WIDTH), F32)
    row_head = lax.broadcasted_iota(jnp.int32, (rows, 1), 0) // n_new

    for sb in range(seqs):
        rs = slice(n_new * sb, n_new * (sb + 1))
        q = q_ref[rs, :]
        lhs = jnp.concatenate(
            [q[:, KV_WIDTH * g4:KV_WIDTH * (g4 + 1)] * hm_ref[h]
             for g4 in range(GQA_GROUP) for h in range(N_KV_HEADS)], axis=0).astype(BF16)
        kt = cache_ref[sb, 0].reshape(KV_WIDTH, n_past).astype(BF16)
        vt = cache_ref[sb, 1].reshape(KV_WIDTH, n_past).astype(BF16)
        kn = jnp.concatenate([kn_ref[rs, :], zpad], axis=0).astype(BF16)
        vn = jnp.concatenate([vn_ref[rs, :], zpad], axis=0).astype(BF16)
        s1 = _dot(lhs, kt) + bp_ref[...]
        s2 = _dot_nt(lhs, kn) + bn_ref[...]
        m = jnp.maximum(jnp.max(s1, axis=1, keepdims=True), jnp.max(s2, axis=1, keepdims=True))
        if has_sink:
            sink_col = jnp.zeros((rows, 1), F32)
            for g4 in range(GQA_GROUP):
                for h in range(N_KV_HEADS):
                    sink_col = jnp.where(row_head == g4 * N_KV_HEADS + h, sink_ref[h * GQA_GROUP + g4], sink_col)
            m = jnp.maximum(m, sink_col)
        p1 = jnp.exp(s1 - m)
        p2 = jnp.exp(s2 - m)
        l = jnp.sum(p1, axis=1, keepdims=True) + jnp.sum(p2, axis=1, keepdims=True)
        if has_sink:
            l = l + jnp.exp(sink_col - m)
        pv = _dot_nt(p1.astype(BF16), vt) + _dot(p2.astype(BF16), vn)
        on = pv * (1.0 / l)
        lse = m + jnp.log(l)
        lse_tile = jnp.zeros((n_new, LANES), F32)
        for g4 in range(GQA_GROUP):
            og = jnp.zeros((n_new, KV_WIDTH), F32)
            for h in range(N_KV_HEADS):
                r0 = (g4 * N_KV_HEADS + h) * n_new
                og = jnp.where(lane_head == h, on[r0:r0 + n_new], og)
                if want_lse:
                    lse_tile = jnp.where(lane == g4 * N_KV_HEADS + h, lse[r0:r0 + n_new], lse_tile)
            o_ref[rs, KV_WIDTH * g4:KV_WIDTH * (g4 + 1)] = og.astype(o_ref.dtype)
        if want_lse:
            lse_ref[rs, :] = lse_tile


def _sample_bias(n_new, n_past, n_keys, dil):
    j = (np.arange(N_Q_HEADS * n_new) % n_new)[:, None]
    span = dil * (n_keys - 1)

    def ok(delta):
        return (delta >= 0) & (delta % dil == 0) & (delta <= span)

    past = ok(n_past + j - np.arange(n_past)[None, :])
    i_new = np.arange(LANES)[None, :]
    new = ok(j - i_new) & (i_new < n_new)
    to_bias = lambda m: jnp.asarray(np.where(m, 0.0, NEG_INF).astype(np.float32))
    return to_bias(past), to_bias(new)


def _sample_attention(q, k_new, v_new, cache, sinks, n_new, n_keys, dil, out_dtype, want_lse):
    n_seq, n_past = cache.shape[0], cache.shape[-1]
    seqs = 4 if n_past <= 512 else 2
    rows = seqs * n_new
    bias_past, bias_new = _sample_bias(n_new, n_past, n_keys, dil)
    n_rows = N_Q_HEADS * n_new
    row = lambda i: (i, 0)
    const = lambda i: (0, 0)
    in_specs, args = [], []
    if sinks is not None:
        in_specs.append(pl.BlockSpec(memory_space=pltpu.SMEM))
        args.append(sinks)
    in_specs += [pl.BlockSpec((rows, Q_WIDTH), row), pl.BlockSpec((rows, KV_WIDTH), row),
                 pl.BlockSpec((rows, KV_WIDTH), row),
                 pl.BlockSpec((seqs, 2, N_KV_HEADS, HEAD_DIM, n_past), lambda i: (i, 0, 0, 0, 0)),
                 pl.BlockSpec((n_rows, n_past), const), pl.BlockSpec((n_rows, LANES), const),
                 pl.BlockSpec((N_KV_HEADS, n_new, KV_WIDTH), lambda i: (0, 0, 0))]
    args += [q, k_new, v_new, cache, bias_past, bias_new, _head_mask(n_new, F32)]
    out_shape = [jax.ShapeDtypeStruct((n_seq * n_new, Q_WIDTH), out_dtype)]
    out_specs = [pl.BlockSpec((rows, Q_WIDTH), row)]
    if want_lse:
        out_shape.append(jax.ShapeDtypeStruct((n_seq * n_new, LANES), F32))
        out_specs.append(pl.BlockSpec((rows, LANES), row))
    res = pl.pallas_call(
        functools.partial(_sample_body, sinks is not None, want_lse, n_new, seqs),
        grid=(n_seq // seqs,), in_specs=in_specs, out_specs=out_specs, out_shape=out_shape,
        compiler_params=_params(1), name=f"sample_d{dil}")(*args)
    return res[0], (res[1] if want_lse else None)


def _route_tile(logits, tri_ref, counts_ref):
    tm = logits.shape[0]
    lane = lax.broadcasted_iota(jnp.int32, (tm, LANES), 1)
    lane_f = lane.astype(F32)
    none = float(LANES)
    first_lane = lambda hit: jnp.min(jnp.where(hit, lane_f, none), axis=1, keepdims=True)

    gl = jnp.where(lane < N_EXPERT_GROUPS, logits, NEG_INF)
    gmax = jnp.max(gl, axis=1, keepdims=True)
    g_idx = first_lane(gl == gmax)
    g_w = 1.0 / jnp.sum(jnp.exp(gl - gmax), axis=1, keepdims=True)

    e_lane = lane - N_EXPERT_GROUPS
    lane_group = jnp.where((e_lane >= 0) & (e_lane < N_EXPERTS), (e_lane // EXPERTS_PER_GROUP).astype(F32), none)
    el = jnp.where(lane_group == g_idx, logits, NEG_INF)
    v1 = jnp.max(el, axis=1, keepdims=True)
    i1 = first_lane(el == v1)
    el2 = jnp.where(lane_f == i1, NEG_INF, el)
    v2 = jnp.max(el2, axis=1, keepdims=True)
    i2 = first_lane(el2 == v2)
    t = jnp.exp(v2 - v1)
    den = 1.0 + t
    w1 = g_w * (1.0 / den)
    w2 = g_w * (t / den)
    e1 = i1 - float(N_EXPERT_GROUPS)
    e2 = i2 - float(N_EXPERT_GROUPS)

    hot1 = lane_f == e1
    hot2 = lane_f == e2
    both = jnp.where(hot1, 1.0, 0.0) + jnp.where(hot2, 1.0, 0.0)
    before = _dot(tri_ref[...], both.astype(BF16)) + counts_ref[0:1, :]
    r1 = jnp.sum(jnp.where(hot1, before, 0.0), axis=1, keepdims=True)
    r2 = jnp.sum(jnp.where(hot2, before, 0.0), axis=1, keepdims=True)
    counts_ref[...] = counts_ref[...] + jnp.sum(both, axis=0, keepdims=True)

    info = jnp.zeros((tm, LANES), F32)
    for j, col in enumerate((e1, e2, r1, r2, w1, w2)):
        info = jnp.where(lane == j, col, info)
    return info


def _outproj_body(dils, n_ptiles, *refs):
    n_groups = len(dils)
    it = iter(refs)
    o_refs = [(next(it), next(it)) for _ in range(n_groups)]
    lse_refs = [(next(it), next(it)) for _ in range(n_groups)] if n_groups > 1 else None
    expand_ref = next(it) if n_groups > 1 else None
    h_ref, wo_ref, g_ref, wr_hi_ref, wr_lo_ref, br_ref, tri_ref = (next(it) for _ in range(7))
    hout_ref, xn_ref, info_ref, counts_ref = next(it), next(it), next(it), next(it)
    stages = {g: (next(it), next(it)) for g in range(n_groups) if dils[g] > 1}

    @pl.when(pl.program_id(0) == 0)
    def _():
        counts_ref[...] = jnp.zeros(counts_ref.shape, F32)

    is_p = pl.program_id(0) < n_ptiles

    def pick(pair, g, which):
        prompt_ref, sample_ref = pair
        dil, width = dils[g], sample_ref.shape[1]
        if dil == 1:
            prompt = prompt_ref[...]
        else:
            stage = stages[g][which]
            rows = prompt_ref.shape[0]
            for r in range(dil):
                for j in range(width // LANES):
                    c0 = r * width + LANES * j
                    stage[j, pl.ds(r, rows, stride=dil), :] = prompt_ref[:, c0:c0 + LANES]
            prompt = jnp.concatenate([stage[j] for j in range(width // LANES)], axis=1)
        return jnp.where(is_p, prompt, sample_ref[...])

    if n_groups == 1:
        o = pick(o_refs[0], 0, 0)
    else:
        lses = [pick(p, g, 1) for g, p in enumerate(lse_refs)]
        mx = functools.reduce(jnp.maximum, lses)
        es = [jnp.exp(x - mx) for x in lses]
        inv = 1.0 / functools.reduce(lambda a, b: a + b, es)
        acc = None
        for g in range(n_groups):
            w = es[g] * inv
            w_hi = w.astype(BF16)
            w_lo = (w - w_hi.astype(F32)).astype(BF16)
            wide = _dot(w_hi, expand_ref[...]) + _dot(w_lo, expand_ref[...])
            term = wide * pick(o_refs[g], g, 0)
            acc = term if acc is None else acc + term
        o = acc.astype(BF16)
    h_new = h_ref[...] + _dot(o, wo_ref[...])
    hout_ref[...] = h_new
    xn = _rms(h_new, g_ref[...])
    xn_ref[...] = xn
    x_hi = xn.astype(BF16)
    x_lo = (xn - x_hi.astype(F32)).astype(BF16)
    logits = (_dot(x_hi, wr_hi_ref[...]) + _dot(x_hi, wr_lo_ref[...])
              + _dot(x_lo, wr_hi_ref[...]) + br_ref[...])
    info_ref[...] = _route_tile(logits, tri_ref, counts_ref)


def _out_project(o_pairs, lse_pairs, dils, h, wo, gain, wr_hi, wr_lo, br, n_ptiles):
    tm = ROW_TILE
    n_tok = h.shape[0]
    n_groups = len(o_pairs)
    row = lambda i: (i, 0)
    const = lambda i: (0, 0)
    prow = lambda i: (jnp.minimum(i, n_ptiles - 1), 0)
    srow = lambda i: (jnp.maximum(i - n_ptiles, 0), 0)
    in_specs, args, scratch = [], [], []
    for (op, os_), d in zip(o_pairs, dils):
        in_specs += [pl.BlockSpec((tm // d, d * Q_WIDTH), prow), pl.BlockSpec((tm, Q_WIDTH), srow)]
        args += [op, os_]
        if d > 1:
            scratch += [pltpu.VMEM((Q_WIDTH // LANES, tm, LANES), F32), pltpu.VMEM((1, tm, LANES), F32)]
    if n_groups > 1:
        for (lp, ls), d in zip(lse_pairs, dils):
            in_specs += [pl.BlockSpec((tm // d, d * LANES), prow), pl.BlockSpec((tm, LANES), srow)]
            args += [lp, ls]
        expand = (np.arange(LANES)[:, None] == np.arange(Q_WIDTH)[None, :] // HEAD_DIM)
        in_specs.append(pl.BlockSpec((LANES, Q_WIDTH), const))
        args.append(jnp.asarray(expand.astype(np.float32), BF16))
    in_specs += [pl.BlockSpec((tm, D_MODEL), row), pl.BlockSpec((Q_WIDTH, D_MODEL), const),
                 pl.BlockSpec((1, D_MODEL), const), pl.BlockSpec((D_MODEL, LANES), const),
                 pl.BlockSpec((D_MODEL, LANES), const), pl.BlockSpec((1, LANES), const),
                 pl.BlockSpec((tm, tm), const)]
    strictly_lower = np.tril(np.ones((tm, tm), np.float32), -1)
    args += [h, wo, gain, wr_hi, wr_lo, br, jnp.asarray(strictly_lower, BF16)]
    out_shape = [jax.ShapeDtypeStruct((n_tok, D_MODEL), F32), jax.ShapeDtypeStruct((n_tok, D_MODEL), F32),
                 jax.ShapeDtypeStruct((n_tok, LANES), F32), jax.ShapeDtypeStruct((SUBLANES, LANES), F32)]
    out_specs = [pl.BlockSpec((tm, D_MODEL), row), pl.BlockSpec((tm, D_MODEL), row),
                 pl.BlockSpec((tm, LANES), row), pl.BlockSpec((SUBLANES, LANES), const)]
    return pl.pallas_call(
        functools.partial(_outproj_body, tuple(dils), n_ptiles),
        grid=(n_tok // tm,), in_specs=in_specs, out_specs=out_specs, out_shape=out_shape,
        scratch_shapes=scratch, compiler_params=_params(1), name=f"outproj_{n_groups}")(*args)


DISPATCH_CHUNK = 256


DISPATCH_BUFFERS = 3


def _dispatch_body(n_tok, dest_ref, counts_ref, pad_start_ref, xn_hbm, xs_hbm, buf, zero_ref,
                   sem_in, sem_out, zero_sem):
    chunk = DISPATCH_CHUNK
    n_chunks = n_tok // chunk

    def load(c, sl):
        rows = pl.ds(pl.multiple_of(c * chunk, chunk), chunk)
        return pltpu.make_async_copy(xn_hbm.at[rows], buf.at[sl], sem_in.at[sl])

    def wait_scatter(sl):
        for _ in range(TOP_K_INNER):
            pltpu.make_async_copy(buf.at[sl], xs_hbm.at[pl.ds(0, chunk)], sem_out.at[sl]).wait()

    load(0, 0).start()

    def do_chunk(c, carry):
        sl = c % DISPATCH_BUFFERS
        nxt = (c + 1) % DISPATCH_BUFFERS

        @pl.when(c >= DISPATCH_BUFFERS - 1)
        def _():
            wait_scatter(nxt)

        @pl.when(c + 1 < n_chunks)
        def _():
            load(c + 1, nxt).start()

        load(c, sl).wait()
        for t in range(chunk):
            for k in range(TOP_K_INNER):
                d = dest_ref[TOP_K_INNER * (c * chunk + t) + k]
                pltpu.make_async_copy(buf.at[sl, pl.ds(t, 1)], xs_hbm.at[pl.ds(d, 1)], sem_out.at[sl]).start()
        return carry

    lax.fori_loop(0, n_chunks, do_chunk, 0)
    for c in range(max(n_chunks - DISPATCH_BUFFERS + 1, 0), n_chunks):
        wait_scatter(c % DISPATCH_BUFFERS)

    zero_ref[...] = jnp.zeros(zero_ref.shape, F32)

    def expert(e, carry):
        count = counts_ref[e]
        n_pad = (EXPERT_BLOCK - count % EXPERT_BLOCK) % EXPERT_BLOCK
        base = pad_start_ref[e] + count
        n_single = jnp.minimum((SUBLANES - base % SUBLANES) % SUBLANES, n_pad)
        n_groups = (n_pad - n_single) // SUBLANES
        zero_row = lambda r: pltpu.make_async_copy(zero_ref.at[pl.ds(0, 1)], xs_hbm.at[pl.ds(base + r, 1)], zero_sem)
        zero_group = lambda j: pltpu.make_async_copy(
            zero_ref.at[pl.ds(0, SUBLANES)],
            xs_hbm.at[pl.ds(pl.multiple_of(base + n_single + j * SUBLANES, SUBLANES), SUBLANES)], zero_sem)

        for n, make in ((n_single, zero_row), (n_groups, zero_group)):
            def start(r, inner, make=make):
                make(r).start()
                return inner
            lax.fori_loop(0, n, start, 0)

            def wait(r, inner, make=make):
                make(r).wait()
                return inner
            lax.fori_loop(0, n, wait, 0)
        return carry

    lax.fori_loop(0, N_EXPERTS, expert, 0)

    n_blocks = xs_hbm.shape[0] // EXPERT_BLOCK
    first_free = (pad_start_ref[N_EXPERTS - 1] + counts_ref[N_EXPERTS - 1] + EXPERT_BLOCK - 1) // EXPERT_BLOCK
    zero_block = lambda b: pltpu.make_async_copy(
        zero_ref, xs_hbm.at[pl.ds(pl.multiple_of(b * EXPERT_BLOCK, EXPERT_BLOCK), EXPERT_BLOCK)], zero_sem)

    def start_block(b, carry):
        zero_block(b).start()
        return carry
    lax.fori_loop(first_free, n_blocks, start_block, 0)

    def wait_block(b, carry):
        zero_block(b).wait()
        return carry
    lax.fori_loop(first_free, n_blocks, wait_block, 0)


def _dispatch(xn, dest, counts, pad_start, n_slots):
    n_tok = xn.shape[0]
    grid_spec = pltpu.PrefetchScalarGridSpec(
        num_scalar_prefetch=3, grid=(1,),
        in_specs=[pl.BlockSpec(memory_space=pl.ANY)],
        out_specs=pl.BlockSpec(memory_space=pl.ANY),
        scratch_shapes=[pltpu.VMEM((DISPATCH_BUFFERS, DISPATCH_CHUNK, D_MODEL), F32),
                        pltpu.VMEM((EXPERT_BLOCK, D_MODEL), F32),
                        pltpu.SemaphoreType.DMA((DISPATCH_BUFFERS,)), pltpu.SemaphoreType.DMA((DISPATCH_BUFFERS,)),
                        pltpu.SemaphoreType.DMA(())])
    return pl.pallas_call(
        functools.partial(_dispatch_body, n_tok),
        grid_spec=grid_spec, out_shape=jax.ShapeDtypeStruct((n_slots, D_MODEL), F32),
        compiler_params=_params(1), name="moe_dispatch")(dest, counts, pad_start, xn)


def _moe_body(layer, block_e_ref, next_e_ref, parity_ref, n_used_ref, x_ref, wg_hbm, wu_hbm, wd_hbm, y_ref,
              wg_f32, wu_f32, wd_f32, wg_bf, wu_bf, wd_bf, sem):
    i = pl.program_id(0)
    n_used = n_used_ref[0]

    def fetch(e, sl):
        return [pltpu.make_async_copy(src.at[layer, e], dst.at[sl], sem.at[sl])
                for src, dst in ((wg_hbm, wg_f32), (wu_hbm, wu_f32), (wd_hbm, wd_f32))]

    @pl.when((i == 0) & (n_used > 0))
    def _():
        for copy in fetch(block_e_ref[0], parity_ref[0]):
            copy.start()

    @pl.when(i < n_used)
    def _():
        changed = (i == 0) | (block_e_ref[i] != block_e_ref[jnp.maximum(i - 1, 0)])

        @pl.when(changed)
        def _():
            sl = parity_ref[i]
            for copy in fetch(block_e_ref[i], sl):
                copy.wait()

            @pl.when(next_e_ref[i] >= 0)
            def _():
                for copy in fetch(next_e_ref[i], 1 - sl):
                    copy.start()

            wg_bf[...] = wg_f32[sl].astype(BF16)
            wu_bf[...] = wu_f32[sl].astype(BF16)
            wd_bf[...] = wd_f32[sl].astype(BF16)

        x = x_ref[...].astype(BF16)
        gate = _dot(x, wg_bf[...])
        up = _dot(x, wu_bf[...])
        mid = (gate * jax.nn.sigmoid(gate) * up).astype(BF16)
        y_ref[...] = _dot(mid, wd_bf[...])

    @pl.when(i >= n_used)
    def _():
        y_ref[...] = jnp.zeros(y_ref.shape, F32)


def _moe_experts(xs, block_e, next_e, parity, n_used, w_gate, w_up, w_down, layer):
    n_slots = xs.shape[0]
    grid_spec = pltpu.PrefetchScalarGridSpec(
        num_scalar_prefetch=4, grid=(n_slots // EXPERT_BLOCK,),
        in_specs=[pl.BlockSpec((EXPERT_BLOCK, D_MODEL),
                               lambda i, be, ne, pa, nu: (jnp.minimum(i, jnp.maximum(nu[0] - 1, 0)), 0)),
                  pl.BlockSpec(memory_space=pl.ANY), pl.BlockSpec(memory_space=pl.ANY),
                  pl.BlockSpec(memory_space=pl.ANY)],
        out_specs=pl.BlockSpec((EXPERT_BLOCK, D_MODEL), lambda i, be, ne, pa, nu: (i, 0)),
        scratch_shapes=[pltpu.VMEM((2, D_MODEL, D_FF_EXPERT), F32),
                        pltpu.VMEM((2, D_MODEL, D_FF_EXPERT), F32),
                        pltpu.VMEM((2, D_FF_EXPERT, D_MODEL), F32),
                        pltpu.VMEM((D_MODEL, D_FF_EXPERT), BF16),
                        pltpu.VMEM((D_MODEL, D_FF_EXPERT), BF16),
                        pltpu.VMEM((D_FF_EXPERT, D_MODEL), BF16),
                        pltpu.SemaphoreType.DMA((2,))])
    return pl.pallas_call(
        functools.partial(_moe_body, layer), grid_spec=grid_spec,
        out_shape=jax.ShapeDtypeStruct((n_slots, D_MODEL), F32),
        compiler_params=_params(1), name=f"moe_{layer}")(
            block_e, next_e, parity, n_used, xs, w_gate, w_up, w_down)


def _slot_layout(info, counts_tile, n_slots):
    counts = counts_tile[0, :N_EXPERTS].astype(jnp.int32)
    expert = info[:, 0:TOP_K_INNER].astype(jnp.int32)
    rank = info[:, TOP_K_INNER:2 * TOP_K_INNER].astype(jnp.int32)
    padded = (counts + EXPERT_BLOCK - 1) // EXPERT_BLOCK * EXPERT_BLOCK
    pad_end = jnp.cumsum(padded)
    pad_start = pad_end - padded
    hot = expert[..., None] == jnp.arange(N_EXPERTS, dtype=jnp.int32)
    dest = (jnp.sum(jnp.where(hot, pad_start, 0), axis=-1) + rank).reshape(-1)
    block_start = jnp.arange(n_slots // EXPERT_BLOCK, dtype=jnp.int32) * EXPERT_BLOCK
    block_e = jnp.minimum(jnp.sum(pad_end[None, :] <= block_start[:, None], axis=1), N_EXPERTS - 1)
    n_used = pad_end[-1] // EXPERT_BLOCK
    own_end = jnp.sum(jnp.where(block_e[:, None] == jnp.arange(N_EXPERTS), pad_end, 0), axis=1)
    after = own_end // EXPERT_BLOCK
    next_e = jnp.where(after < n_used, jnp.sum(pad_end[None, :] <= (after * EXPERT_BLOCK)[:, None], axis=1), -1)
    changed = jnp.concatenate([jnp.zeros((1,), jnp.int32), (block_e[1:] != block_e[:-1]).astype(jnp.int32)])
    parity = jnp.cumsum(changed) % 2
    i32 = lambda a: a.astype(jnp.int32)
    return (i32(dest), counts, i32(pad_start), i32(block_e), i32(next_e), i32(parity), i32(n_used).reshape(1))


def _start_row_gather(dest_ref, ys_hbm, ybuf, sem, tile, sl):
    for t in range(ROW_TILE):
        for k in range(TOP_K_INNER):
            d = dest_ref[TOP_K_INNER * (tile * ROW_TILE + t) + k]
            pltpu.make_async_copy(ys_hbm.at[pl.ds(d, 1)], ybuf.at[sl, k, pl.ds(t, 1)], sem.at[sl]).start()


def _moe_output_tile(dest_ref, info_ref, ys_hbm, ybuf, sem, tile0):
    i = pl.program_id(0)
    sl = i % 2

    @pl.when(i == 0)
    def _():
        _start_row_gather(dest_ref, ys_hbm, ybuf, sem, tile0, 0)

    @pl.when(i + 1 < pl.num_programs(0))
    def _():
        _start_row_gather(dest_ref, ys_hbm, ybuf, sem, tile0 + i + 1, 1 - sl)

    for k in range(TOP_K_INNER):
        pltpu.make_async_copy(ys_hbm.at[pl.ds(0, ROW_TILE)], ybuf.at[sl, k], sem.at[sl]).wait()
    info = info_ref[...]
    w_lane = 2 * TOP_K_INNER
    return ybuf[sl, 0] * info[:, w_lane:w_lane + 1] + ybuf[sl, 1] * info[:, w_lane + 1:w_lane + 2]


_COMBINE_SCRATCH = [pltpu.VMEM((2, TOP_K_INNER, ROW_TILE, D_MODEL), F32), pltpu.SemaphoreType.DMA((2,))]


def _final_body(tile0, dest_ref, h_ref, info_ref, ys_hbm, g_ref, o_ref, ybuf, sem):
    y = _moe_output_tile(dest_ref, info_ref, ys_hbm, ybuf, sem, tile0)
    o_ref[...] = _rms(h_ref[...] + y, g_ref[...])


def _final_norm(h, info, ys, dest, gain, tile0, n_tiles):
    tm = ROW_TILE
    grid_spec = pltpu.PrefetchScalarGridSpec(
        num_scalar_prefetch=1, grid=(n_tiles,),
        in_specs=[pl.BlockSpec((tm, D_MODEL), lambda i, d: (i + tile0, 0)),
                  pl.BlockSpec((tm, LANES), lambda i, d: (i + tile0, 0)),
                  pl.BlockSpec(memory_space=pl.ANY),
                  pl.BlockSpec((1, D_MODEL), lambda i, d: (0, 0))],
        out_specs=pl.BlockSpec((tm, D_MODEL), lambda i, d: (i, 0)),
        scratch_shapes=_COMBINE_SCRATCH)
    return pl.pallas_call(
        functools.partial(_final_body, tile0), grid_spec=grid_spec,
        out_shape=jax.ShapeDtypeStruct((n_tiles * tm, D_MODEL), F32),
        compiler_params=_params(1), name="final_norm")(dest, h, info, ys, gain)


def _q_perm():
    g4, h, d = np.meshgrid(np.arange(GQA_GROUP), np.arange(N_KV_HEADS), np.arange(HEAD_DIM), indexing="ij")
    return ((h * GQA_GROUP + g4) * HEAD_DIM + d).reshape(-1)


def _prep_qkv_weight(w, n_groups):
    perm = _q_perm()
    cols = []
    for g in range(n_groups):
        base = g * QKV_WIDTH
        cols.append(w[:, base:base + Q_WIDTH][:, perm] * ATTN_SCALE)
        cols.append(w[:, base + Q_WIDTH:base + QKV_WIDTH])
    return jnp.concatenate(cols, axis=1).astype(BF16)


def _rope_tables(seq_len, past_len, n_new):
    inv_freq = ROPE_THETA ** (-jnp.arange(0, ROT_DIM, 2, dtype=F32) / ROT_DIM)
    pos_p = jnp.arange(seq_len, dtype=jnp.int32)
    pos_s = past_len + (jnp.arange(ROW_TILE, dtype=jnp.int32) % n_new)
    ang = jnp.concatenate([pos_p, pos_s]).astype(F32)[:, None] * inv_freq[None, :]
    cos, sin = jnp.cos(ang), jnp.sin(ang)
    rows = ang.shape[0]
    pad = jnp.zeros((rows, HEAD_DIM - ROT_DIM), F32)
    zero = jnp.zeros((rows, ROT_HALF), F32)
    c64 = jnp.concatenate([cos, cos, pad + 1.0], axis=1)
    s1 = jnp.concatenate([-sin, zero, pad], axis=1)
    s2 = jnp.concatenate([zero, sin, pad], axis=1)
    dup = lambda t: jnp.concatenate([t] * (LANES // HEAD_DIM), axis=1)
    return dup(c64), dup(s1), dup(s2)


def _split_hi_lo(w):
    hi = w.astype(BF16)
    return hi, (w - hi.astype(F32)).astype(BF16)


def _state_rows(k, v, lead):
    kv = jnp.stack([k, v], axis=-2)
    return kv.reshape(lead + (k.shape[-2], 2, N_KV_HEADS, HEAD_DIM))


def kernel(x_prompt, x_sample, cache_a_kv, cache_b_kv0, cache_b_kv1, cache_b_kv2, norm_mix, norm_ffn, norm_final, a_w_qkv, a_sinks, a_w_o, b_w_qkv, b_w_o, moe_router_g_w, moe_router_g_b, moe_router_e_w, moe_router_e_b, moe_w_gate, moe_w_up, moe_w_down):
    n_seq, seq_len, _ = x_prompt.shape
    n_dec, n_new, _ = x_sample.shape
    tp, ts = n_seq * seq_len, n_dec * n_new
    n_tok = tp + ts
    n_ptiles = tp // ROW_TILE
    tiles_per_seq = seq_len // ROW_TILE
    n_slots = (-(-(n_tok * TOP_K_INNER) // EXPERT_BLOCK) + N_EXPERTS) * EXPERT_BLOCK
    caches_b = (cache_b_kv0, cache_b_kv1, cache_b_kv2)

    tables = _rope_tables(seq_len, PAST_LEN, n_new)
    perm = _q_perm()
    xp = x_prompt.reshape(tp, D_MODEL)
    xs = x_sample.reshape(ts, D_MODEL)

    def keys_on_lanes(cache):
        return jnp.transpose(cache, (0, 2, 3, 4, 1))

    def router(i):
        w = jnp.concatenate([moe_router_g_w[i], moe_router_e_w[i],
                             jnp.zeros((D_MODEL, LANES - N_EXPERT_GROUPS - N_EXPERTS), F32)], axis=1)
        b = jnp.concatenate([moe_router_g_b[i], moe_router_e_b[i],
                             jnp.zeros((LANES - N_EXPERT_GROUPS - N_EXPERTS,), F32)]).reshape(1, LANES)
        return _split_hi_lo(w) + (b,)

    def moe(i, xn, info, counts_tile):
        dest, counts, pad_start, block_e, next_e, parity, n_used = _slot_layout(info, counts_tile, n_slots)
        xs_sorted = _dispatch(xn, dest, counts, pad_start, n_slots)
        return _moe_experts(xs_sorted, block_e, next_e, parity, n_used, moe_w_gate, moe_w_up, moe_w_down, i), dest

    def sample_rows(t, d, width):
        return t[tp // d:].reshape(ts, width)

    def window_state(k, v, n_keep, d):
        tail = lambda t: jnp.stack([t[((b + 1) * seq_len - n_keep) // d:(b + 1) * seq_len // d].reshape(n_keep, KV_WIDTH)
                                    for b in range(n_seq)])
        return _state_rows(tail(k), tail(v), (n_seq,))[None]

    def new_rows_state(k, v, d):
        rows = lambda t: sample_rows(t, d, KV_WIDTH).reshape(n_dec, n_new, KV_WIDTH)
        return _state_rows(rows(k), rows(v), (n_dec,))[None]

    def decode(q, k, v, d, cache, sinks, n_keys, out_dtype, want_lse):
        return _sample_attention(sample_rows(q, d, Q_WIDTH).astype(F32), sample_rows(k, d, KV_WIDTH),
                                 sample_rows(v, d, KV_WIDTH), keys_on_lanes(cache), sinks,
                                 n_new, n_keys, d, out_dtype, want_lse)

    w_a = _prep_qkv_weight(a_w_qkv[0], 1)
    h0, [(q, k, v)] = _project("split", (xp, xs), norm_mix[0:1], w_a, tables, (1,), n_tok, n_ptiles, tiles_per_seq)
    sinks = a_sinks[0].astype(F32)
    o_p, _ = _band_attention(q, k, v, sinks, n_seq, seq_len, 1, WINDOW_A - 1, BF16, False)
    o_s, _ = decode(q, k, v, 1, cache_a_kv[0], sinks, WINDOW_A, BF16, False)
    new_a_p = window_state(k, v, min(WINDOW_A, seq_len), 1)
    new_a_s = new_rows_state(k, v, 1)
    wr_hi, wr_lo, br = router(0)
    h1, xn, info, counts_tile = _out_project([(o_p, o_s)], None, (1,), h0, a_w_o[0][perm].astype(BF16),
                                             norm_ffn[0:1], wr_hi, wr_lo, br, n_ptiles)
    ys, dest = moe(0, xn, info, counts_tile)

    w_b = _prep_qkv_weight(b_w_qkv[0], len(B_PATTERNS))
    dils = tuple(d for _, d in B_PATTERNS)
    h2, groups = _project("combine", (h1, info, ys, dest), norm_mix[1:2], w_b, tables, dils,
                          n_tok, n_ptiles, tiles_per_seq)
    o_pairs, lse_pairs, new_b = [], [], []
    for g, (window, dil) in enumerate(B_PATTERNS):
        q, k, v = groups[g]
        o_p, lse_p = _band_attention(q, k, v, None, n_seq, seq_len, dil, window // dil, F32, True)
        o_s, lse_s = decode(q, k, v, dil, caches_b[g][0], None, window // dil + 1, F32, True)
        o_pairs.append((o_p, o_s))
        lse_pairs.append((lse_p, lse_s))
        new_b.append(window_state(k, v, min(window, seq_len), dil))
        new_b.append(new_rows_state(k, v, dil))
    wr_hi, wr_lo, br = router(1)
    h3, xn, info, counts_tile = _out_project(o_pairs, lse_pairs, dils, h2, b_w_o[0][perm].astype(BF16),
                                             norm_ffn[1:2], wr_hi, wr_lo, br, n_ptiles)
    ys, dest = moe(1, xn, info, counts_tile)

    gain = norm_final.reshape(1, D_MODEL)
    y_p = _final_norm(h3, info, ys, dest, gain, 0, n_ptiles).reshape(n_seq, seq_len, D_MODEL)
    y_s = _final_norm(h3, info, ys, dest, gain, n_ptiles, ts // ROW_TILE).reshape(n_dec, n_new, D_MODEL)
    return (y_p, y_s, new_a_p, new_a_s, *new_b)
```

```python
import functools

import numpy as np
import jax
import jax.numpy as jnp
from jax import lax
from jax.experimental import pallas as pl
from jax.experimental.pallas import tpu as pltpu

D_MODEL = 1024
HEAD_DIM = 64
N_KV_HEADS = 4
GQA_GROUP = 4
N_Q_HEADS = N_KV_HEADS * GQA_GROUP
Q_WIDTH = N_Q_HEADS * HEAD_DIM
KV_WIDTH = N_KV_HEADS * HEAD_DIM
QKV_WIDTH = Q_WIDTH + 2 * KV_WIDTH
ROT_DIM = HEAD_DIM // 4
ROT_HALF = ROT_DIM // 2
ROPE_THETA = 500000.0
ATTN_SCALE = HEAD_DIM ** -0.5
WINDOW_A = 128
B_PATTERNS = ((128, 1), (512, 4), (2048, 16))
BAND_BLOCK = 128
N_EXPERT_GROUPS = 8
EXPERTS_PER_GROUP = 8
N_EXPERTS = N_EXPERT_GROUPS * EXPERTS_PER_GROUP
TOP_K_INNER = 2
D_FF_EXPERT = 512
RMS_EPS = 1e-5
PAST_LEN = 16384

LANES = 128
SUBLANES = 8
ROW_TILE = 256
EXPERT_BLOCK = 256
VMEM_LIMIT_BYTES = 56 * 1024 * 1024

BF16 = jnp.bfloat16
F32 = jnp.float32
NEG_INF = float("-inf")


def _params(n_axes, vmem=VMEM_LIMIT_BYTES):
    return pltpu.CompilerParams(dimension_semantics=("arbitrary",) * n_axes,
                                vmem_limit_bytes=vmem)


def _rms(x, g):
    var = jnp.mean(x * x, axis=-1, keepdims=True)
    return x * lax.rsqrt(var + RMS_EPS) * g


def _dot(a, b):
    return jnp.dot(a, b, preferred_element_type=F32)


def _dot_nt(a, b):
    return lax.dot_general(a, b, (((1,), (1,)), ((), ())), preferred_element_type=F32)


def _proj_body(dils, mode, n_ptiles, *refs):
    n_groups = len(dils)
    it = iter(refs)
    if mode == "split":
        xp_ref, xs_ref = next(it), next(it)
    else:
        dest_ref, h_ref, info_ref, ys_hbm = next(it), next(it), next(it), next(it)
    g_ref, w_ref, cos_ref, s1_ref, s2_ref = (next(it) for _ in range(5))
    hout_ref = next(it)
    outs = [(next(it), next(it), next(it)) for _ in range(n_groups)]
    stage = next(it) if max(dils) > 1 else None

    if mode == "split":
        is_p = pl.program_id(0) < n_ptiles
        x = jnp.where(is_p, xp_ref[...], xs_ref[...])
    else:
        ybuf, sem = next(it), next(it)
        y, drain = _moe_output_tile(dest_ref, info_ref, ys_hbm, ybuf, sem, 0)
        x = h_ref[...] + y
    hout_ref[...] = x
    xn = _rms(x, g_ref[...]).astype(BF16)
    c, s1, s2 = cos_ref[...], s1_ref[...], s2_ref[...]

    def rope(y):
        parts = []
        for j in range(y.shape[1] // LANES):
            yb = y[:, LANES * j:LANES * (j + 1)]
            parts.append(yb * c + pltpu.roll(yb, LANES - ROT_HALF, 1) * s1
                         + pltpu.roll(yb, ROT_HALF, 1) * s2)
        return jnp.concatenate(parts, axis=1)

    def emit(out_ref, y, dil, width, col):
        n = y.shape[1]
        if dil == 1:
            out_ref[:, col:col + n] = y.astype(out_ref.dtype)
            return
        rows = y.shape[0] // dil
        for j in range(n // LANES):
            stage[j] = y[:, LANES * j:LANES * (j + 1)]
        for r in range(dil):
            for j in range(n // LANES):
                c0 = r * width + col + LANES * j
                out_ref[:, c0:c0 + LANES] = stage[j, pl.ds(r, rows, stride=dil), :].astype(out_ref.dtype)

    half = Q_WIDTH // 2
    for g in range(n_groups):
        base = g * QKV_WIDTH
        q_ref, k_ref, v_ref = outs[g]
        for hf in range(2):
            y = _dot(xn, w_ref[:, base + half * hf:base + half * (hf + 1)])
            emit(q_ref, rope(y), dils[g], Q_WIDTH, half * hf)
        y = _dot(xn, w_ref[:, base + Q_WIDTH:base + Q_WIDTH + KV_WIDTH])
        emit(k_ref, rope(y), dils[g], KV_WIDTH, 0)
        y = _dot(xn, w_ref[:, base + Q_WIDTH + KV_WIDTH:base + QKV_WIDTH])
        emit(v_ref, y, dils[g], KV_WIDTH, 0)
    if mode == "combine":
        drain()


def _project(mode, xs, gain, w, tables, dils, n_tok, n_ptiles, tiles_per_seq):
    tm = ROW_TILE
    n_tiles = n_tok // tm
    row = lambda i, *_: (i, 0)
    const = lambda i, *_: (0, 0)
    tab = lambda i, *_: (jnp.where(i < n_ptiles, i % tiles_per_seq, tiles_per_seq), 0)
    if mode == "split":
        xp, xsamp = xs
        in_specs = [pl.BlockSpec((tm, D_MODEL), lambda i: (jnp.minimum(i, n_ptiles - 1), 0)),
                    pl.BlockSpec((tm, D_MODEL), lambda i: (jnp.maximum(i - n_ptiles, 0), 0))]
        args = [xp, xsamp]
        prefetch, scratch = [], []
    else:
        h, info, ys, dest = xs
        in_specs = [pl.BlockSpec((tm, D_MODEL), row), pl.BlockSpec((tm, LANES), row),
                    pl.BlockSpec(memory_space=pl.ANY)]
        args = [h, info, ys]
        prefetch, scratch = [dest], _COMBINE_SCRATCH
    if max(dils) > 1:
        scratch = [pltpu.VMEM((Q_WIDTH // 2 // LANES, tm, LANES), F32)] + list(scratch)
    in_specs += [pl.BlockSpec((1, D_MODEL), const),
                 pl.BlockSpec(w.shape, const),
                 pl.BlockSpec((tm, LANES), tab), pl.BlockSpec((tm, LANES), tab),
                 pl.BlockSpec((tm, LANES), tab)]
    args += [gain, w, *tables]
    out_shape = [jax.ShapeDtypeStruct((n_tok, D_MODEL), F32)]
    out_specs = [pl.BlockSpec((tm, D_MODEL), row)]
    for d in dils:
        for width, dtype in ((Q_WIDTH, BF16), (KV_WIDTH, F32), (KV_WIDTH, F32)):
            out_shape.append(jax.ShapeDtypeStruct((n_tok // d, d * width), dtype))
            out_specs.append(pl.BlockSpec((tm // d, d * width), row))
    grid_spec = pltpu.PrefetchScalarGridSpec(
        num_scalar_prefetch=len(prefetch), grid=(n_tiles,), in_specs=in_specs, out_specs=out_specs,
        scratch_shapes=scratch)
    res = pl.pallas_call(
        functools.partial(_proj_body, tuple(dils), mode, n_ptiles), grid_spec=grid_spec, out_shape=out_shape,
        compiler_params=_params(1), name=f"proj_{mode}")(*prefetch, *args)
    h_out = res[0]
    groups = [tuple(res[1 + 3 * g:4 + 3 * g]) for g in range(len(dils))]
    return h_out, groups


BAND_STEP_BLOCKS = 2


def _band_body(has_sink, want_lse, *refs):
    it = iter(refs)
    sink_ref = next(it) if has_sink else None
    q_ref, kc_ref, kp_ref, vc_ref, vp_ref, bias_first_ref, bias_ref, hm_ref = (next(it) for _ in range(8))
    o_ref = next(it)
    lse_ref = next(it) if want_lse else None
    blk = BAND_BLOCK
    for sub in range(BAND_STEP_BLOCKS):
        rows = slice(blk * sub, blk * (sub + 1))
        if sub == 0:
            k = jnp.concatenate([kp_ref[...], kc_ref[rows, :]], axis=0)
            v = jnp.concatenate([vp_ref[...], vc_ref[rows, :]], axis=0)
            bias = bias_first_ref[0]
        else:
            k = kc_ref[blk * (sub - 1):blk * (sub + 1), :]
            v = vc_ref[blk * (sub - 1):blk * (sub + 1), :]
            bias = bias_ref[0]
        _band_block(has_sink, want_lse, sink_ref, q_ref[rows, :], k.astype(BF16), v.astype(BF16), bias, hm_ref,
                    o_ref, lse_ref, rows)


def _band_block(has_sink, want_lse, sink_ref, q, k, v, bias, hm_ref, o_ref, lse_ref, rows):
    blk = BAND_BLOCK
    bias4 = jnp.concatenate([bias] * N_KV_HEADS, axis=0)
    lane_head = lax.broadcasted_iota(jnp.int32, (blk, KV_WIDTH), 1) // HEAD_DIM
    lane = lax.broadcasted_iota(jnp.int32, (blk, LANES), 1)
    lse_tile = jnp.zeros((blk, LANES), F32)
    row_head = lax.broadcasted_iota(jnp.int32, (N_KV_HEADS * blk, 1), 0) // blk

    for g4 in range(GQA_GROUP):
        qg = q[:, KV_WIDTH * g4:KV_WIDTH * (g4 + 1)]
        lhs = jnp.concatenate([qg * hm_ref[h] for h in range(N_KV_HEADS)], axis=0)
        s = _dot_nt(lhs, k) + bias4
        m = jnp.max(s, axis=1, keepdims=True)
        if has_sink:
            sink_col = jnp.zeros((N_KV_HEADS * blk, 1), F32)
            for h in range(N_KV_HEADS):
                sink_col = jnp.where(row_head == h, sink_ref[h * GQA_GROUP + g4], sink_col)
            m = jnp.maximum(m, sink_col)
        p = jnp.exp(s - m)
        pv = _dot(p.astype(BF16), v)
        l = jnp.sum(p, axis=1, keepdims=True)
        if has_sink:
            l = l + jnp.exp(sink_col - m)
        on = pv * (1.0 / l)
        og = jnp.zeros((blk, KV_WIDTH), F32)
        for h in range(N_KV_HEADS):
            og = jnp.where(lane_head == h, on[blk * h:blk * (h + 1)], og)
            if want_lse:
                lse_h = m[blk * h:blk * (h + 1)] + jnp.log(l[blk * h:blk * (h + 1)])
                lse_tile = jnp.where(lane == g4 * N_KV_HEADS + h, lse_h, lse_tile)
        o_ref[rows, KV_WIDTH * g4:KV_WIDTH * (g4 + 1)] = og.astype(o_ref.dtype)
    if want_lse:
        lse_ref[rows, :] = lse_tile


def _band_bias(max_dist):
    qi = np.arange(BAND_BLOCK)[:, None]
    mi = np.arange(2 * BAND_BLOCK)[None, :]
    dist = BAND_BLOCK + qi - mi
    ok = (dist >= 0) & (dist <= max_dist)
    first = ok & (mi >= BAND_BLOCK)
    return np.where(np.stack([first, ok]), 0.0, NEG_INF).astype(np.float32)


def _head_mask(rows, dtype):
    lane_head = np.arange(KV_WIDTH)[None, None, :] // HEAD_DIM
    hm = (lane_head == np.arange(N_KV_HEADS)[:, None, None]).astype(np.float32)
    return jnp.asarray(np.broadcast_to(hm, (N_KV_HEADS, rows, KV_WIDTH)), dtype)


def _band_attention(q, k, v, sinks, n_seq, seq_len, dil, max_dist, out_dtype, want_lse):
    blk = BAND_BLOCK
    step = BAND_STEP_BLOCKS * blk
    tp = n_seq * seq_len
    n_steps = seq_len // dil // step
    q2, k2, v2 = q, k, v
    cur = lambda b, r, n: (b * n_steps + n, r)
    prev = lambda b, r, n: (BAND_STEP_BLOCKS * (b * n_steps + n) - jnp.minimum(n, 1), r)
    in_specs, args = [], []
    if sinks is not None:
        in_specs.append(pl.BlockSpec(memory_space=pltpu.SMEM))
        args.append(sinks)
    bias = jnp.asarray(_band_bias(max_dist))
    in_specs += [pl.BlockSpec((step, Q_WIDTH), cur),
                 pl.BlockSpec((step, KV_WIDTH), cur), pl.BlockSpec((blk, KV_WIDTH), prev),
                 pl.BlockSpec((step, KV_WIDTH), cur), pl.BlockSpec((blk, KV_WIDTH), prev),
                 pl.BlockSpec((1, blk, 2 * blk), lambda b, r, n: (jnp.minimum(n, 1), 0, 0)),
                 pl.BlockSpec((1, blk, 2 * blk), lambda b, r, n: (1, 0, 0)),
                 pl.BlockSpec((N_KV_HEADS, blk, KV_WIDTH), lambda b, r, n: (0, 0, 0))]
    args += [q2, k2, k2, v2, v2, bias, bias, _head_mask(blk, BF16)]
    out_shape = [jax.ShapeDtypeStruct((tp // dil, dil * Q_WIDTH), out_dtype)]
    out_specs = [pl.BlockSpec((step, Q_WIDTH), cur)]
    if want_lse:
        out_shape.append(jax.ShapeDtypeStruct((tp // dil, dil * LANES), F32))
        out_specs.append(pl.BlockSpec((step, LANES), cur))
    res = pl.pallas_call(
        functools.partial(_band_body, sinks is not None, want_lse),
        grid=(n_seq, dil, n_steps), in_specs=in_specs, out_specs=out_specs, out_shape=out_shape,
        compiler_params=_params(3), name=f"band_d{dil}")(*args)
    return res[0], (res[1] if want_lse else None)


def _sample_body(has_sink, want_lse, n_new, seqs, *refs):
    it = iter(refs)
    sink_ref = next(it) if has_sink else None
    q_ref, kn_ref, vn_ref, cache_ref, bp_ref, bn_ref, hm_ref = (next(it) for _ in range(7))
    o_ref = next(it)
    lse_ref = next(it) if want_lse else None
    n_past = cache_ref.shape[-1]
    rows = N_Q_HEADS * n_new
    lane_head = lax.broadcasted_iota(jnp.int32, (n_new, KV_WIDTH), 1) // HEAD_DIM
    lane = lax.broadcasted_iota(jnp.int32, (n_new, LANES), 1)
    zpad = jnp.zeros((LANES - n_new, KV_WIDTH), F32)
    row_head = lax.broadcasted_iota(jnp.int32, (rows, 1), 0) // n_new

    for sb in range(seqs):
        rs = slice(n_new * sb, n_new * (sb + 1))
        q = q_ref[rs, :]
        lhs = jnp.concatenate(
            [q[:, KV_WIDTH * g4:KV_WIDTH * (g4 + 1)] * hm_ref[h]
             for g4 in range(GQA_GROUP) for h in range(N_KV_HEADS)], axis=0).astype(BF16)
        kt = cache_ref[sb, 0].reshape(KV_WIDTH, n_past).astype(BF16)
        vt = cache_ref[sb, 1].reshape(KV_WIDTH, n_past).astype(BF16)
        kn = jnp.concatenate([kn_ref[rs, :], zpad], axis=0).astype(BF16)
        vn = jnp.concatenate([vn_ref[rs, :], zpad], axis=0).astype(BF16)
        s1 = _dot(lhs, kt) + bp_ref[...]
        s2 = _dot_nt(lhs, kn) + bn_ref[...]
        m = jnp.maximum(jnp.max(s1, axis=1, keepdims=True), jnp.max(s2, axis=1, keepdims=True))
        if has_sink:
            sink_col = jnp.zeros((rows, 1), F32)
            for g4 in range(GQA_GROUP):
                for h in range(N_KV_HEADS):
                    sink_col = jnp.where(row_head == g4 * N_KV_HEADS + h, sink_ref[h * GQA_GROUP + g4], sink_col)
            m = jnp.maximum(m, sink_col)
        p1 = jnp.exp(s1 - m)
        p2 = jnp.exp(s2 - m)
        l = jnp.sum(p1, axis=1, keepdims=True) + jnp.sum(p2, axis=1, keepdims=True)
        if has_sink:
            l = l + jnp.exp(sink_col - m)
        pv = _dot_nt(p1.astype(BF16), vt) + _dot(p2.astype(BF16), vn)
        on = pv * (1.0 / l)
        lse = m + jnp.log(l)
        lse_tile = jnp.zeros((n_new, LANES), F32)
        for g4 in range(GQA_GROUP):
            og = jnp.zeros((n_new, KV_WIDTH), F32)
            for h in range(N_KV_HEADS):
                r0 = (g4 * N_KV_HEADS + h) * n_new
                og = jnp.where(lane_head == h, on[r0:r0 + n_new], og)
                if want_lse:
                    lse_tile = jnp.where(lane == g4 * N_KV_HEADS + h, lse[r0:r0 + n_new], lse_tile)
            o_ref[rs, KV_WIDTH * g4:KV_WIDTH * (g4 + 1)] = og.astype(o_ref.dtype)
        if want_lse:
            lse_ref[rs, :] = lse_tile


def _sample_bias(n_new, n_past, n_keys, dil):
    j = (np.arange(N_Q_HEADS * n_new) % n_new)[:, None]
    span = dil * (n_keys - 1)

    def ok(delta):
        return (delta >= 0) & (delta % dil == 0) & (delta <= span)

    past = ok(n_past + j - np.arange(n_past)[None, :])
    i_new = np.arange(LANES)[None, :]
    new = ok(j - i_new) & (i_new < n_new)
    to_bias = lambda m: jnp.asarray(np.where(m, 0.0, NEG_INF).astype(np.float32))
    return to_bias(past), to_bias(new)


def _sample_attention(q, k_new, v_new, cache, sinks, n_new, n_keys, dil, out_dtype, want_lse):
    n_seq, n_past = cache.shape[0], cache.shape[-1]
    seqs = 4 if n_past <= 512 else 2
    rows = seqs * n_new
    bias_past, bias_new = _sample_bias(n_new, n_past, n_keys, dil)
    n_rows = N_Q_HEADS * n_new
    row = lambda i: (i, 0)
    const = lambda i: (0, 0)
    in_specs, args = [], []
    if sinks is not None:
        in_specs.append(pl.BlockSpec(memory_space=pltpu.SMEM))
        args.append(sinks)
    in_specs += [pl.BlockSpec((rows, Q_WIDTH), row), pl.BlockSpec((rows, KV_WIDTH), row),
                 pl.BlockSpec((rows, KV_WIDTH), row),
                 pl.BlockSpec((seqs, 2, N_KV_HEADS, HEAD_DIM, n_past), lambda i: (i, 0, 0, 0, 0)),
                 pl.BlockSpec((n_rows, n_past), const), pl.BlockSpec((n_rows, LANES), const),
                 pl.BlockSpec((N_KV_HEADS, n_new, KV_WIDTH), lambda i: (0, 0, 0))]
    args += [q, k_new, v_new, cache, bias_past, bias_new, _head_mask(n_new, F32)]
    out_shape = [jax.ShapeDtypeStruct((n_seq * n_new, Q_WIDTH), out_dtype)]
    out_specs = [pl.BlockSpec((rows, Q_WIDTH), row)]
    if want_lse:
        out_shape.append(jax.ShapeDtypeStruct((n_seq * n_new, LANES), F32))
        out_specs.append(pl.BlockSpec((rows, LANES), row))
    res = pl.pallas_call(
        functools.partial(_sample_body, sinks is not None, want_lse, n_new, seqs),
        grid=(n_seq // seqs,), in_specs=in_specs, out_specs=out_specs, out_shape=out_shape,
        compiler_params=_params(1), name=f"sample_d{dil}")(*args)
    return res[0], (res[1] if want_lse else None)


def _route_tile(logits, tri_ref, counts_ref):
    tm = logits.shape[0]
    lane = lax.broadcasted_iota(jnp.int32, (tm, LANES), 1)
    lane_f = lane.astype(F32)
    none = float(LANES)
    first_lane = lambda hit: jnp.min(jnp.where(hit, lane_f, none), axis=1, keepdims=True)

    gl = jnp.where(lane < N_EXPERT_GROUPS, logits, NEG_INF)
    gmax = jnp.max(gl, axis=1, keepdims=True)
    g_idx = first_lane(gl == gmax)
    g_w = 1.0 / jnp.sum(jnp.exp(gl - gmax), axis=1, keepdims=True)

    e_lane = lane - N_EXPERT_GROUPS
    lane_group = jnp.where((e_lane >= 0) & (e_lane < N_EXPERTS), (e_lane // EXPERTS_PER_GROUP).astype(F32), none)
    el = jnp.where(lane_group == g_idx, logits, NEG_INF)
    v1 = jnp.max(el, axis=1, keepdims=True)
    i1 = first_lane(el == v1)
    el2 = jnp.where(lane_f == i1, NEG_INF, el)
    v2 = jnp.max(el2, axis=1, keepdims=True)
    i2 = first_lane(el2 == v2)
    t = jnp.exp(v2 - v1)
    den = 1.0 + t
    w1 = g_w * (1.0 / den)
    w2 = g_w * (t / den)
    e1 = i1 - float(N_EXPERT_GROUPS)
    e2 = i2 - float(N_EXPERT_GROUPS)

    hot1 = lane_f == e1
    hot2 = lane_f == e2
    both = jnp.where(hot1, 1.0, 0.0) + jnp.where(hot2, 1.0, 0.0)
    before = _dot(tri_ref[...], both.astype(BF16)) + counts_ref[0:1, :]
    r1 = jnp.sum(jnp.where(hot1, before, 0.0), axis=1, keepdims=True)
    r2 = jnp.sum(jnp.where(hot2, before, 0.0), axis=1, keepdims=True)
    counts_ref[...] = counts_ref[...] + jnp.sum(both, axis=0, keepdims=True)

    info = jnp.zeros((tm, LANES), F32)
    for j, col in enumerate((e1, e2, r1, r2, w1, w2)):
        info = jnp.where(lane == j, col, info)
    return info


def _outproj_body(dils, n_ptiles, *refs):
    n_groups = len(dils)
    it = iter(refs)
    o_refs = [(next(it), next(it)) for _ in range(n_groups)]
    lse_refs = [(next(it), next(it)) for _ in range(n_groups)] if n_groups > 1 else None
    expand_ref = next(it) if n_groups > 1 else None
    h_ref, wo_ref, g_ref, wr_hi_ref, wr_lo_ref, br_ref, tri_ref = (next(it) for _ in range(7))
    hout_ref, xn_ref, info_ref, counts_ref = next(it), next(it), next(it), next(it)
    stages = {g: (next(it), next(it)) for g in range(n_groups) if dils[g] > 1}

    @pl.when(pl.program_id(0) == 0)
    def _():
        counts_ref[...] = jnp.zeros(counts_ref.shape, F32)

    is_p = pl.program_id(0) < n_ptiles

    def pick(pair, g, which):
        prompt_ref, sample_ref = pair
        dil, width = dils[g], sample_ref.shape[1]
        if dil == 1:
            prompt = prompt_ref[...]
        else:
            stage = stages[g][which]
            rows = prompt_ref.shape[0]
            for r in range(dil):
                for j in range(width // LANES):
                    c0 = r * width + LANES * j
                    stage[j, pl.ds(r, rows, stride=dil), :] = prompt_ref[:, c0:c0 + LANES]
            prompt = jnp.concatenate([stage[j] for j in range(width // LANES)], axis=1)
        return jnp.where(is_p, prompt, sample_ref[...])

    if n_groups == 1:
        o = pick(o_refs[0], 0, 0)
    else:
        lses = [pick(p, g, 1) for g, p in enumerate(lse_refs)]
        mx = functools.reduce(jnp.maximum, lses)
        es = [jnp.exp(x - mx) for x in lses]
        inv = 1.0 / functools.reduce(lambda a, b: a + b, es)
        acc, wide_sum = None, None
        for g in range(n_groups):
            if g < n_groups - 1:
                w = es[g] * inv
                w_hi = w.astype(BF16)
                w_lo = (w - w_hi.astype(F32)).astype(BF16)
                wide = _dot(w_hi, expand_ref[...]) + _dot(w_lo, expand_ref[...])
                wide_sum = wide if wide_sum is None else wide_sum + wide
            else:
                wide = 1.0 - wide_sum
            term = wide * pick(o_refs[g], g, 0)
            acc = term if acc is None else acc + term
        o = acc.astype(BF16)
    h_new = h_ref[...] + _dot(o, wo_ref[...])
    hout_ref[...] = h_new
    xn = _rms(h_new, g_ref[...])
    xn_ref[...] = xn
    x_hi = xn.astype(BF16)
    x_lo = (xn - x_hi.astype(F32)).astype(BF16)
    logits = (_dot(x_hi, wr_hi_ref[...]) + _dot(x_hi, wr_lo_ref[...])
              + _dot(x_lo, wr_hi_ref[...]) + br_ref[...])
    info_ref[...] = _route_tile(logits, tri_ref, counts_ref)


def _out_project(o_pairs, lse_pairs, dils, h, wo, gain, wr_hi, wr_lo, br, n_ptiles):
    tm = ROW_TILE
    n_tok = h.shape[0]
    n_groups = len(o_pairs)
    row = lambda i: (i, 0)
    const = lambda i: (0, 0)
    prow = lambda i: (jnp.minimum(i, n_ptiles - 1), 0)
    srow = lambda i: (jnp.maximum(i - n_ptiles, 0), 0)
    in_specs, args, scratch = [], [], []
    for (op, os_), d in zip(o_pairs, dils):
        in_specs += [pl.BlockSpec((tm // d, d * Q_WIDTH), prow), pl.BlockSpec((tm, Q_WIDTH), srow)]
        args += [op, os_]
        if d > 1:
            scratch += [pltpu.VMEM((Q_WIDTH // LANES, tm, LANES), F32), pltpu.VMEM((1, tm, LANES), F32)]
    if n_groups > 1:
        for (lp, ls), d in zip(lse_pairs, dils):
            in_specs += [pl.BlockSpec((tm // d, d * LANES), prow), pl.BlockSpec((tm, LANES), srow)]
            args += [lp, ls]
        expand = (np.arange(LANES)[:, None] == np.arange(Q_WIDTH)[None, :] // HEAD_DIM)
        in_specs.append(pl.BlockSpec((LANES, Q_WIDTH), const))
        args.append(jnp.asarray(expand.astype(np.float32), BF16))
    in_specs += [pl.BlockSpec((tm, D_MODEL), row), pl.BlockSpec((Q_WIDTH, D_MODEL), const),
                 pl.BlockSpec((1, D_MODEL), const), pl.BlockSpec((D_MODEL, LANES), const),
                 pl.BlockSpec((D_MODEL, LANES), const), pl.BlockSpec((1, LANES), const),
                 pl.BlockSpec((tm, tm), const)]
    strictly_lower = np.tril(np.ones((tm, tm), np.float32), -1)
    args += [h, wo, gain, wr_hi, wr_lo, br, jnp.asarray(strictly_lower, BF16)]
    out_shape = [jax.ShapeDtypeStruct((n_tok, D_MODEL), F32), jax.ShapeDtypeStruct((n_tok, D_MODEL), F32),
                 jax.ShapeDtypeStruct((n_tok, LANES), F32), jax.ShapeDtypeStruct((SUBLANES, LANES), F32)]
    out_specs = [pl.BlockSpec((tm, D_MODEL), row), pl.BlockSpec((tm, D_MODEL), row),
                 pl.BlockSpec((tm, LANES), row), pl.BlockSpec((SUBLANES, LANES), const)]
    return pl.pallas_call(
        functools.partial(_outproj_body, tuple(dils), n_ptiles),
        grid=(n_tok // tm,), in_specs=in_specs, out_specs=out_specs, out_shape=out_shape,
        scratch_shapes=scratch, compiler_params=_params(1), name=f"outproj_{n_groups}")(*args)


DISPATCH_CHUNK = 256


DISPATCH_BUFFERS = 3


def _dispatch_body(n_tok, dest_ref, counts_ref, pad_start_ref, xn_hbm, xs_hbm, buf, zero_ref,
                   sem_in, sem_out, zero_sem):
    chunk = DISPATCH_CHUNK
    n_chunks = n_tok // chunk

    def load(c, sl):
        rows = pl.ds(pl.multiple_of(c * chunk, chunk), chunk)
        return pltpu.make_async_copy(xn_hbm.at[rows], buf.at[sl], sem_in.at[sl])

    def wait_scatter(sl):
        for _ in range(TOP_K_INNER):
            pltpu.make_async_copy(buf.at[sl], xs_hbm.at[pl.ds(0, chunk)], sem_out.at[sl]).wait()

    load(0, 0).start()

    def do_chunk(c, carry):
        sl = c % DISPATCH_BUFFERS
        nxt = (c + 1) % DISPATCH_BUFFERS

        @pl.when(c >= DISPATCH_BUFFERS - 1)
        def _():
            wait_scatter(nxt)

        @pl.when(c + 1 < n_chunks)
        def _():
            load(c + 1, nxt).start()

        load(c, sl).wait()
        for t in range(chunk):
            for k in range(TOP_K_INNER):
                d = dest_ref[TOP_K_INNER * (c * chunk + t) + k]
                pltpu.make_async_copy(buf.at[sl, pl.ds(t, 1)], xs_hbm.at[pl.ds(d, 1)], sem_out.at[sl]).start()
        return carry

    lax.fori_loop(0, n_chunks, do_chunk, 0)
    for c in range(max(n_chunks - DISPATCH_BUFFERS + 1, 0), n_chunks):
        wait_scatter(c % DISPATCH_BUFFERS)

    zero_ref[...] = jnp.zeros(zero_ref.shape, F32)

    def expert(e, carry):
        count = counts_ref[e]
        n_pad = (EXPERT_BLOCK - count % EXPERT_BLOCK) % EXPERT_BLOCK
        base = pad_start_ref[e] + count
        n_single = jnp.minimum((SUBLANES - base % SUBLANES) % SUBLANES, n_pad)
        n_groups = (n_pad - n_single) // SUBLANES
        zero_row = lambda r: pltpu.make_async_copy(zero_ref.at[pl.ds(0, 1)], xs_hbm.at[pl.ds(base + r, 1)], zero_sem)
        zero_group = lambda j: pltpu.make_async_copy(
            zero_ref.at[pl.ds(0, SUBLANES)],
            xs_hbm.at[pl.ds(pl.multiple_of(base + n_single + j * SUBLANES, SUBLANES), SUBLANES)], zero_sem)

        for n, make in ((n_single, zero_row), (n_groups, zero_group)):
            def start(r, inner, make=make):
                make(r).start()
                return inner
            lax.fori_loop(0, n, start, 0)

            def wait(r, inner, make=make):
                make(r).wait()
                return inner
            lax.fori_loop(0, n, wait, 0)
        return carry

    lax.fori_loop(0, N_EXPERTS, expert, 0)

    n_blocks = xs_hbm.shape[0] // EXPERT_BLOCK
    first_free = (pad_start_ref[N_EXPERTS - 1] + counts_ref[N_EXPERTS - 1] + EXPERT_BLOCK - 1) // EXPERT_BLOCK
    zero_block = lambda b: pltpu.make_async_copy(
        zero_ref, xs_hbm.at[pl.ds(pl.multiple_of(b * EXPERT_BLOCK, EXPERT_BLOCK), EXPERT_BLOCK)], zero_sem)

    def start_block(b, carry):
        zero_block(b).start()
        return carry
    lax.fori_loop(first_free, n_blocks, start_block, 0)

    def wait_block(b, carry):
        zero_block(b).wait()
        return carry
    lax.fori_loop(first_free, n_blocks, wait_block, 0)


def _dispatch(xn, dest, counts, pad_start, n_slots):
    n_tok = xn.shape[0]
    grid_spec = pltpu.PrefetchScalarGridSpec(
        num_scalar_prefetch=3, grid=(1,),
        in_specs=[pl.BlockSpec(memory_space=pl.ANY)],
        out_specs=pl.BlockSpec(memory_space=pl.ANY),
        scratch_shapes=[pltpu.VMEM((DISPATCH_BUFFERS, DISPATCH_CHUNK, D_MODEL), F32),
                        pltpu.VMEM((EXPERT_BLOCK, D_MODEL), F32),
                        pltpu.SemaphoreType.DMA((DISPATCH_BUFFERS,)), pltpu.SemaphoreType.DMA((DISPATCH_BUFFERS,)),
                        pltpu.SemaphoreType.DMA(())])
    return pl.pallas_call(
        functools.partial(_dispatch_body, n_tok),
        grid_spec=grid_spec, out_shape=jax.ShapeDtypeStruct((n_slots, D_MODEL), F32),
        compiler_params=_params(1), name="moe_dispatch")(dest, counts, pad_start, xn)


def _moe_body(layer, block_e_ref, next_e_ref, parity_ref, n_used_ref, x_ref, wg_hbm, wu_hbm, wd_hbm, y_ref,
              wg_f32, wu_f32, wd_f32, wg_bf, wu_bf, wd_bf, sem):
    i = pl.program_id(0)
    n_used = n_used_ref[0]

    def fetch(e, sl):
        return [pltpu.make_async_copy(src.at[layer, e], dst.at[sl], sem.at[sl])
                for src, dst in ((wg_hbm, wg_f32), (wu_hbm, wu_f32), (wd_hbm, wd_f32))]

    @pl.when((i == 0) & (n_used > 0))
    def _():
        for copy in fetch(block_e_ref[0], parity_ref[0]):
            copy.start()

    @pl.when(i < n_used)
    def _():
        changed = (i == 0) | (block_e_ref[i] != block_e_ref[jnp.maximum(i - 1, 0)])

        @pl.when(changed)
        def _():
            sl = parity_ref[i]
            for copy in fetch(block_e_ref[i], sl):
                copy.wait()

            @pl.when(next_e_ref[i] >= 0)
            def _():
                for copy in fetch(next_e_ref[i], 1 - sl):
                    copy.start()

            wg_bf[...] = wg_f32[sl].astype(BF16)
            wu_bf[...] = wu_f32[sl].astype(BF16)
            wd_bf[...] = wd_f32[sl].astype(BF16)

        x = x_ref[...].astype(BF16)
        gate = _dot(x, wg_bf[...])
        up = _dot(x, wu_bf[...])
        mid = (gate * jax.nn.sigmoid(gate) * up).astype(BF16)
        y_ref[...] = _dot(mid, wd_bf[...])

    @pl.when(i >= n_used)
    def _():
        y_ref[...] = jnp.zeros(y_ref.shape, F32)


def _moe_experts(xs, block_e, next_e, parity, n_used, w_gate, w_up, w_down, layer):
    n_slots = xs.shape[0]
    grid_spec = pltpu.PrefetchScalarGridSpec(
        num_scalar_prefetch=4, grid=(n_slots // EXPERT_BLOCK,),
        in_specs=[pl.BlockSpec((EXPERT_BLOCK, D_MODEL),
                               lambda i, be, ne, pa, nu: (jnp.minimum(i, jnp.maximum(nu[0] - 1, 0)), 0)),
                  pl.BlockSpec(memory_space=pl.ANY), pl.BlockSpec(memory_space=pl.ANY),
                  pl.BlockSpec(memory_space=pl.ANY)],
        out_specs=pl.BlockSpec((EXPERT_BLOCK, D_MODEL), lambda i, be, ne, pa, nu: (i, 0)),
        scratch_shapes=[pltpu.VMEM((2, D_MODEL, D_FF_EXPERT), F32),
                        pltpu.VMEM((2, D_MODEL, D_FF_EXPERT), F32),
                        pltpu.VMEM((2, D_FF_EXPERT, D_MODEL), F32),
                        pltpu.VMEM((D_MODEL, D_FF_EXPERT), BF16),
                        pltpu.VMEM((D_MODEL, D_FF_EXPERT), BF16),
                        pltpu.VMEM((D_FF_EXPERT, D_MODEL), BF16),
                        pltpu.SemaphoreType.DMA((2,))])
    return pl.pallas_call(
        functools.partial(_moe_body, layer), grid_spec=grid_spec,
        out_shape=jax.ShapeDtypeStruct((n_slots, D_MODEL), F32),
        compiler_params=_params(1), name=f"moe_{layer}")(
            block_e, next_e, parity, n_used, xs, w_gate, w_up, w_down)


def _slot_layout(info, counts_tile, n_slots):
    counts = counts_tile[0, :N_EXPERTS].astype(jnp.int32)
    expert = info[:, 0:TOP_K_INNER].astype(jnp.int32)
    rank = info[:, TOP_K_INNER:2 * TOP_K_INNER].astype(jnp.int32)
    padded = (counts + EXPERT_BLOCK - 1) // EXPERT_BLOCK * EXPERT_BLOCK
    pad_end = jnp.cumsum(padded)
    pad_start = pad_end - padded
    hot = expert[..., None] == jnp.arange(N_EXPERTS, dtype=jnp.int32)
    dest = (jnp.sum(jnp.where(hot, pad_start, 0), axis=-1) + rank).reshape(-1)
    block_start = jnp.arange(n_slots // EXPERT_BLOCK, dtype=jnp.int32) * EXPERT_BLOCK
    block_e = jnp.minimum(jnp.sum(pad_end[None, :] <= block_start[:, None], axis=1), N_EXPERTS - 1)
    n_used = pad_end[-1] // EXPERT_BLOCK
    own_end = jnp.sum(jnp.where(block_e[:, None] == jnp.arange(N_EXPERTS), pad_end, 0), axis=1)
    after = own_end // EXPERT_BLOCK
    next_e = jnp.where(after < n_used, jnp.sum(pad_end[None, :] <= (after * EXPERT_BLOCK)[:, None], axis=1), -1)
    changed = jnp.concatenate([jnp.zeros((1,), jnp.int32), (block_e[1:] != block_e[:-1]).astype(jnp.int32)])
    parity = jnp.cumsum(changed) % 2
    i32 = lambda a: a.astype(jnp.int32)
    return (i32(dest), counts, i32(pad_start), i32(block_e), i32(next_e), i32(parity), i32(n_used).reshape(1))


def _start_row_gather(dest_ref, ys_hbm, ybuf, sem, tile, sl):
    for t in range(ROW_TILE):
        for k in range(TOP_K_INNER):
            d = dest_ref[TOP_K_INNER * (tile * ROW_TILE + t) + k]
            pltpu.make_async_copy(ys_hbm.at[pl.ds(d, 1)], ybuf.at[sl, k, pl.ds(t, 1)], sem.at[sl]).start()


def _moe_output_tile(dest_ref, info_ref, ys_hbm, ybuf, sem, tile0):
    i = pl.program_id(0)
    last = pl.num_programs(0) - 1
    sl = i % 2

    def wait(slot):
        for k in range(TOP_K_INNER):
            pltpu.make_async_copy(ys_hbm.at[pl.ds(0, ROW_TILE)], ybuf.at[slot, k], sem.at[slot]).wait()

    @pl.when(i == 0)
    def _():
        _start_row_gather(dest_ref, ys_hbm, ybuf, sem, tile0, 0)

    wait(sl)
    info = info_ref[...]
    w_lane = 2 * TOP_K_INNER
    y = ybuf[sl, 0] * info[:, w_lane:w_lane + 1] + ybuf[sl, 1] * info[:, w_lane + 1:w_lane + 2]
    _start_row_gather(dest_ref, ys_hbm, ybuf, sem, tile0 + jnp.minimum(i + 1, last), 1 - sl)

    def drain():
        @pl.when(i == last)
        def _():
            wait(1 - sl)
    return y, drain


_COMBINE_SCRATCH = [pltpu.VMEM((2, TOP_K_INNER, ROW_TILE, D_MODEL), F32), pltpu.SemaphoreType.DMA((2,))]


def _final_body(tile0, dest_ref, h_ref, info_ref, ys_hbm, g_ref, o_ref, ybuf, sem):
    y, drain = _moe_output_tile(dest_ref, info_ref, ys_hbm, ybuf, sem, tile0)
    o_ref[...] = _rms(h_ref[...] + y, g_ref[...])
    drain()


def _final_norm(h, info, ys, dest, gain, tile0, n_tiles):
    tm = ROW_TILE
    grid_spec = pltpu.PrefetchScalarGridSpec(
        num_scalar_prefetch=1, grid=(n_tiles,),
        in_specs=[pl.BlockSpec((tm, D_MODEL), lambda i, d: (i + tile0, 0)),
                  pl.BlockSpec((tm, LANES), lambda i, d: (i + tile0, 0)),
                  pl.BlockSpec(memory_space=pl.ANY),
                  pl.BlockSpec((1, D_MODEL), lambda i, d: (0, 0))],
        out_specs=pl.BlockSpec((tm, D_MODEL), lambda i, d: (i, 0)),
        scratch_shapes=_COMBINE_SCRATCH)
    return pl.pallas_call(
        functools.partial(_final_body, tile0), grid_spec=grid_spec,
        out_shape=jax.ShapeDtypeStruct((n_tiles * tm, D_MODEL), F32),
        compiler_params=_params(1), name="final_norm")(dest, h, info, ys, gain)


def _heads_g4_major(w, axis):
    shape = w.shape
    split = shape[:axis] + (N_KV_HEADS, GQA_GROUP, HEAD_DIM) + shape[axis + 1:]
    return jnp.swapaxes(w.reshape(split), axis, axis + 1).reshape(shape)


def _prep_qkv_weight(w, n_groups):
    cols = []
    for g in range(n_groups):
        base = g * QKV_WIDTH
        cols.append(_heads_g4_major(w[:, base:base + Q_WIDTH], 1) * ATTN_SCALE)
        cols.append(w[:, base + Q_WIDTH:base + QKV_WIDTH])
    return jnp.concatenate(cols, axis=1).astype(BF16)


def _rope_tables(seq_len, past_len, n_new):
    inv_freq = ROPE_THETA ** (-jnp.arange(0, ROT_DIM, 2, dtype=F32) / ROT_DIM)
    pos_p = jnp.arange(seq_len, dtype=jnp.int32)
    pos_s = past_len + (jnp.arange(ROW_TILE, dtype=jnp.int32) % n_new)
    ang = jnp.concatenate([pos_p, pos_s]).astype(F32)[:, None] * inv_freq[None, :]
    cos, sin = jnp.cos(ang), jnp.sin(ang)
    in_head = jnp.arange(LANES, dtype=jnp.int32) % HEAD_DIM
    spread = lambda t: jnp.tile(t, (1, LANES // ROT_HALF))
    c = jnp.where(in_head < ROT_DIM, spread(cos), 1.0)
    s1 = jnp.where(in_head < ROT_HALF, -spread(sin), 0.0)
    s2 = jnp.where((in_head >= ROT_HALF) & (in_head < ROT_DIM), spread(sin), 0.0)
    return c, s1, s2


def _split_hi_lo(w):
    hi = w.astype(BF16)
    return hi, (w - hi.astype(F32)).astype(BF16)


def _state_rows(k, v, lead):
    kv = jnp.stack([k, v], axis=-2)
    return kv.reshape(lead + (k.shape[-2], 2, N_KV_HEADS, HEAD_DIM))


def kernel(x_prompt, x_sample, cache_a_kv, cache_b_kv0, cache_b_kv1, cache_b_kv2, norm_mix, norm_ffn, norm_final, a_w_qkv, a_sinks, a_w_o, b_w_qkv, b_w_o, moe_router_g_w, moe_router_g_b, moe_router_e_w, moe_router_e_b, moe_w_gate, moe_w_up, moe_w_down):
    n_seq, seq_len, _ = x_prompt.shape
    n_dec, n_new, _ = x_sample.shape
    tp, ts = n_seq * seq_len, n_dec * n_new
    n_tok = tp + ts
    n_ptiles = tp // ROW_TILE
    tiles_per_seq = seq_len // ROW_TILE
    n_slots = (-(-(n_tok * TOP_K_INNER) // EXPERT_BLOCK) + N_EXPERTS) * EXPERT_BLOCK
    caches_b = (cache_b_kv0, cache_b_kv1, cache_b_kv2)

    tables = _rope_tables(seq_len, PAST_LEN, n_new)
    xp = x_prompt.reshape(tp, D_MODEL)
    xs = x_sample.reshape(ts, D_MODEL)

    def keys_on_lanes(cache):
        return jnp.transpose(cache, (0, 2, 3, 4, 1))

    def router(i):
        w = jnp.concatenate([moe_router_g_w[i], moe_router_e_w[i],
                             jnp.zeros((D_MODEL, LANES - N_EXPERT_GROUPS - N_EXPERTS), F32)], axis=1)
        b = jnp.concatenate([moe_router_g_b[i], moe_router_e_b[i],
                             jnp.zeros((LANES - N_EXPERT_GROUPS - N_EXPERTS,), F32)]).reshape(1, LANES)
        return _split_hi_lo(w) + (b,)

    def moe(i, xn, info, counts_tile):
        dest, counts, pad_start, block_e, next_e, parity, n_used = _slot_layout(info, counts_tile, n_slots)
        xs_sorted = _dispatch(xn, dest, counts, pad_start, n_slots)
        return _moe_experts(xs_sorted, block_e, next_e, parity, n_used, moe_w_gate, moe_w_up, moe_w_down, i), dest

    def sample_rows(t, d, width):
        return t[tp // d:].reshape(ts, width)

    def window_state(k, v, n_keep, d):
        tail = lambda t: jnp.stack([t[((b + 1) * seq_len - n_keep) // d:(b + 1) * seq_len // d].reshape(n_keep, KV_WIDTH)
                                    for b in range(n_seq)])
        return _state_rows(tail(k), tail(v), (n_seq,))[None]

    def new_rows_state(k, v, d):
        rows = lambda t: sample_rows(t, d, KV_WIDTH).reshape(n_dec, n_new, KV_WIDTH)
        return _state_rows(rows(k), rows(v), (n_dec,))[None]

    def decode(q, k, v, d, cache, sinks, n_keys, out_dtype, want_lse):
        return _sample_attention(sample_rows(q, d, Q_WIDTH).astype(F32), sample_rows(k, d, KV_WIDTH),
                                 sample_rows(v, d, KV_WIDTH), keys_on_lanes(cache), sinks,
                                 n_new, n_keys, d, out_dtype, want_lse)

    w_a = _prep_qkv_weight(a_w_qkv[0], 1)
    h0, [(q, k, v)] = _project("split", (xp, xs), norm_mix[0:1], w_a, tables, (1,), n_tok, n_ptiles, tiles_per_seq)
    sinks = a_sinks[0].astype(F32)
    o_p, _ = _band_attention(q, k, v, sinks, n_seq, seq_len, 1, WINDOW_A - 1, BF16, False)
    o_s, _ = decode(q, k, v, 1, cache_a_kv[0], sinks, WINDOW_A, BF16, False)
    new_a_p = window_state(k, v, min(WINDOW_A, seq_len), 1)
    new_a_s = new_rows_state(k, v, 1)
    wr_hi, wr_lo, br = router(0)
    h1, xn, info, counts_tile = _out_project([(o_p, o_s)], None, (1,), h0, _heads_g4_major(a_w_o[0], 0).astype(BF16),
                                             norm_ffn[0:1], wr_hi, wr_lo, br, n_ptiles)
    ys, dest = moe(0, xn, info, counts_tile)

    w_b = _prep_qkv_weight(b_w_qkv[0], len(B_PATTERNS))
    dils = tuple(d for _, d in B_PATTERNS)
    h2, groups = _project("combine", (h1, info, ys, dest), norm_mix[1:2], w_b, tables, dils,
                          n_tok, n_ptiles, tiles_per_seq)
    o_pairs, lse_pairs, new_b = [], [], []
    for g, (window, dil) in enumerate(B_PATTERNS):
        q, k, v = groups[g]
        o_p, lse_p = _band_attention(q, k, v, None, n_seq, seq_len, dil, window // dil, F32, True)
        o_s, lse_s = decode(q, k, v, dil, caches_b[g][0], None, window // dil + 1, F32, True)
        o_pairs.append((o_p, o_s))
        lse_pairs.append((lse_p, lse_s))
        new_b.append(window_state(k, v, min(window, seq_len), dil))
        new_b.append(new_rows_state(k, v, dil))
    wr_hi, wr_lo, br = router(1)
    h3, xn, info, counts_tile = _out_project(o_pairs, lse_pairs, dils, h2, _heads_g4_major(b_w_o[0], 0).astype(BF16),
                                             norm_ffn[1:2], wr_hi, wr_lo, br, n_ptiles)
    ys, dest = moe(1, xn, info, counts_tile)

    gain = norm_final.reshape(1, D_MODEL)
    y_p = _final_norm(h3, info, ys, dest, gain, 0, n_ptiles).reshape(n_seq, seq_len, D_MODEL)
    y_s = _final_norm(h3, info, ys, dest, gain, n_ptiles, ts // ROW_TILE).reshape(n_dec, n_new, D_MODEL)
    return (y_p, y_s, new_a_p, new_a_s, *new_b)
```

```python
import functools

import numpy as np
import jax
import jax.numpy as jnp
from jax import lax
from jax.experimental import pallas as pl
from jax.experimental.pallas import tpu as pltpu

D_MODEL = 1024
HEAD_DIM = 64
N_KV_HEADS = 4
GQA_GROUP = 4
N_Q_HEADS = N_KV_HEADS * GQA_GROUP
Q_WIDTH = N_Q_HEADS * HEAD_DIM
KV_WIDTH = N_KV_HEADS * HEAD_DIM
QKV_WIDTH = Q_WIDTH + 2 * KV_WIDTH
ROT_DIM = HEAD_DIM // 4
ROT_HALF = ROT_DIM // 2
ROPE_THETA = 500000.0
ATTN_SCALE = HEAD_DIM ** -0.5
WINDOW_A = 128
B_PATTERNS = ((128, 1), (512, 4), (2048, 16))
BAND_BLOCK = 128
N_EXPERT_GROUPS = 8
EXPERTS_PER_GROUP = 8
N_EXPERTS = N_EXPERT_GROUPS * EXPERTS_PER_GROUP
TOP_K_INNER = 2
D_FF_EXPERT = 512
RMS_EPS = 1e-5
PAST_LEN = 16384

LANES = 128
SUBLANES = 8
ROW_TILE = 256
EXPERT_BLOCK = 256
VMEM_LIMIT_BYTES = 56 * 1024 * 1024

BF16 = jnp.bfloat16
F32 = jnp.float32
NEG_INF = float("-inf")


def _params(n_axes, vmem=VMEM_LIMIT_BYTES):
    return pltpu.CompilerParams(dimension_semantics=("arbitrary",) * n_axes,
                                vmem_limit_bytes=vmem)


def _rms(x, g):
    var = jnp.mean(x * x, axis=-1, keepdims=True)
    return x * lax.rsqrt(var + RMS_EPS) * g


def _dot(a, b):
    return jnp.dot(a, b, preferred_element_type=F32)


def _dot_nt(a, b):
    return lax.dot_general(a, b, (((1,), (1,)), ((), ())), preferred_element_type=F32)


def _proj_body(dils, mode, n_ptiles, *refs):
    n_groups = len(dils)
    it = iter(refs)
    if mode == "split":
        xp_ref, xs_ref = next(it), next(it)
    else:
        dest_ref, h_ref, info_ref, ys_hbm = next(it), next(it), next(it), next(it)
    g_ref, w_ref, cos_ref, s1_ref, s2_ref = (next(it) for _ in range(5))
    hout_ref = next(it)
    outs = [(next(it), next(it), next(it)) for _ in range(n_groups)]
    stage = next(it) if max(dils) > 1 else None

    if mode == "split":
        is_p = pl.program_id(0) < n_ptiles
        x = jnp.where(is_p, xp_ref[...], xs_ref[...])
    else:
        ybuf, sem = next(it), next(it)
        x = h_ref[...] + _moe_output_tile(dest_ref, info_ref, ys_hbm, ybuf, sem, 0)
    hout_ref[...] = x
    xn = _rms(x, g_ref[...]).astype(BF16)
    c, s1, s2 = cos_ref[...], s1_ref[...], s2_ref[...]

    def rope(y):
        parts = []
        for j in range(y.shape[1] // LANES):
            yb = y[:, LANES * j:LANES * (j + 1)]
            parts.append(yb * c + pltpu.roll(yb, LANES - ROT_HALF, 1) * s1
                         + pltpu.roll(yb, ROT_HALF, 1) * s2)
        return jnp.concatenate(parts, axis=1)

    def emit(out_ref, y, dil, width, col):
        n = y.shape[1]
        if dil == 1:
            out_ref[:, col:col + n] = y.astype(out_ref.dtype)
            return
        rows = y.shape[0] // dil
        for j in range(n // LANES):
            stage[j] = y[:, LANES * j:LANES * (j + 1)]
        for r in range(dil):
            for j in range(n // LANES):
                c0 = r * width + col + LANES * j
                out_ref[:, c0:c0 + LANES] = stage[j, pl.ds(r, rows, stride=dil), :].astype(out_ref.dtype)

    half = Q_WIDTH // 2
    for g in range(n_groups):
        base = g * QKV_WIDTH
        q_ref, k_ref, v_ref = outs[g]
        for hf in range(2):
            y = _dot(xn, w_ref[:, base + half * hf:base + half * (hf + 1)])
            emit(q_ref, rope(y), dils[g], Q_WIDTH, half * hf)
        y = _dot(xn, w_ref[:, base + Q_WIDTH:base + Q_WIDTH + KV_WIDTH])
        emit(k_ref, rope(y), dils[g], KV_WIDTH, 0)
        y = _dot(xn, w_ref[:, base + Q_WIDTH + KV_WIDTH:base + QKV_WIDTH])
        emit(v_ref, y, dils[g], KV_WIDTH, 0)


def _project(mode, xs, gain, w, tables, dils, n_tok, n_ptiles, tiles_per_seq):
    tm = ROW_TILE
    n_tiles = n_tok // tm
    row = lambda i, *_: (i, 0)
    const = lambda i, *_: (0, 0)
    tab = lambda i, *_: (jnp.where(i < n_ptiles, i % tiles_per_seq, tiles_per_seq), 0)
    if mode == "split":
        xp, xsamp = xs
        in_specs = [pl.BlockSpec((tm, D_MODEL), lambda i: (jnp.minimum(i, n_ptiles - 1), 0)),
                    pl.BlockSpec((tm, D_MODEL), lambda i: (jnp.maximum(i - n_ptiles, 0), 0))]
        args = [xp, xsamp]
        prefetch, scratch = [], []
    else:
        h, info, ys, dest = xs
        in_specs = [pl.BlockSpec((tm, D_MODEL), row), pl.BlockSpec((tm, LANES), row),
                    pl.BlockSpec(memory_space=pl.ANY)]
        args = [h, info, ys]
        prefetch, scratch = [dest], _COMBINE_SCRATCH
    if max(dils) > 1:
        scratch = [pltpu.VMEM((Q_WIDTH // 2 // LANES, tm, LANES), F32)] + list(scratch)
    in_specs += [pl.BlockSpec((1, D_MODEL), const),
                 pl.BlockSpec(w.shape, const),
                 pl.BlockSpec((tm, LANES), tab), pl.BlockSpec((tm, LANES), tab),
                 pl.BlockSpec((tm, LANES), tab)]
    args += [gain, w, *tables]
    out_shape = [jax.ShapeDtypeStruct((n_tok, D_MODEL), F32)]
    out_specs = [pl.BlockSpec((tm, D_MODEL), row)]
    for d in dils:
        for width, dtype in ((Q_WIDTH, BF16), (KV_WIDTH, F32), (KV_WIDTH, F32)):
            out_shape.append(jax.ShapeDtypeStruct((n_tok // d, d * width), dtype))
            out_specs.append(pl.BlockSpec((tm // d, d * width), row))
    grid_spec = pltpu.PrefetchScalarGridSpec(
        num_scalar_prefetch=len(prefetch), grid=(n_tiles,), in_specs=in_specs, out_specs=out_specs,
        scratch_shapes=scratch)
    res = pl.pallas_call(
        functools.partial(_proj_body, tuple(dils), mode, n_ptiles), grid_spec=grid_spec, out_shape=out_shape,
        compiler_params=_params(1), name=f"proj_{mode}")(*prefetch, *args)
    h_out = res[0]
    groups = [tuple(res[1 + 3 * g:4 + 3 * g]) for g in range(len(dils))]
    return h_out, groups


BAND_STEP_BLOCKS = 2


def _band_body(has_sink, want_lse, *refs):
    it = iter(refs)
    sink_ref = next(it) if has_sink else None
    q_ref, kc_ref, kp_ref, vc_ref, vp_ref, bias_first_ref, bias_ref, hm_ref = (next(it) for _ in range(8))
    o_ref = next(it)
    lse_ref = next(it) if want_lse else None
    blk = BAND_BLOCK
    for sub in range(BAND_STEP_BLOCKS):
        rows = slice(blk * sub, blk * (sub + 1))
        if sub == 0:
            k = jnp.concatenate([kp_ref[...], kc_ref[rows, :]], axis=0)
            v = jnp.concatenate([vp_ref[...], vc_ref[rows, :]], axis=0)
            bias = bias_first_ref[0]
        else:
            k = kc_ref[blk * (sub - 1):blk * (sub + 1), :]
            v = vc_ref[blk * (sub - 1):blk * (sub + 1), :]
            bias = bias_ref[0]
        _band_block(has_sink, want_lse, sink_ref, q_ref[rows, :], k.astype(BF16), v.astype(BF16), bias, hm_ref,
                    o_ref, lse_ref, rows)


def _band_block(has_sink, want_lse, sink_ref, q, k, v, bias, hm_ref, o_ref, lse_ref, rows):
    blk = BAND_BLOCK
    bias4 = jnp.concatenate([bias] * N_KV_HEADS, axis=0)
    lane_head = lax.broadcasted_iota(jnp.int32, (blk, KV_WIDTH), 1) // HEAD_DIM
    lane = lax.broadcasted_iota(jnp.int32, (blk, LANES), 1)
    lse_tile = jnp.zeros((blk, LANES), F32)
    row_head = lax.broadcasted_iota(jnp.int32, (N_KV_HEADS * blk, 1), 0) // blk

    for g4 in range(GQA_GROUP):
        qg = q[:, KV_WIDTH * g4:KV_WIDTH * (g4 + 1)]
        lhs = jnp.concatenate([qg * hm_ref[h] for h in range(N_KV_HEADS)], axis=0)
        s = _dot_nt(lhs, k) + bias4
        m = jnp.max(s, axis=1, keepdims=True)
        if has_sink:
            sink_col = jnp.zeros((N_KV_HEADS * blk, 1), F32)
            for h in range(N_KV_HEADS):
                sink_col = jnp.where(row_head == h, sink_ref[h * GQA_GROUP + g4], sink_col)
            m = jnp.maximum(m, sink_col)
        p = jnp.exp(s - m)
        pv = _dot(p.astype(BF16), v)
        l = jnp.sum(p, axis=1, keepdims=True)
        if has_sink:
            l = l + jnp.exp(sink_col - m)
        on = pv * (1.0 / l)
        og = jnp.zeros((blk, KV_WIDTH), F32)
        for h in range(N_KV_HEADS):
            og = jnp.where(lane_head == h, on[blk * h:blk * (h + 1)], og)
            if want_lse:
                lse_h = m[blk * h:blk * (h + 1)] + jnp.log(l[blk * h:blk * (h + 1)])
                lse_tile = jnp.where(lane == g4 * N_KV_HEADS + h, lse_h, lse_tile)
        o_ref[rows, KV_WIDTH * g4:KV_WIDTH * (g4 + 1)] = og.astype(o_ref.dtype)
    if want_lse:
        lse_ref[rows, :] = lse_tile


def _band_bias(max_dist):
    qi = np.arange(BAND_BLOCK)[:, None]
    mi = np.arange(2 * BAND_BLOCK)[None, :]
    dist = BAND_BLOCK + qi - mi
    ok = (dist >= 0) & (dist <= max_dist)
    first = ok & (mi >= BAND_BLOCK)
    return np.where(np.stack([first, ok]), 0.0, NEG_INF).astype(np.float32)


def _head_mask(rows, dtype):
    lane_head = np.arange(KV_WIDTH)[None, None, :] // HEAD_DIM
    hm = (lane_head == np.arange(N_KV_HEADS)[:, None, None]).astype(np.float32)
    return jnp.asarray(np.broadcast_to(hm, (N_KV_HEADS, rows, KV_WIDTH)), dtype)


def _band_attention(q, k, v, sinks, n_seq, seq_len, dil, max_dist, out_dtype, want_lse):
    blk = BAND_BLOCK
    step = BAND_STEP_BLOCKS * blk
    tp = n_seq * seq_len
    n_steps = seq_len // dil // step
    q2, k2, v2 = q, k, v
    cur = lambda b, r, n: (b * n_steps + n, r)
    prev = lambda b, r, n: (BAND_STEP_BLOCKS * (b * n_steps + n) - jnp.minimum(n, 1), r)
    in_specs, args = [], []
    if sinks is not None:
        in_specs.append(pl.BlockSpec(memory_space=pltpu.SMEM))
        args.append(sinks)
    bias = jnp.asarray(_band_bias(max_dist))
    in_specs += [pl.BlockSpec((step, Q_WIDTH), cur),
                 pl.BlockSpec((step, KV_WIDTH), cur), pl.BlockSpec((blk, KV_WIDTH), prev),
                 pl.BlockSpec((step, KV_WIDTH), cur), pl.BlockSpec((blk, KV_WIDTH), prev),
                 pl.BlockSpec((1, blk, 2 * blk), lambda b, r, n: (jnp.minimum(n, 1), 0, 0)),
                 pl.BlockSpec((1, blk, 2 * blk), lambda b, r, n: (1, 0, 0)),
                 pl.BlockSpec((N_KV_HEADS, blk, KV_WIDTH), lambda b, r, n: (0, 0, 0))]
    args += [q2, k2, k2, v2, v2, bias, bias, _head_mask(blk, BF16)]
    out_shape = [jax.ShapeDtypeStruct((tp // dil, dil * Q_WIDTH), out_dtype)]
    out_specs = [pl.BlockSpec((step, Q_WIDTH), cur)]
    if want_lse:
        out_shape.append(jax.ShapeDtypeStruct((tp // dil, dil * LANES), F32))
        out_specs.append(pl.BlockSpec((step, LANES), cur))
    res = pl.pallas_call(
        functools.partial(_band_body, sinks is not None, want_lse),
        grid=(n_seq, dil, n_steps), in_specs=in_specs, out_specs=out_specs, out_shape=out_shape,
        compiler_params=_params(3), name=f"band_d{dil}")(*args)
    return res[0], (res[1] if want_lse else None)


def _sample_body(has_sink, want_lse, n_new, seqs, *refs):
    it = iter(refs)
    sink_ref = next(it) if has_sink else None
    q_ref, kn_ref, vn_ref, cache_ref, bp_ref, bn_ref, hm_ref = (next(it) for _ in range(7))
    o_ref = next(it)
    lse_ref = next(it) if want_lse else None
    n_past = cache_ref.shape[-1]
    rows = N_Q_HEADS * n_new
    lane_head = lax.broadcasted_iota(jnp.int32, (n_new, KV_WIDTH), 1) // HEAD_DIM
    lane = lax.broadcasted_iota(jnp.int32, (n_new, LANES), 1)
    zpad = jnp.zeros((LANES - n_new, KV_WIDTH), F32)
    row_head = lax.broadcasted_iota(jnp.int32, (rows, 1), 0) // n_new

    for sb in range(seqs):
        rs = slice(n_new * sb, n_new * (sb + 1))
        q = q_ref[rs, :]
        lhs = jnp.concatenate(
            [q[:, KV_WIDTH * g4:KV_WIDTH * (g4 + 1)] * hm_ref[h]
             for g4 in range(GQA_GROUP) for h in range(N_KV_HEADS)], axis=0).astype(BF16)
        kt = cache_ref[sb, 0].reshape(KV_WIDTH, n_past).astype(BF16)
        vt = cache_ref[sb, 1].reshape(KV_WIDTH, n_past).astype(BF16)
        kn = jnp.concatenate([kn_ref[rs, :], zpad], axis=0).astype(BF16)
        vn = jnp.concatenate([vn_ref[rs, :], zpad], axis=0).astype(BF16)
        s1 = _dot(lhs, kt) + bp_ref[...]
        s2 = _dot_nt(lhs, kn) + bn_ref[...]
        m = jnp.maximum(jnp.max(s1, axis=1, keepdims=True), jnp.max(s2, axis=1, keepdims=True))
        if has_sink:
            sink_col = jnp.zeros((rows, 1), F32)
            for g4 in range(GQA_GROUP):
                for h in range(N_KV_HEADS):
                    sink_col = jnp.where(row_head == g4 * N_KV_HEADS + h, sink_ref[h * GQA_GROUP + g4], sink_col)
            m = jnp.maximum(m, sink_col)
        p1 = jnp.exp(s1 - m)
        p2 = jnp.exp(s2 - m)
        l = jnp.sum(p1, axis=1, keepdims=True) + jnp.sum(p2, axis=1, keepdims=True)
        if has_sink:
            l = l + jnp.exp(sink_col - m)
        pv = _dot_nt(p1.astype(BF16), vt) + _dot(p2.astype(BF16), vn)
        on = pv * (1.0 / l)
        lse = m + jnp.log(l)
        lse_tile = jnp.zeros((n_new, LANES), F32)
        for g4 in range(GQA_GROUP):
            og = jnp.zeros((n_new, KV_WIDTH), F32)
            for h in range(N_KV_HEADS):
                r0 = (g4 * N_KV_HEADS + h) * n_new
                og = jnp.where(lane_head == h, on[r0:r0 + n_new], og)
                if want_lse:
                    lse_tile = jnp.where(lane == g4 * N_KV_HEADS + h, lse[r0:r0 + n_new], lse_tile)
            o_ref[rs, KV_WIDTH * g4:KV_WIDTH * (g4 + 1)] = og.astype(o_ref.dtype)
        if want_lse:
            lse_ref[rs, :] = lse_tile


def _sample_bias(n_new, n_past, n_keys, dil):
    j = (np.arange(N_Q_HEADS * n_new) % n_new)[:, None]
    span = dil * (n_keys - 1)

    def ok(delta):
        return (delta >= 0) & (delta % dil == 0) & (delta <= span)

    past = ok(n_past + j - np.arange(n_past)[None, :])
    i_new = np.arange(LANES)[None, :]
    new = ok(j - i_new) & (i_new < n_new)
    to_bias = lambda m: jnp.asarray(np.where(m, 0.0, NEG_INF).astype(np.float32))
    return to_bias(past), to_bias(new)


def _sample_attention(q, k_new, v_new, cache, sinks, n_new, n_keys, dil, out_dtype, want_lse):
    n_seq, n_past = cache.shape[0], cache.shape[-1]
    seqs = 4 if n_past <= 512 else 2
    rows = seqs * n_new
    bias_past, bias_new = _sample_bias(n_new, n_past, n_keys, dil)
    n_rows = N_Q_HEADS * n_new
    row = lambda i: (i, 0)
    const = lambda i: (0, 0)
    in_specs, args = [], []
    if sinks is not None:
        in_specs.append(pl.BlockSpec(memory_space=pltpu.SMEM))
        args.append(sinks)
    in_specs += [pl.BlockSpec((rows, Q_WIDTH), row), pl.BlockSpec((rows, KV_WIDTH), row),
                 pl.BlockSpec((rows, KV_WIDTH), row),
                 pl.BlockSpec((seqs, 2, N_KV_HEADS, HEAD_DIM, n_past), lambda i: (i, 0, 0, 0, 0)),
                 pl.BlockSpec((n_rows, n_past), const), pl.BlockSpec((n_rows, LANES), const),
                 pl.BlockSpec((N_KV_HEADS, n_new, KV_WIDTH), lambda i: (0, 0, 0))]
    args += [q, k_new, v_new, cache, bias_past, bias_new, _head_mask(n_new, F32)]
    out_shape = [jax.ShapeDtypeStruct((n_seq * n_new, Q_WIDTH), out_dtype)]
    out_specs = [pl.BlockSpec((rows, Q_WIDTH), row)]
    if want_lse:
        out_shape.append(jax.ShapeDtypeStruct((n_seq * n_new, LANES), F32))
        out_specs.append(pl.BlockSpec((rows, LANES), row))
    res = pl.pallas_call(
        functools.partial(_sample_body, sinks is not None, want_lse, n_new, seqs),
        grid=(n_seq // seqs,), in_specs=in_specs, out_specs=out_specs, out_shape=out_shape,
        compiler_params=_params(1), name=f"sample_d{dil}")(*args)
    return res[0], (res[1] if want_lse else None)


def _route_tile(logits, tri_ref, counts_ref):
    tm = logits.shape[0]
    lane = lax.broadcasted_iota(jnp.int32, (tm, LANES), 1)
    lane_f = lane.astype(F32)
    none = float(LANES)
    first_lane = lambda hit: jnp.min(jnp.where(hit, lane_f, none), axis=1, keepdims=True)

    gl = jnp.where(lane < N_EXPERT_GROUPS, logits, NEG_INF)
    gmax = jnp.max(gl, axis=1, keepdims=True)
    g_idx = first_lane(gl == gmax)
    g_w = 1.0 / jnp.sum(jnp.exp(gl - gmax), axis=1, keepdims=True)

    e_lane = lane - N_EXPERT_GROUPS
    lane_group = jnp.where((e_lane >= 0) & (e_lane < N_EXPERTS), (e_lane // EXPERTS_PER_GROUP).astype(F32), none)
    el = jnp.where(lane_group == g_idx, logits, NEG_INF)
    v1 = jnp.max(el, axis=1, keepdims=True)
    i1 = first_lane(el == v1)
    el2 = jnp.where(lane_f == i1, NEG_INF, el)
    v2 = jnp.max(el2, axis=1, keepdims=True)
    i2 = first_lane(el2 == v2)
    t = jnp.exp(v2 - v1)
    den = 1.0 + t
    w1 = g_w * (1.0 / den)
    w2 = g_w * (t / den)
    e1 = i1 - float(N_EXPERT_GROUPS)
    e2 = i2 - float(N_EXPERT_GROUPS)

    hot1 = lane_f == e1
    hot2 = lane_f == e2
    both = jnp.where(hot1, 1.0, 0.0) + jnp.where(hot2, 1.0, 0.0)
    before = _dot(tri_ref[...], both.astype(BF16)) + counts_ref[0:1, :]
    r1 = jnp.sum(jnp.where(hot1, before, 0.0), axis=1, keepdims=True)
    r2 = jnp.sum(jnp.where(hot2, before, 0.0), axis=1, keepdims=True)
    counts_ref[...] = counts_ref[...] + jnp.sum(both, axis=0, keepdims=True)

    info = jnp.zeros((tm, LANES), F32)
    for j, col in enumerate((e1, e2, r1, r2, w1, w2)):
        info = jnp.where(lane == j, col, info)
    return info


def _outproj_body(dils, n_ptiles, *refs):
    n_groups = len(dils)
    it = iter(refs)
    o_refs = [(next(it), next(it)) for _ in range(n_groups)]
    lse_refs = [(next(it), next(it)) for _ in range(n_groups)] if n_groups > 1 else None
    expand_ref = next(it) if n_groups > 1 else None
    h_ref, wo_ref, g_ref, wr_hi_ref, wr_lo_ref, br_ref, tri_ref = (next(it) for _ in range(7))
    hout_ref, xn_ref, info_ref, counts_ref = next(it), next(it), next(it), next(it)
    stages = {g: (next(it), next(it)) for g in range(n_groups) if dils[g] > 1}

    @pl.when(pl.program_id(0) == 0)
    def _():
        counts_ref[...] = jnp.zeros(counts_ref.shape, F32)

    is_p = pl.program_id(0) < n_ptiles

    def pick(pair, g, which):
        prompt_ref, sample_ref = pair
        dil, width = dils[g], sample_ref.shape[1]
        if dil == 1:
            prompt = prompt_ref[...]
        else:
            stage = stages[g][which]
            rows = prompt_ref.shape[0]
            for r in range(dil):
                for j in range(width // LANES):
                    c0 = r * width + LANES * j
                    stage[j, pl.ds(r, rows, stride=dil), :] = prompt_ref[:, c0:c0 + LANES]
            prompt = jnp.concatenate([stage[j] for j in range(width // LANES)], axis=1)
        return jnp.where(is_p, prompt, sample_ref[...])

    if n_groups == 1:
        o = pick(o_refs[0], 0, 0)
    else:
        lses = [pick(p, g, 1) for g, p in enumerate(lse_refs)]
        mx = functools.reduce(jnp.maximum, lses)
        es = [jnp.exp(x - mx) for x in lses]
        inv = 1.0 / functools.reduce(lambda a, b: a + b, es)
        acc, wide_sum = None, None
        for g in range(n_groups):
            if g < n_groups - 1:
                w = es[g] * inv
                w_hi = w.astype(BF16)
                w_lo = (w - w_hi.astype(F32)).astype(BF16)
                wide = _dot(w_hi, expand_ref[...]) + _dot(w_lo, expand_ref[...])
                wide_sum = wide if wide_sum is None else wide_sum + wide
            else:
                wide = 1.0 - wide_sum
            term = wide * pick(o_refs[g], g, 0)
            acc = term if acc is None else acc + term
        o = acc.astype(BF16)
    h_new = h_ref[...] + _dot(o, wo_ref[...])
    hout_ref[...] = h_new
    xn = _rms(h_new, g_ref[...])
    xn_ref[...] = xn
    x_hi = xn.astype(BF16)
    x_lo = (xn - x_hi.astype(F32)).astype(BF16)
    logits = (_dot(x_hi, wr_hi_ref[...]) + _dot(x_hi, wr_lo_ref[...])
              + _dot(x_lo, wr_hi_ref[...]) + br_ref[...])
    info_ref[...] = _route_tile(logits, tri_ref, counts_ref)


def _out_project(o_pairs, lse_pairs, dils, h, wo, gain, wr_hi, wr_lo, br, n_ptiles):
    tm = ROW_TILE
    n_tok = h.shape[0]
    n_groups = len(o_pairs)
    row = lambda i: (i, 0)
    const = lambda i: (0, 0)
    prow = lambda i: (jnp.minimum(i, n_ptiles - 1), 0)
    srow = lambda i: (jnp.maximum(i - n_ptiles, 0), 0)
    in_specs, args, scratch = [], [], []
    for (op, os_), d in zip(o_pairs, dils):
        in_specs += [pl.BlockSpec((tm // d, d * Q_WIDTH), prow), pl.BlockSpec((tm, Q_WIDTH), srow)]
        args += [op, os_]
        if d > 1:
            scratch += [pltpu.VMEM((Q_WIDTH // LANES, tm, LANES), F32), pltpu.VMEM((1, tm, LANES), F32)]
    if n_groups > 1:
        for (lp, ls), d in zip(lse_pairs, dils):
            in_specs += [pl.BlockSpec((tm // d, d * LANES), prow), pl.BlockSpec((tm, LANES), srow)]
            args += [lp, ls]
        expand = (np.arange(LANES)[:, None] == np.arange(Q_WIDTH)[None, :] // HEAD_DIM)
        in_specs.append(pl.BlockSpec((LANES, Q_WIDTH), const))
        args.append(jnp.asarray(expand.astype(np.float32), BF16))
    in_specs += [pl.BlockSpec((tm, D_MODEL), row), pl.BlockSpec((Q_WIDTH, D_MODEL), const),
                 pl.BlockSpec((1, D_MODEL), const), pl.BlockSpec((D_MODEL, LANES), const),
                 pl.BlockSpec((D_MODEL, LANES), const), pl.BlockSpec((1, LANES), const),
                 pl.BlockSpec((tm, tm), const)]
    strictly_lower = np.tril(np.ones((tm, tm), np.float32), -1)
    args += [h, wo, gain, wr_hi, wr_lo, br, jnp.asarray(strictly_lower, BF16)]
    out_shape = [jax.ShapeDtypeStruct((n_tok, D_MODEL), F32), jax.ShapeDtypeStruct((n_tok, D_MODEL), F32),
                 jax.ShapeDtypeStruct((n_tok, LANES), F32), jax.ShapeDtypeStruct((SUBLANES, LANES), F32)]
    out_specs = [pl.BlockSpec((tm, D_MODEL), row), pl.BlockSpec((tm, D_MODEL), row),
                 pl.BlockSpec((tm, LANES), row), pl.BlockSpec((SUBLANES, LANES), const)]
    return pl.pallas_call(
        functools.partial(_outproj_body, tuple(dils), n_ptiles),
        grid=(n_tok // tm,), in_specs=in_specs, out_specs=out_specs, out_shape=out_shape,
        scratch_shapes=scratch, compiler_params=_params(1), name=f"outproj_{n_groups}")(*args)


GATHER_AHEAD = 2
GATHER_BUFFERS = GATHER_AHEAD + 1


def _moe_body(layer, block_e_ref, next_e_ref, parity_ref, n_used_ref, first_ref, token_ref,
              xn_hbm, wg_hbm, wu_hbm, wd_hbm, y_ref,
              xbuf, wg_f32, wu_f32, wd_f32, wg_bf, wu_bf, wd_bf, sem_x, sem):
    i = pl.program_id(0)
    n_used = n_used_ref[0]
    last_used = jnp.maximum(n_used - 1, 0)

    def fetch(e, sl):
        return [pltpu.make_async_copy(src.at[layer, e], dst.at[sl], sem.at[sl])
                for src, dst in ((wg_hbm, wg_f32), (wu_hbm, wu_f32), (wd_hbm, wd_f32))]

    def gather(block, sl):
        first = first_ref[block]
        for r in range(EXPERT_BLOCK):
            tok = token_ref[first + r]
            pltpu.make_async_copy(xn_hbm.at[pl.ds(tok, 1)], xbuf.at[sl, pl.ds(r, 1)], sem_x.at[sl]).start()

    def wait_rows(sl):
        pltpu.make_async_copy(xn_hbm.at[pl.ds(0, EXPERT_BLOCK)], xbuf.at[sl], sem_x.at[sl]).wait()

    @pl.when((i == 0) & (n_used > 0))
    def _():
        for copy in fetch(block_e_ref[0], parity_ref[0]):
            copy.start()
        for b in range(GATHER_AHEAD):
            gather(jnp.minimum(b, last_used), b)

    @pl.when(i < n_used)
    def _():
        slot = i % GATHER_BUFFERS
        wait_rows(slot)
        changed = (i == 0) | (block_e_ref[i] != block_e_ref[jnp.maximum(i - 1, 0)])

        @pl.when(changed)
        def _():
            sl = parity_ref[i]
            for copy in fetch(block_e_ref[i], sl):
                copy.wait()

            @pl.when(next_e_ref[i] >= 0)
            def _():
                for copy in fetch(next_e_ref[i], 1 - sl):
                    copy.start()

            wg_bf[...] = wg_f32[sl].astype(BF16)
            wu_bf[...] = wu_f32[sl].astype(BF16)
            wd_bf[...] = wd_f32[sl].astype(BF16)

        x = xbuf[slot].astype(BF16)
        gather(jnp.minimum(i + GATHER_AHEAD, last_used), (i + GATHER_AHEAD) % GATHER_BUFFERS)
        gate = _dot(x, wg_bf[...])
        up = _dot(x, wu_bf[...])
        mid = (gate * jax.nn.sigmoid(gate) * up).astype(BF16)
        y_ref[...] = _dot(mid, wd_bf[...])

        @pl.when(i == last_used)
        def _():
            for b in range(1, GATHER_BUFFERS):
                wait_rows((i + b) % GATHER_BUFFERS)

    @pl.when(i >= n_used)
    def _():
        y_ref[...] = jnp.zeros(y_ref.shape, F32)


def _moe_experts(xn, tokens, first, block_e, next_e, parity, n_used, n_slots, w_gate, w_up, w_down, layer):
    grid_spec = pltpu.PrefetchScalarGridSpec(
        num_scalar_prefetch=6, grid=(n_slots // EXPERT_BLOCK,),
        in_specs=[pl.BlockSpec(memory_space=pl.ANY), pl.BlockSpec(memory_space=pl.ANY),
                  pl.BlockSpec(memory_space=pl.ANY), pl.BlockSpec(memory_space=pl.ANY)],
        out_specs=pl.BlockSpec((EXPERT_BLOCK, D_MODEL), lambda i, *_: (i, 0)),
        scratch_shapes=[pltpu.VMEM((GATHER_BUFFERS, EXPERT_BLOCK, D_MODEL), F32),
                        pltpu.VMEM((2, D_MODEL, D_FF_EXPERT), F32),
                        pltpu.VMEM((2, D_MODEL, D_FF_EXPERT), F32),
                        pltpu.VMEM((2, D_FF_EXPERT, D_MODEL), F32),
                        pltpu.VMEM((D_MODEL, D_FF_EXPERT), BF16),
                        pltpu.VMEM((D_MODEL, D_FF_EXPERT), BF16),
                        pltpu.VMEM((D_FF_EXPERT, D_MODEL), BF16),
                        pltpu.SemaphoreType.DMA((GATHER_BUFFERS,)), pltpu.SemaphoreType.DMA((2,))])
    return pl.pallas_call(
        functools.partial(_moe_body, layer), grid_spec=grid_spec,
        out_shape=jax.ShapeDtypeStruct((n_slots, D_MODEL), F32),
        compiler_params=_params(1), name=f"moe_{layer}")(
            block_e, next_e, parity, n_used, first, tokens, xn, w_gate, w_up, w_down)


def _slot_layout(info, counts_tile, n_slots):
    counts = counts_tile[0, :N_EXPERTS].astype(jnp.int32)
    expert = info[:, 0:TOP_K_INNER].astype(jnp.int32)
    rank = info[:, TOP_K_INNER:2 * TOP_K_INNER].astype(jnp.int32)
    padded = (counts + EXPERT_BLOCK - 1) // EXPERT_BLOCK * EXPERT_BLOCK
    pad_end = jnp.cumsum(padded)
    pad_start = pad_end - padded
    hot = expert[..., None] == jnp.arange(N_EXPERTS, dtype=jnp.int32)
    dest = (jnp.sum(jnp.where(hot, pad_start, 0), axis=-1) + rank).reshape(-1)
    block_start = jnp.arange(n_slots // EXPERT_BLOCK, dtype=jnp.int32) * EXPERT_BLOCK
    block_e = jnp.minimum(jnp.sum(pad_end[None, :] <= block_start[:, None], axis=1), N_EXPERTS - 1)
    n_used = pad_end[-1] // EXPERT_BLOCK
    own_end = jnp.sum(jnp.where(block_e[:, None] == jnp.arange(N_EXPERTS), pad_end, 0), axis=1)
    after = own_end // EXPERT_BLOCK
    next_e = jnp.where(after < n_used, jnp.sum(pad_end[None, :] <= (after * EXPERT_BLOCK)[:, None], axis=1), -1)
    changed = jnp.concatenate([jnp.zeros((1,), jnp.int32), (block_e[1:] != block_e[:-1]).astype(jnp.int32)])
    parity = jnp.cumsum(changed) % 2
    order = jnp.argsort(expert.reshape(-1), stable=True)
    tokens = jnp.concatenate([order // TOP_K_INNER, jnp.zeros((EXPERT_BLOCK,), order.dtype)])
    seg_start = jnp.cumsum(counts) - counts
    own = lambda table: jnp.sum(jnp.where(block_e[:, None] == jnp.arange(N_EXPERTS), table, 0), axis=1)
    first = own(seg_start) + block_start - own(pad_start)
    i32 = lambda a: a.astype(jnp.int32)
    return (i32(dest), i32(tokens), i32(first), i32(block_e), i32(next_e), i32(parity), i32(n_used).reshape(1))


def _start_row_gather(dest_ref, ys_hbm, ybuf, sem, tile, sl):
    for t in range(ROW_TILE):
        for k in range(TOP_K_INNER):
            d = dest_ref[TOP_K_INNER * (tile * ROW_TILE + t) + k]
            pltpu.make_async_copy(ys_hbm.at[pl.ds(d, 1)], ybuf.at[sl, k, pl.ds(t, 1)], sem.at[sl]).start()


def _moe_output_tile(dest_ref, info_ref, ys_hbm, ybuf, sem, tile0):
    i = pl.program_id(0)
    last = pl.num_programs(0) - 1
    sl = i % 2

    def wait(slot):
        for k in range(TOP_K_INNER):
            pltpu.make_async_copy(ys_hbm.at[pl.ds(0, ROW_TILE)], ybuf.at[slot, k], sem.at[slot]).wait()

    @pl.when(i == 0)
    def _():
        _start_row_gather(dest_ref, ys_hbm, ybuf, sem, tile0, 0)

    @pl.when(i < last)
    def _():
        _start_row_gather(dest_ref, ys_hbm, ybuf, sem, tile0 + i + 1, 1 - sl)

    wait(sl)
    info = info_ref[...]
    w_lane = 2 * TOP_K_INNER
    return ybuf[sl, 0] * info[:, w_lane:w_lane + 1] + ybuf[sl, 1] * info[:, w_lane + 1:w_lane + 2]


_COMBINE_SCRATCH = [pltpu.VMEM((2, TOP_K_INNER, ROW_TILE, D_MODEL), F32), pltpu.SemaphoreType.DMA((2,))]


def _final_body(tile0, dest_ref, h_ref, info_ref, ys_hbm, g_ref, o_ref, ybuf, sem):
    y = _moe_output_tile(dest_ref, info_ref, ys_hbm, ybuf, sem, tile0)
    o_ref[...] = _rms(h_ref[...] + y, g_ref[...])


def _final_norm(h, info, ys, dest, gain, tile0, n_tiles):
    tm = ROW_TILE
    grid_spec = pltpu.PrefetchScalarGridSpec(
        num_scalar_prefetch=1, grid=(n_tiles,),
        in_specs=[pl.BlockSpec((tm, D_MODEL), lambda i, d: (i + tile0, 0)),
                  pl.BlockSpec((tm, LANES), lambda i, d: (i + tile0, 0)),
                  pl.BlockSpec(memory_space=pl.ANY),
                  pl.BlockSpec((1, D_MODEL), lambda i, d: (0, 0))],
        out_specs=pl.BlockSpec((tm, D_MODEL), lambda i, d: (i, 0)),
        scratch_shapes=_COMBINE_SCRATCH)
    return pl.pallas_call(
        functools.partial(_final_body, tile0), grid_spec=grid_spec,
        out_shape=jax.ShapeDtypeStruct((n_tiles * tm, D_MODEL), F32),
        compiler_params=_params(1), name="final_norm")(dest, h, info, ys, gain)


def _heads_g4_major(w, axis):
    shape = w.shape
    split = shape[:axis] + (N_KV_HEADS, GQA_GROUP, HEAD_DIM) + shape[axis + 1:]
    return jnp.swapaxes(w.reshape(split), axis, axis + 1).reshape(shape)


def _prep_qkv_weight(w, n_groups):
    cols = []
    for g in range(n_groups):
        base = g * QKV_WIDTH
        cols.append(_heads_g4_major(w[:, base:base + Q_WIDTH], 1) * ATTN_SCALE)
        cols.append(w[:, base + Q_WIDTH:base + QKV_WIDTH])
    return jnp.concatenate(cols, axis=1).astype(BF16)


def _rope_tables(seq_len, past_len, n_new):
    inv_freq = ROPE_THETA ** (-jnp.arange(0, ROT_DIM, 2, dtype=F32) / ROT_DIM)
    pos_p = jnp.arange(seq_len, dtype=jnp.int32)
    pos_s = past_len + (jnp.arange(ROW_TILE, dtype=jnp.int32) % n_new)
    ang = jnp.concatenate([pos_p, pos_s]).astype(F32)[:, None] * inv_freq[None, :]
    cos, sin = jnp.cos(ang), jnp.sin(ang)
    in_head = jnp.arange(LANES, dtype=jnp.int32) % HEAD_DIM
    spread = lambda t: jnp.tile(t, (1, LANES // ROT_HALF))
    c = jnp.where(in_head < ROT_DIM, spread(cos), 1.0)
    s1 = jnp.where(in_head < ROT_HALF, -spread(sin), 0.0)
    s2 = jnp.where((in_head >= ROT_HALF) & (in_head < ROT_DIM), spread(sin), 0.0)
    return c, s1, s2


def _split_hi_lo(w):
    hi = w.astype(BF16)
    return hi, (w - hi.astype(F32)).astype(BF16)


def _state_rows(k, v, lead):
    kv = jnp.stack([k, v], axis=-2)
    return kv.reshape(lead + (k.shape[-2], 2, N_KV_HEADS, HEAD_DIM))


def kernel(x_prompt, x_sample, cache_a_kv, cache_b_kv0, cache_b_kv1, cache_b_kv2, norm_mix, norm_ffn, norm_final, a_w_qkv, a_sinks, a_w_o, b_w_qkv, b_w_o, moe_router_g_w, moe_router_g_b, moe_router_e_w, moe_router_e_b, moe_w_gate, moe_w_up, moe_w_down):
    n_seq, seq_len, _ = x_prompt.shape
    n_dec, n_new, _ = x_sample.shape
    tp, ts = n_seq * seq_len, n_dec * n_new
    n_tok = tp + ts
    n_ptiles = tp // ROW_TILE
    tiles_per_seq = seq_len // ROW_TILE
    n_slots = (-(-(n_tok * TOP_K_INNER) // EXPERT_BLOCK) + N_EXPERTS) * EXPERT_BLOCK
    caches_b = (cache_b_kv0, cache_b_kv1, cache_b_kv2)

    tables = _rope_tables(seq_len, PAST_LEN, n_new)
    xp = x_prompt.reshape(tp, D_MODEL)
    xs = x_sample.reshape(ts, D_MODEL)

    def keys_on_lanes(cache):
        return jnp.transpose(cache, (0, 2, 3, 4, 1))

    def router(i):
        w = jnp.concatenate([moe_router_g_w[i], moe_router_e_w[i],
                             jnp.zeros((D_MODEL, LANES - N_EXPERT_GROUPS - N_EXPERTS), F32)], axis=1)
        b = jnp.concatenate([moe_router_g_b[i], moe_router_e_b[i],
                             jnp.zeros((LANES - N_EXPERT_GROUPS - N_EXPERTS,), F32)]).reshape(1, LANES)
        return _split_hi_lo(w) + (b,)

    def moe(i, xn, info, counts_tile):
        dest, tokens, first, block_e, next_e, parity, n_used = _slot_layout(info, counts_tile, n_slots)
        ys = _moe_experts(xn, tokens, first, block_e, next_e, parity, n_used, n_slots,
                          moe_w_gate, moe_w_up, moe_w_down, i)
        return ys, dest

    def sample_rows(t, d, width):
        return t[tp // d:].reshape(ts, width)

    def window_state(k, v, n_keep, d):
        tail = lambda t: jnp.stack([t[((b + 1) * seq_len - n_keep) // d:(b + 1) * seq_len // d].reshape(n_keep, KV_WIDTH)
                                    for b in range(n_seq)])
        return _state_rows(tail(k), tail(v), (n_seq,))[None]

    def new_rows_state(k, v, d):
        rows = lambda t: sample_rows(t, d, KV_WIDTH).reshape(n_dec, n_new, KV_WIDTH)
        return _state_rows(rows(k), rows(v), (n_dec,))[None]

    def decode(q, k, v, d, cache, sinks, n_keys, out_dtype, want_lse):
        return _sample_attention(sample_rows(q, d, Q_WIDTH).astype(F32), sample_rows(k, d, KV_WIDTH),
                                 sample_rows(v, d, KV_WIDTH), keys_on_lanes(cache), sinks,
                                 n_new, n_keys, d, out_dtype, want_lse)

    w_a = _prep_qkv_weight(a_w_qkv[0], 1)
    h0, [(q, k, v)] = _project("split", (xp, xs), norm_mix[0:1], w_a, tables, (1,), n_tok, n_ptiles, tiles_per_seq)
    sinks = a_sinks[0].astype(F32)
    o_p, _ = _band_attention(q, k, v, sinks, n_seq, seq_len, 1, WINDOW_A - 1, BF16, False)
    o_s, _ = decode(q, k, v, 1, cache_a_kv[0], sinks, WINDOW_A, BF16, False)
    new_a_p = window_state(k, v, min(WINDOW_A, seq_len), 1)
    new_a_s = new_rows_state(k, v, 1)
    wr_hi, wr_lo, br = router(0)
    h1, xn, info, counts_tile = _out_project([(o_p, o_s)], None, (1,), h0, _heads_g4_major(a_w_o[0], 0).astype(BF16),
                                             norm_ffn[0:1], wr_hi, wr_lo, br, n_ptiles)
    ys, dest = moe(0, xn, info, counts_tile)

    w_b = _prep_qkv_weight(b_w_qkv[0], len(B_PATTERNS))
    dils = tuple(d for _, d in B_PATTERNS)
    h2, groups = _project("combine", (h1, info, ys, dest), norm_mix[1:2], w_b, tables, dils,
                          n_tok, n_ptiles, tiles_per_seq)
    o_pairs, lse_pairs, new_b = [], [], []
    for g, (window, dil) in enumerate(B_PATTERNS):
        q, k, v = groups[g]
        o_p, lse_p = _band_attention(q, k, v, None, n_seq, seq_len, dil, window // dil, F32, True)
        o_s, lse_s = decode(q, k, v, dil, caches_b[g][0], None, window // dil + 1, F32, True)
        o_pairs.append((o_p, o_s))
        lse_pairs.append((lse_p, lse_s))
        new_b.append(window_state(k, v, min(window, seq_len), dil))
        new_b.append(new_rows_state(k, v, dil))
    wr_hi, wr_lo, br = router(1)
    h3, xn, info, counts_tile = _out_project(o_pairs, lse_pairs, dils, h2, _heads_g4_major(b_w_o[0], 0).astype(BF16),
                                             norm_ffn[1:2], wr_hi, wr_lo, br, n_ptiles)
    ys, dest = moe(1, xn, info, counts_tile)

    gain = norm_final.reshape(1, D_MODEL)
    y_p = _final_norm(h3, info, ys, dest, gain, 0, n_ptiles).reshape(n_seq, seq_len, D_MODEL)
    y_s = _final_norm(h3, info, ys, dest, gain, n_ptiles, ts // ROW_TILE).reshape(n_dec, n_new, D_MODEL)
    return (y_p, y_s, new_a_p, new_a_s, *new_b)
```

```python
import functools

import numpy as np
import jax
import jax.numpy as jnp
from jax import lax
from jax.experimental import pallas as pl
from jax.experimental.pallas import tpu as pltpu

D_MODEL = 1024
HEAD_DIM = 64
N_KV_HEADS = 4
GQA_GROUP = 4
N_Q_HEADS = N_KV_HEADS * GQA_GROUP
Q_WIDTH = N_Q_HEADS * HEAD_DIM
KV_WIDTH = N_KV_HEADS * HEAD_DIM
QKV_WIDTH = Q_WIDTH + 2 * KV_WIDTH
ROT_DIM = HEAD_DIM // 4
ROT_HALF = ROT_DIM // 2
ROPE_THETA = 500000.0
ATTN_SCALE = HEAD_DIM ** -0.5
WINDOW_A = 128
B_PATTERNS = ((128, 1), (512, 4), (2048, 16))
BAND_BLOCK = 128
N_EXPERT_GROUPS = 8
EXPERTS_PER_GROUP = 8
N_EXPERTS = N_EXPERT_GROUPS * EXPERTS_PER_GROUP
TOP_K_INNER = 2
D_FF_EXPERT = 512
RMS_EPS = 1e-5
PAST_LEN = 16384

LANES = 128
SUBLANES = 8
ROW_TILE = 256
EXPERT_BLOCK = 256
VMEM_LIMIT_BYTES = 56 * 1024 * 1024

BF16 = jnp.bfloat16
F32 = jnp.float32
NEG_INF = float("-inf")


def _params(n_axes, vmem=VMEM_LIMIT_BYTES):
    return pltpu.CompilerParams(dimension_semantics=("arbitrary",) * n_axes,
                                vmem_limit_bytes=vmem)


def _rms(x, g):
    var = jnp.mean(x * x, axis=-1, keepdims=True)
    return x * lax.rsqrt(var + RMS_EPS) * g


def _dot(a, b):
    return jnp.dot(a, b, preferred_element_type=F32)


def _dot_nt(a, b):
    return lax.dot_general(a, b, (((1,), (1,)), ((), ())), preferred_element_type=F32)


def _proj_body(dils, mode, n_ptiles, *refs):
    n_groups = len(dils)
    it = iter(refs)
    if mode == "split":
        xp_ref, xs_ref = next(it), next(it)
    else:
        dest_ref, h_ref, info_ref, ys_hbm = next(it), next(it), next(it), next(it)
    g_ref, w_ref, cos_ref, s1_ref, s2_ref = (next(it) for _ in range(5))
    hout_ref = next(it)
    outs = [(next(it), next(it), next(it)) for _ in range(n_groups)]
    stage = next(it) if max(dils) > 1 else None

    if mode == "split":
        is_p = pl.program_id(0) < n_ptiles
        x = jnp.where(is_p, xp_ref[...], xs_ref[...])
    else:
        ybuf, sem = next(it), next(it)
        y, drain = _moe_output_tile(dest_ref, info_ref, ys_hbm, ybuf, sem, 0)
        x = h_ref[...] + y
    hout_ref[...] = x
    xn = _rms(x, g_ref[...]).astype(BF16)
    c, s1, s2 = cos_ref[...], s1_ref[...], s2_ref[...]

    def rope(y):
        parts = []
        for j in range(y.shape[1] // LANES):
            yb = y[:, LANES * j:LANES * (j + 1)]
            parts.append(yb * c + pltpu.roll(yb, LANES - ROT_HALF, 1) * s1
                         + pltpu.roll(yb, ROT_HALF, 1) * s2)
        return jnp.concatenate(parts, axis=1)

    def emit(out_ref, y, dil, width, col):
        n = y.shape[1]
        if dil == 1:
            out_ref[:, col:col + n] = y.astype(out_ref.dtype)
            return
        rows = y.shape[0] // dil
        for j in range(n // LANES):
            stage[j] = y[:, LANES * j:LANES * (j + 1)]
        for r in range(dil):
            for j in range(n // LANES):
                c0 = r * width + col + LANES * j
                out_ref[:, c0:c0 + LANES] = stage[j, pl.ds(r, rows, stride=dil), :].astype(out_ref.dtype)

    half = Q_WIDTH // 2
    for g in range(n_groups):
        base = g * QKV_WIDTH
        q_ref, k_ref, v_ref = outs[g]
        for hf in range(2):
            y = _dot(xn, w_ref[:, base + half * hf:base + half * (hf + 1)])
            emit(q_ref, rope(y), dils[g], Q_WIDTH, half * hf)
        y = _dot(xn, w_ref[:, base + Q_WIDTH:base + Q_WIDTH + KV_WIDTH])
        emit(k_ref, rope(y), dils[g], KV_WIDTH, 0)
        y = _dot(xn, w_ref[:, base + Q_WIDTH + KV_WIDTH:base + QKV_WIDTH])
        emit(v_ref, y, dils[g], KV_WIDTH, 0)
    if mode == "combine":
        drain()


def _project(mode, xs, gain, w, tables, dils, n_tok, n_ptiles, tiles_per_seq):
    tm = ROW_TILE
    n_tiles = n_tok // tm
    row = lambda i, *_: (i, 0)
    const = lambda i, *_: (0, 0)
    tab = lambda i, *_: (jnp.where(i < n_ptiles, i % tiles_per_seq, tiles_per_seq), 0)
    if mode == "split":
        xp, xsamp = xs
        in_specs = [pl.BlockSpec((tm, D_MODEL), lambda i: (jnp.minimum(i, n_ptiles - 1), 0)),
                    pl.BlockSpec((tm, D_MODEL), lambda i: (jnp.maximum(i - n_ptiles, 0), 0))]
        args = [xp, xsamp]
        prefetch, scratch = [], []
    else:
        h, info, ys, dest = xs
        in_specs = [pl.BlockSpec((tm, D_MODEL), row), pl.BlockSpec((tm, LANES), row),
                    pl.BlockSpec(memory_space=pl.ANY)]
        args = [h, info, ys]
        prefetch, scratch = [dest], _COMBINE_SCRATCH
    if max(dils) > 1:
        scratch = [pltpu.VMEM((Q_WIDTH // 2 // LANES, tm, LANES), F32)] + list(scratch)
    in_specs += [pl.BlockSpec((1, D_MODEL), const),
                 pl.BlockSpec(w.shape, const),
                 pl.BlockSpec((tm, LANES), tab), pl.BlockSpec((tm, LANES), tab),
                 pl.BlockSpec((tm, LANES), tab)]
    args += [gain, w, *tables]
    out_shape = [jax.ShapeDtypeStruct((n_tok, D_MODEL), F32)]
    out_specs = [pl.BlockSpec((tm, D_MODEL), row)]
    for d in dils:
        for width, dtype in ((Q_WIDTH, BF16), (KV_WIDTH, F32), (KV_WIDTH, F32)):
            out_shape.append(jax.ShapeDtypeStruct((n_tok // d, d * width), dtype))
            out_specs.append(pl.BlockSpec((tm // d, d * width), row))
    grid_spec = pltpu.PrefetchScalarGridSpec(
        num_scalar_prefetch=len(prefetch), grid=(n_tiles,), in_specs=in_specs, out_specs=out_specs,
        scratch_shapes=scratch)
    res = pl.pallas_call(
        functools.partial(_proj_body, tuple(dils), mode, n_ptiles), grid_spec=grid_spec, out_shape=out_shape,
        compiler_params=_params(1), name=f"proj_{mode}")(*prefetch, *args)
    h_out = res[0]
    groups = [tuple(res[1 + 3 * g:4 + 3 * g]) for g in range(len(dils))]
    return h_out, groups


BAND_STEP_BLOCKS = 2


def _band_body(has_sink, want_lse, *refs):
    it = iter(refs)
    sink_ref = next(it) if has_sink else None
    q_ref, kc_ref, kp_ref, vc_ref, vp_ref, bias_first_ref, bias_ref, hm_ref = (next(it) for _ in range(8))
    o_ref = next(it)
    lse_ref = next(it) if want_lse else None
    blk = BAND_BLOCK
    for sub in range(BAND_STEP_BLOCKS):
        rows = slice(blk * sub, blk * (sub + 1))
        if sub == 0:
            k = jnp.concatenate([kp_ref[...], kc_ref[rows, :]], axis=0)
            v = jnp.concatenate([vp_ref[...], vc_ref[rows, :]], axis=0)
            bias = bias_first_ref[0]
        else:
            k = kc_ref[blk * (sub - 1):blk * (sub + 1), :]
            v = vc_ref[blk * (sub - 1):blk * (sub + 1), :]
            bias = bias_ref[0]
        _band_block(has_sink, want_lse, sink_ref, q_ref[rows, :], k.astype(BF16), v.astype(BF16), bias, hm_ref,
                    o_ref, lse_ref, rows)


def _band_block(has_sink, want_lse, sink_ref, q, k, v, bias, hm_ref, o_ref, lse_ref, rows):
    blk = BAND_BLOCK
    bias4 = jnp.concatenate([bias] * N_KV_HEADS, axis=0)
    lane_head = lax.broadcasted_iota(jnp.int32, (blk, KV_WIDTH), 1) // HEAD_DIM
    lane = lax.broadcasted_iota(jnp.int32, (blk, LANES), 1)
    lse_tile = jnp.zeros((blk, LANES), F32)
    row_head = lax.broadcasted_iota(jnp.int32, (N_KV_HEADS * blk, 1), 0) // blk

    for g4 in range(GQA_GROUP):
        qg = q[:, KV_WIDTH * g4:KV_WIDTH * (g4 + 1)]
        lhs = jnp.concatenate([qg * hm_ref[h] for h in range(N_KV_HEADS)], axis=0)
        s = _dot_nt(lhs, k) + bias4
        m = jnp.max(s, axis=1, keepdims=True)
        if has_sink:
            sink_col = jnp.zeros((N_KV_HEADS * blk, 1), F32)
            for h in range(N_KV_HEADS):
                sink_col = jnp.where(row_head == h, sink_ref[h * GQA_GROUP + g4], sink_col)
            m = jnp.maximum(m, sink_col)
        p = jnp.exp(s - m)
        pv = _dot(p.astype(BF16), v)
        l = jnp.sum(p, axis=1, keepdims=True)
        if has_sink:
            l = l + jnp.exp(sink_col - m)
        on = pv * (1.0 / l)
        og = jnp.zeros((blk, KV_WIDTH), F32)
        for h in range(N_KV_HEADS):
            og = jnp.where(lane_head == h, on[blk * h:blk * (h + 1)], og)
            if want_lse:
                lse_h = m[blk * h:blk * (h + 1)] + jnp.log(l[blk * h:blk * (h + 1)])
                lse_tile = jnp.where(lane == g4 * N_KV_HEADS + h, lse_h, lse_tile)
        o_ref[rows, KV_WIDTH * g4:KV_WIDTH * (g4 + 1)] = og.astype(o_ref.dtype)
    if want_lse:
        lse_ref[rows, :] = lse_tile


def _band_bias(max_dist):
    qi = np.arange(BAND_BLOCK)[:, None]
    mi = np.arange(2 * BAND_BLOCK)[None, :]
    dist = BAND_BLOCK + qi - mi
    ok = (dist >= 0) & (dist <= max_dist)
    first = ok & (mi >= BAND_BLOCK)
    return np.where(np.stack([first, ok]), 0.0, NEG_INF).astype(np.float32)


def _head_mask(rows, dtype):
    lane_head = np.arange(KV_WIDTH)[None, None, :] // HEAD_DIM
    hm = (lane_head == np.arange(N_KV_HEADS)[:, None, None]).astype(np.float32)
    return jnp.asarray(np.broadcast_to(hm, (N_KV_HEADS, rows, KV_WIDTH)), dtype)


def _band_attention(q, k, v, sinks, n_seq, seq_len, dil, max_dist, out_dtype, want_lse):
    blk = BAND_BLOCK
    step = BAND_STEP_BLOCKS * blk
    tp = n_seq * seq_len
    n_steps = seq_len // dil // step
    q2, k2, v2 = q, k, v
    cur = lambda b, r, n: (b * n_steps + n, r)
    prev = lambda b, r, n: (BAND_STEP_BLOCKS * (b * n_steps + n) - jnp.minimum(n, 1), r)
    in_specs, args = [], []
    if sinks is not None:
        in_specs.append(pl.BlockSpec(memory_space=pltpu.SMEM))
        args.append(sinks)
    bias = jnp.asarray(_band_bias(max_dist))
    in_specs += [pl.BlockSpec((step, Q_WIDTH), cur),
                 pl.BlockSpec((step, KV_WIDTH), cur), pl.BlockSpec((blk, KV_WIDTH), prev),
                 pl.BlockSpec((step, KV_WIDTH), cur), pl.BlockSpec((blk, KV_WIDTH), prev),
                 pl.BlockSpec((1, blk, 2 * blk), lambda b, r, n: (jnp.minimum(n, 1), 0, 0)),
                 pl.BlockSpec((1, blk, 2 * blk), lambda b, r, n: (1, 0, 0)),
                 pl.BlockSpec((N_KV_HEADS, blk, KV_WIDTH), lambda b, r, n: (0, 0, 0))]
    args += [q2, k2, k2, v2, v2, bias, bias, _head_mask(blk, BF16)]
    out_shape = [jax.ShapeDtypeStruct((tp // dil, dil * Q_WIDTH), out_dtype)]
    out_specs = [pl.BlockSpec((step, Q_WIDTH), cur)]
    if want_lse:
        out_shape.append(jax.ShapeDtypeStruct((tp // dil, dil * LANES), F32))
        out_specs.append(pl.BlockSpec((step, LANES), cur))
    res = pl.pallas_call(
        functools.partial(_band_body, sinks is not None, want_lse),
        grid=(n_seq, dil, n_steps), in_specs=in_specs, out_specs=out_specs, out_shape=out_shape,
        compiler_params=_params(3), name=f"band_d{dil}")(*args)
    return res[0], (res[1] if want_lse else None)


def _sample_body(has_sink, want_lse, n_new, seqs, *refs):
    it = iter(refs)
    sink_ref = next(it) if has_sink else None
    q_ref, kn_ref, vn_ref, cache_ref, bp_ref, bn_ref, hm_ref = (next(it) for _ in range(7))
    o_ref = next(it)
    lse_ref = next(it) if want_lse else None
    n_past = cache_ref.shape[-1]
    rows = N_Q_HEADS * n_new
    lane_head = lax.broadcasted_iota(jnp.int32, (n_new, KV_WIDTH), 1) // HEAD_DIM
    lane = lax.broadcasted_iota(jnp.int32, (n_new, LANES), 1)
    zpad = jnp.zeros((LANES - n_new, KV_WIDTH), F32)
    row_head = lax.broadcasted_iota(jnp.int32, (rows, 1), 0) // n_new

    for sb in range(seqs):
        rs = slice(n_new * sb, n_new * (sb + 1))
        q = q_ref[rs, :]
        lhs = jnp.concatenate(
            [q[:, KV_WIDTH * g4:KV_WIDTH * (g4 + 1)] * hm_ref[h]
             for g4 in range(GQA_GROUP) for h in range(N_KV_HEADS)], axis=0).astype(BF16)
        kt = cache_ref[sb, 0].reshape(KV_WIDTH, n_past).astype(BF16)
        vt = cache_ref[sb, 1].reshape(KV_WIDTH, n_past).astype(BF16)
        kn = jnp.concatenate([kn_ref[rs, :], zpad], axis=0).astype(BF16)
        vn = jnp.concatenate([vn_ref[rs, :], zpad], axis=0).astype(BF16)
        s1 = _dot(lhs, kt) + bp_ref[...]
        s2 = _dot_nt(lhs, kn) + bn_ref[...]
        m = jnp.maximum(jnp.max(s1, axis=1, keepdims=True), jnp.max(s2, axis=1, keepdims=True))
        if has_sink:
            sink_col = jnp.zeros((rows, 1), F32)
            for g4 in range(GQA_GROUP):
                for h in range(N_KV_HEADS):
                    sink_col = jnp.where(row_head == g4 * N_KV_HEADS + h, sink_ref[h * GQA_GROUP + g4], sink_col)
            m = jnp.maximum(m, sink_col)
        p1 = jnp.exp(s1 - m)
        p2 = jnp.exp(s2 - m)
        l = jnp.sum(p1, axis=1, keepdims=True) + jnp.sum(p2, axis=1, keepdims=True)
        if has_sink:
            l = l + jnp.exp(sink_col - m)
        pv = _dot_nt(p1.astype(BF16), vt) + _dot(p2.astype(BF16), vn)
        on = pv * (1.0 / l)
        lse = m + jnp.log(l)
        lse_tile = jnp.zeros((n_new, LANES), F32)
        for g4 in range(GQA_GROUP):
            og = jnp.zeros((n_new, KV_WIDTH), F32)
            for h in range(N_KV_HEADS):
                r0 = (g4 * N_KV_HEADS + h) * n_new
                og = jnp.where(lane_head == h, on[r0:r0 + n_new], og)
                if want_lse:
                    lse_tile = jnp.where(lane == g4 * N_KV_HEADS + h, lse[r0:r0 + n_new], lse_tile)
            o_ref[rs, KV_WIDTH * g4:KV_WIDTH * (g4 + 1)] = og.astype(o_ref.dtype)
        if want_lse:
            lse_ref[rs, :] = lse_tile


def _sample_bias(n_new, n_past, n_keys, dil):
    j = (np.arange(N_Q_HEADS * n_new) % n_new)[:, None]
    span = dil * (n_keys - 1)

    def ok(delta):
        return (delta >= 0) & (delta % dil == 0) & (delta <= span)

    past = ok(n_past + j - np.arange(n_past)[None, :])
    i_new = np.arange(LANES)[None, :]
    new = ok(j - i_new) & (i_new < n_new)
    to_bias = lambda m: jnp.asarray(np.where(m, 0.0, NEG_INF).astype(np.float32))
    return to_bias(past), to_bias(new)


def _sample_attention(q, k_new, v_new, cache, sinks, n_new, n_keys, dil, out_dtype, want_lse):
    n_seq, n_past = cache.shape[0], cache.shape[-1]
    seqs = 4 if n_past <= 512 else 2
    rows = seqs * n_new
    bias_past, bias_new = _sample_bias(n_new, n_past, n_keys, dil)
    n_rows = N_Q_HEADS * n_new
    row = lambda i: (i, 0)
    const = lambda i: (0, 0)
    in_specs, args = [], []
    if sinks is not None:
        in_specs.append(pl.BlockSpec(memory_space=pltpu.SMEM))
        args.append(sinks)
    in_specs += [pl.BlockSpec((rows, Q_WIDTH), row), pl.BlockSpec((rows, KV_WIDTH), row),
                 pl.BlockSpec((rows, KV_WIDTH), row),
                 pl.BlockSpec((seqs, 2, N_KV_HEADS, HEAD_DIM, n_past), lambda i: (i, 0, 0, 0, 0)),
                 pl.BlockSpec((n_rows, n_past), const), pl.BlockSpec((n_rows, LANES), const),
                 pl.BlockSpec((N_KV_HEADS, n_new, KV_WIDTH), lambda i: (0, 0, 0))]
    args += [q, k_new, v_new, cache, bias_past, bias_new, _head_mask(n_new, F32)]
    out_shape = [jax.ShapeDtypeStruct((n_seq * n_new, Q_WIDTH), out_dtype)]
    out_specs = [pl.BlockSpec((rows, Q_WIDTH), row)]
    if want_lse:
        out_shape.append(jax.ShapeDtypeStruct((n_seq * n_new, LANES), F32))
        out_specs.append(pl.BlockSpec((rows, LANES), row))
    res = pl.pallas_call(
        functools.partial(_sample_body, sinks is not None, want_lse, n_new, seqs),
        grid=(n_seq // seqs,), in_specs=in_specs, out_specs=out_specs, out_shape=out_shape,
        compiler_params=_params(1), name=f"sample_d{dil}")(*args)
    return res[0], (res[1] if want_lse else None)


def _route_tile(logits, tri_ref, counts_ref):
    tm = logits.shape[0]
    lane = lax.broadcasted_iota(jnp.int32, (tm, LANES), 1)
    lane_f = lane.astype(F32)
    none = float(LANES)
    first_lane = lambda hit: jnp.min(jnp.where(hit, lane_f, none), axis=1, keepdims=True)

    gl = jnp.where(lane < N_EXPERT_GROUPS, logits, NEG_INF)
    gmax = jnp.max(gl, axis=1, keepdims=True)
    g_idx = first_lane(gl == gmax)
    g_w = 1.0 / jnp.sum(jnp.exp(gl - gmax), axis=1, keepdims=True)

    e_lane = lane - N_EXPERT_GROUPS
    lane_group = jnp.where((e_lane >= 0) & (e_lane < N_EXPERTS), (e_lane // EXPERTS_PER_GROUP).astype(F32), none)
    el = jnp.where(lane_group == g_idx, logits, NEG_INF)
    v1 = jnp.max(el, axis=1, keepdims=True)
    i1 = first_lane(el == v1)
    el2 = jnp.where(lane_f == i1, NEG_INF, el)
    v2 = jnp.max(el2, axis=1, keepdims=True)
    i2 = first_lane(el2 == v2)
    t = jnp.exp(v2 - v1)
    den = 1.0 + t
    w1 = g_w * (1.0 / den)
    w2 = g_w * (t / den)
    e1 = i1 - float(N_EXPERT_GROUPS)
    e2 = i2 - float(N_EXPERT_GROUPS)

    hot1 = lane_f == e1
    hot2 = lane_f == e2
    both = jnp.where(hot1, 1.0, 0.0) + jnp.where(hot2, 1.0, 0.0)
    before = _dot(tri_ref[...], both.astype(BF16)) + counts_ref[0:1, :]
    r1 = jnp.sum(jnp.where(hot1, before, 0.0), axis=1, keepdims=True)
    r2 = jnp.sum(jnp.where(hot2, before, 0.0), axis=1, keepdims=True)
    counts_ref[...] = counts_ref[...] + jnp.sum(both, axis=0, keepdims=True)

    info = jnp.zeros((tm, LANES), F32)
    for j, col in enumerate((e1, e2, r1, r2, w1, w2)):
        info = jnp.where(lane == j, col, info)
    return info


def _outproj_body(dils, n_ptiles, *refs):
    n_groups = len(dils)
    it = iter(refs)
    o_refs = [(next(it), next(it)) for _ in range(n_groups)]
    lse_refs = [(next(it), next(it)) for _ in range(n_groups)] if n_groups > 1 else None
    expand_ref = next(it) if n_groups > 1 else None
    h_ref, wo_ref, g_ref, wr_hi_ref, wr_lo_ref, br_ref, tri_ref = (next(it) for _ in range(7))
    hout_ref, xn_ref, info_ref, counts_ref = next(it), next(it), next(it), next(it)
    stages = {g: (next(it), next(it)) for g in range(n_groups) if dils[g] > 1}

    @pl.when(pl.program_id(0) == 0)
    def _():
        counts_ref[...] = jnp.zeros(counts_ref.shape, F32)

    is_p = pl.program_id(0) < n_ptiles

    def pick(pair, g, which):
        prompt_ref, sample_ref = pair
        dil, width = dils[g], sample_ref.shape[1]
        if dil == 1:
            prompt = prompt_ref[...]
        else:
            stage = stages[g][which]
            rows = prompt_ref.shape[0]
            for r in range(dil):
                for j in range(width // LANES):
                    c0 = r * width + LANES * j
                    stage[j, pl.ds(r, rows, stride=dil), :] = prompt_ref[:, c0:c0 + LANES]
            prompt = jnp.concatenate([stage[j] for j in range(width // LANES)], axis=1)
        return jnp.where(is_p, prompt, sample_ref[...])

    if n_groups == 1:
        o = pick(o_refs[0], 0, 0)
    else:
        lses = [pick(p, g, 1) for g, p in enumerate(lse_refs)]
        mx = functools.reduce(jnp.maximum, lses)
        es = [jnp.exp(x - mx) for x in lses]
        inv = 1.0 / functools.reduce(lambda a, b: a + b, es)
        acc, wide_sum = None, None
        for g in range(n_groups):
            if g < n_groups - 1:
                w = es[g] * inv
                w_hi = w.astype(BF16)
                w_lo = (w - w_hi.astype(F32)).astype(BF16)
                wide = _dot(w_hi, expand_ref[...]) + _dot(w_lo, expand_ref[...])
                wide_sum = wide if wide_sum is None else wide_sum + wide
            else:
                wide = 1.0 - wide_sum
            term = wide * pick(o_refs[g], g, 0)
            acc = term if acc is None else acc + term
        o = acc.astype(BF16)
    h_new = h_ref[...] + _dot(o, wo_ref[...])
    hout_ref[...] = h_new
    xn = _rms(h_new, g_ref[...])
    xn_ref[...] = xn
    x_hi = xn.astype(BF16)
    x_lo = (xn - x_hi.astype(F32)).astype(BF16)
    logits = (_dot(x_hi, wr_hi_ref[...]) + _dot(x_hi, wr_lo_ref[...])
              + _dot(x_lo, wr_hi_ref[...]) + br_ref[...])
    info_ref[...] = _route_tile(logits, tri_ref, counts_ref)


def _out_project(o_pairs, lse_pairs, dils, h, wo, gain, wr_hi, wr_lo, br, n_ptiles):
    tm = ROW_TILE
    n_tok = h.shape[0]
    n_groups = len(o_pairs)
    row = lambda i: (i, 0)
    const = lambda i: (0, 0)
    prow = lambda i: (jnp.minimum(i, n_ptiles - 1), 0)
    srow = lambda i: (jnp.maximum(i - n_ptiles, 0), 0)
    in_specs, args, scratch = [], [], []
    for (op, os_), d in zip(o_pairs, dils):
        in_specs += [pl.BlockSpec((tm // d, d * Q_WIDTH), prow), pl.BlockSpec((tm, Q_WIDTH), srow)]
        args += [op, os_]
        if d > 1:
            scratch += [pltpu.VMEM((Q_WIDTH // LANES, tm, LANES), F32), pltpu.VMEM((1, tm, LANES), F32)]
    if n_groups > 1:
        for (lp, ls), d in zip(lse_pairs, dils):
            in_specs += [pl.BlockSpec((tm // d, d * LANES), prow), pl.BlockSpec((tm, LANES), srow)]
            args += [lp, ls]
        expand = (np.arange(LANES)[:, None] == np.arange(Q_WIDTH)[None, :] // HEAD_DIM)
        in_specs.append(pl.BlockSpec((LANES, Q_WIDTH), const))
        args.append(jnp.asarray(expand.astype(np.float32), BF16))
    in_specs += [pl.BlockSpec((tm, D_MODEL), row), pl.BlockSpec((Q_WIDTH, D_MODEL), const),
                 pl.BlockSpec((1, D_MODEL), const), pl.BlockSpec((D_MODEL, LANES), const),
                 pl.BlockSpec((D_MODEL, LANES), const), pl.BlockSpec((1, LANES), const),
                 pl.BlockSpec((tm, tm), const)]
    strictly_lower = np.tril(np.ones((tm, tm), np.float32), -1)
    args += [h, wo, gain, wr_hi, wr_lo, br, jnp.asarray(strictly_lower, BF16)]
    out_shape = [jax.ShapeDtypeStruct((n_tok, D_MODEL), F32), jax.ShapeDtypeStruct((n_tok, D_MODEL), F32),
                 jax.ShapeDtypeStruct((n_tok, LANES), F32), jax.ShapeDtypeStruct((SUBLANES, LANES), F32)]
    out_specs = [pl.BlockSpec((tm, D_MODEL), row), pl.BlockSpec((tm, D_MODEL), row),
                 pl.BlockSpec((tm, LANES), row), pl.BlockSpec((SUBLANES, LANES), const)]
    return pl.pallas_call(
        functools.partial(_outproj_body, tuple(dils), n_ptiles),
        grid=(n_tok // tm,), in_specs=in_specs, out_specs=out_specs, out_shape=out_shape,
        scratch_shapes=scratch, compiler_params=_params(1), name=f"outproj_{n_groups}")(*args)


GATHER_AHEAD = 2
GATHER_BUFFERS = GATHER_AHEAD + 1


def _moe_body(layer, block_e_ref, next_e_ref, parity_ref, n_used_ref, first_ref, token_ref,
              xn_hbm, wg_hbm, wu_hbm, wd_hbm, y_ref,
              xbuf, wg_f32, wu_f32, wd_f32, wg_bf, wu_bf, wd_bf, sem_x, sem):
    i = pl.program_id(0)
    n_used = n_used_ref[0]
    last_used = jnp.maximum(n_used - 1, 0)

    def fetch(e, sl):
        return [pltpu.make_async_copy(src.at[layer, e], dst.at[sl], sem.at[sl])
                for src, dst in ((wg_hbm, wg_f32), (wu_hbm, wu_f32), (wd_hbm, wd_f32))]

    def gather(block, sl):
        first = first_ref[block]
        for r in range(EXPERT_BLOCK):
            tok = token_ref[first + r]
            pltpu.make_async_copy(xn_hbm.at[pl.ds(tok, 1)], xbuf.at[sl, pl.ds(r, 1)], sem_x.at[sl]).start()

    def wait_rows(sl):
        pltpu.make_async_copy(xn_hbm.at[pl.ds(0, EXPERT_BLOCK)], xbuf.at[sl], sem_x.at[sl]).wait()

    @pl.when((i == 0) & (n_used > 0))
    def _():
        for copy in fetch(block_e_ref[0], parity_ref[0]):
            copy.start()
        for b in range(GATHER_AHEAD):
            gather(jnp.minimum(b, last_used), b)

    @pl.when(i < n_used)
    def _():
        slot = i % GATHER_BUFFERS
        wait_rows(slot)
        changed = (i == 0) | (block_e_ref[i] != block_e_ref[jnp.maximum(i - 1, 0)])

        @pl.when(changed)
        def _():
            sl = parity_ref[i]
            for copy in fetch(block_e_ref[i], sl):
                copy.wait()

            @pl.when(next_e_ref[i] >= 0)
            def _():
                for copy in fetch(next_e_ref[i], 1 - sl):
                    copy.start()

            wg_bf[...] = wg_f32[sl].astype(BF16)
            wu_bf[...] = wu_f32[sl].astype(BF16)
            wd_bf[...] = wd_f32[sl].astype(BF16)

        x = xbuf[slot].astype(BF16)
        gather(jnp.minimum(i + GATHER_AHEAD, last_used), (i + GATHER_AHEAD) % GATHER_BUFFERS)
        gate = _dot(x, wg_bf[...])
        up = _dot(x, wu_bf[...])
        mid = (gate * jax.nn.sigmoid(gate) * up).astype(BF16)
        y_ref[...] = _dot(mid, wd_bf[...])

        @pl.when(i == last_used)
        def _():
            for b in range(1, GATHER_BUFFERS):
                wait_rows((i + b) % GATHER_BUFFERS)

    @pl.when(i >= n_used)
    def _():
        y_ref[...] = jnp.zeros(y_ref.shape, F32)


def _moe_experts(xn, tokens, first, block_e, next_e, parity, n_used, n_slots, w_gate, w_up, w_down, layer):
    grid_spec = pltpu.PrefetchScalarGridSpec(
        num_scalar_prefetch=6, grid=(n_slots // EXPERT_BLOCK,),
        in_specs=[pl.BlockSpec(memory_space=pl.ANY), pl.BlockSpec(memory_space=pl.ANY),
                  pl.BlockSpec(memory_space=pl.ANY), pl.BlockSpec(memory_space=pl.ANY)],
        out_specs=pl.BlockSpec((EXPERT_BLOCK, D_MODEL), lambda i, *_: (i, 0)),
        scratch_shapes=[pltpu.VMEM((GATHER_BUFFERS, EXPERT_BLOCK, D_MODEL), F32),
                        pltpu.VMEM((2, D_MODEL, D_FF_EXPERT), F32),
                        pltpu.VMEM((2, D_MODEL, D_FF_EXPERT), F32),
                        pltpu.VMEM((2, D_FF_EXPERT, D_MODEL), F32),
                        pltpu.VMEM((D_MODEL, D_FF_EXPERT), BF16),
                        pltpu.VMEM((D_MODEL, D_FF_EXPERT), BF16),
                        pltpu.VMEM((D_FF_EXPERT, D_MODEL), BF16),
                        pltpu.SemaphoreType.DMA((GATHER_BUFFERS,)), pltpu.SemaphoreType.DMA((2,))])
    return pl.pallas_call(
        functools.partial(_moe_body, layer), grid_spec=grid_spec,
        out_shape=jax.ShapeDtypeStruct((n_slots, D_MODEL), F32),
        compiler_params=_params(1), name=f"moe_{layer}")(
            block_e, next_e, parity, n_used, first, tokens, xn, w_gate, w_up, w_down)


def _slot_layout(info, counts_tile, n_slots):
    counts = counts_tile[0, :N_EXPERTS].astype(jnp.int32)
    expert = info[:, 0:TOP_K_INNER].astype(jnp.int32)
    rank = info[:, TOP_K_INNER:2 * TOP_K_INNER].astype(jnp.int32)
    padded = (counts + EXPERT_BLOCK - 1) // EXPERT_BLOCK * EXPERT_BLOCK
    pad_end = jnp.cumsum(padded)
    pad_start = pad_end - padded
    hot = expert[..., None] == jnp.arange(N_EXPERTS, dtype=jnp.int32)
    dest = (jnp.sum(jnp.where(hot, pad_start, 0), axis=-1) + rank).reshape(-1)
    block_start = jnp.arange(n_slots // EXPERT_BLOCK, dtype=jnp.int32) * EXPERT_BLOCK
    block_e = jnp.minimum(jnp.sum(pad_end[None, :] <= block_start[:, None], axis=1), N_EXPERTS - 1)
    n_used = pad_end[-1] // EXPERT_BLOCK
    own_end = jnp.sum(jnp.where(block_e[:, None] == jnp.arange(N_EXPERTS), pad_end, 0), axis=1)
    after = own_end // EXPERT_BLOCK
    next_e = jnp.where(after < n_used, jnp.sum(pad_end[None, :] <= (after * EXPERT_BLOCK)[:, None], axis=1), -1)
    changed = jnp.concatenate([jnp.zeros((1,), jnp.int32), (block_e[1:] != block_e[:-1]).astype(jnp.int32)])
    parity = jnp.cumsum(changed) % 2
    order = jnp.argsort(expert.reshape(-1), stable=True)
    tokens = jnp.concatenate([order // TOP_K_INNER, jnp.zeros((EXPERT_BLOCK,), order.dtype)])
    seg_start = jnp.cumsum(counts) - counts
    own = lambda table: jnp.sum(jnp.where(block_e[:, None] == jnp.arange(N_EXPERTS), table, 0), axis=1)
    first = own(seg_start) + block_start - own(pad_start)
    i32 = lambda a: a.astype(jnp.int32)
    return (i32(dest), i32(tokens), i32(first), i32(block_e), i32(next_e), i32(parity), i32(n_used).reshape(1))


def _start_row_gather(dest_ref, ys_hbm, ybuf, sem, tile, sl):
    for t in range(ROW_TILE):
        for k in range(TOP_K_INNER):
            d = dest_ref[TOP_K_INNER * (tile * ROW_TILE + t) + k]
            pltpu.make_async_copy(ys_hbm.at[pl.ds(d, 1)], ybuf.at[sl, k, pl.ds(t, 1)], sem.at[sl]).start()


def _moe_output_tile(dest_ref, info_ref, ys_hbm, ybuf, sem, tile0):
    i = pl.program_id(0)
    last = pl.num_programs(0) - 1
    sl = i % GATHER_BUFFERS

    def wait(slot):
        for k in range(TOP_K_INNER):
            pltpu.make_async_copy(ys_hbm.at[pl.ds(0, ROW_TILE)], ybuf.at[slot, k], sem.at[slot]).wait()

    @pl.when(i == 0)
    def _():
        for b in range(GATHER_AHEAD):
            _start_row_gather(dest_ref, ys_hbm, ybuf, sem, tile0 + jnp.minimum(b, last), b)

    wait(sl)
    info = info_ref[...]
    w_lane = 2 * TOP_K_INNER
    y = ybuf[sl, 0] * info[:, w_lane:w_lane + 1] + ybuf[sl, 1] * info[:, w_lane + 1:w_lane + 2]
    _start_row_gather(dest_ref, ys_hbm, ybuf, sem, tile0 + jnp.minimum(i + GATHER_AHEAD, last),
                      (i + GATHER_AHEAD) % GATHER_BUFFERS)

    def drain():
        @pl.when(i == last)
        def _():
            for b in range(1, GATHER_BUFFERS):
                wait((i + b) % GATHER_BUFFERS)
    return y, drain


_COMBINE_SCRATCH = [pltpu.VMEM((GATHER_BUFFERS, TOP_K_INNER, ROW_TILE, D_MODEL), F32),
                    pltpu.SemaphoreType.DMA((GATHER_BUFFERS,))]


def _final_body(tile0, dest_ref, h_ref, info_ref, ys_hbm, g_ref, o_ref, ybuf, sem):
    y, drain = _moe_output_tile(dest_ref, info_ref, ys_hbm, ybuf, sem, tile0)
    o_ref[...] = _rms(h_ref[...] + y, g_ref[...])
    drain()


def _final_norm(h, info, ys, dest, gain, tile0, n_tiles):
    tm = ROW_TILE
    grid_spec = pltpu.PrefetchScalarGridSpec(
        num_scalar_prefetch=1, grid=(n_tiles,),
        in_specs=[pl.BlockSpec((tm, D_MODEL), lambda i, d: (i + tile0, 0)),
                  pl.BlockSpec((tm, LANES), lambda i, d: (i + tile0, 0)),
                  pl.BlockSpec(memory_space=pl.ANY),
                  pl.BlockSpec((1, D_MODEL), lambda i, d: (0, 0))],
        out_specs=pl.BlockSpec((tm, D_MODEL), lambda i, d: (i, 0)),
        scratch_shapes=_COMBINE_SCRATCH)
    return pl.pallas_call(
        functools.partial(_final_body, tile0), grid_spec=grid_spec,
        out_shape=jax.ShapeDtypeStruct((n_tiles * tm, D_MODEL), F32),
        compiler_params=_params(1), name="final_norm")(dest, h, info, ys, gain)


def _heads_g4_major(w, axis):
    shape = w.shape
    split = shape[:axis] + (N_KV_HEADS, GQA_GROUP, HEAD_DIM) + shape[axis + 1:]
    return jnp.swapaxes(w.reshape(split), axis, axis + 1).reshape(shape)


def _prep_qkv_weight(w, n_groups):
    cols = []
    for g in range(n_groups):
        base = g * QKV_WIDTH
        cols.append(_heads_g4_major(w[:, base:base + Q_WIDTH], 1) * ATTN_SCALE)
        cols.append(w[:, base + Q_WIDTH:base + QKV_WIDTH])
    return jnp.concatenate(cols, axis=1).astype(BF16)


def _rope_tables(seq_len, past_len, n_new):
    inv_freq = ROPE_THETA ** (-jnp.arange(0, ROT_DIM, 2, dtype=F32) / ROT_DIM)
    pos_p = jnp.arange(seq_len, dtype=jnp.int32)
    pos_s = past_len + (jnp.arange(ROW_TILE, dtype=jnp.int32) % n_new)
    ang = jnp.concatenate([pos_p, pos_s]).astype(F32)[:, None] * inv_freq[None, :]
    cos, sin = jnp.cos(ang), jnp.sin(ang)
    in_head = jnp.arange(LANES, dtype=jnp.int32) % HEAD_DIM
    spread = lambda t: jnp.tile(t, (1, LANES // ROT_HALF))
    c = jnp.where(in_head < ROT_DIM, spread(cos), 1.0)
    s1 = jnp.where(in_head < ROT_HALF, -spread(sin), 0.0)
    s2 = jnp.where((in_head >= ROT_HALF) & (in_head < ROT_DIM), spread(sin), 0.0)
    return c, s1, s2


def _split_hi_lo(w):
    hi = w.astype(BF16)
    return hi, (w - hi.astype(F32)).astype(BF16)


def _state_rows(k, v, lead):
    kv = jnp.stack([k, v], axis=-2)
    return kv.reshape(lead + (k.shape[-2], 2, N_KV_HEADS, HEAD_DIM))


def kernel(x_prompt, x_sample, cache_a_kv, cache_b_kv0, cache_b_kv1, cache_b_kv2, norm_mix, norm_ffn, norm_final, a_w_qkv, a_sinks, a_w_o, b_w_qkv, b_w_o, moe_router_g_w, moe_router_g_b, moe_router_e_w, moe_router_e_b, moe_w_gate, moe_w_up, moe_w_down):
    n_seq, seq_len, _ = x_prompt.shape
    n_dec, n_new, _ = x_sample.shape
    tp, ts = n_seq * seq_len, n_dec * n_new
    n_tok = tp + ts
    n_ptiles = tp // ROW_TILE
    tiles_per_seq = seq_len // ROW_TILE
    n_slots = (-(-(n_tok * TOP_K_INNER) // EXPERT_BLOCK) + N_EXPERTS) * EXPERT_BLOCK
    caches_b = (cache_b_kv0, cache_b_kv1, cache_b_kv2)

    tables = _rope_tables(seq_len, PAST_LEN, n_new)
    xp = x_prompt.reshape(tp, D_MODEL)
    xs = x_sample.reshape(ts, D_MODEL)

    def keys_on_lanes(cache):
        return jnp.transpose(cache, (0, 2, 3, 4, 1))

    def router(i):
        w = jnp.concatenate([moe_router_g_w[i], moe_router_e_w[i],
                             jnp.zeros((D_MODEL, LANES - N_EXPERT_GROUPS - N_EXPERTS), F32)], axis=1)
        b = jnp.concatenate([moe_router_g_b[i], moe_router_e_b[i],
                             jnp.zeros((LANES - N_EXPERT_GROUPS - N_EXPERTS,), F32)]).reshape(1, LANES)
        return _split_hi_lo(w) + (b,)

    def moe(i, xn, info, counts_tile):
        dest, tokens, first, block_e, next_e, parity, n_used = _slot_layout(info, counts_tile, n_slots)
        ys = _moe_experts(xn, tokens, first, block_e, next_e, parity, n_used, n_slots,
                          moe_w_gate, moe_w_up, moe_w_down, i)
        return ys, dest

    def sample_rows(t, d, width):
        return t[tp // d:].reshape(ts, width)

    def window_state(k, v, n_keep, d):
        tail = lambda t: jnp.stack([t[((b + 1) * seq_len - n_keep) // d:(b + 1) * seq_len // d].reshape(n_keep, KV_WIDTH)
                                    for b in range(n_seq)])
        return _state_rows(tail(k), tail(v), (n_seq,))[None]

    def new_rows_state(k, v, d):
        rows = lambda t: sample_rows(t, d, KV_WIDTH).reshape(n_dec, n_new, KV_WIDTH)
        return _state_rows(rows(k), rows(v), (n_dec,))[None]

    def decode(q, k, v, d, cache, sinks, n_keys, out_dtype, want_lse):
        return _sample_attention(sample_rows(q, d, Q_WIDTH).astype(F32), sample_rows(k, d, KV_WIDTH),
                                 sample_rows(v, d, KV_WIDTH), keys_on_lanes(cache), sinks,
                                 n_new, n_keys, d, out_dtype, want_lse)

    w_a = _prep_qkv_weight(a_w_qkv[0], 1)
    h0, [(q, k, v)] = _project("split", (xp, xs), norm_mix[0:1], w_a, tables, (1,), n_tok, n_ptiles, tiles_per_seq)
    sinks = a_sinks[0].astype(F32)
    o_p, _ = _band_attention(q, k, v, sinks, n_seq, seq_len, 1, WINDOW_A - 1, BF16, False)
    o_s, _ = decode(q, k, v, 1, cache_a_kv[0], sinks, WINDOW_A, BF16, False)
    new_a_p = window_state(k, v, min(WINDOW_A, seq_len), 1)
    new_a_s = new_rows_state(k, v, 1)
    wr_hi, wr_lo, br = router(0)
    h1, xn, info, counts_tile = _out_project([(o_p, o_s)], None, (1,), h0, _heads_g4_major(a_w_o[0], 0).astype(BF16),
                                             norm_ffn[0:1], wr_hi, wr_lo, br, n_ptiles)
    ys, dest = moe(0, xn, info, counts_tile)

    w_b = _prep_qkv_weight(b_w_qkv[0], len(B_PATTERNS))
    dils = tuple(d for _, d in B_PATTERNS)
    h2, groups = _project("combine", (h1, info, ys, dest), norm_mix[1:2], w_b, tables, dils,
                          n_tok, n_ptiles, tiles_per_seq)
    o_pairs, lse_pairs, new_b = [], [], []
    for g, (window, dil) in enumerate(B_PATTERNS):
        q, k, v = groups[g]
        o_p, lse_p = _band_attention(q, k, v, None, n_seq, seq_len, dil, window // dil, F32, True)
        o_s, lse_s = decode(q, k, v, dil, caches_b[g][0], None, window // dil + 1, F32, True)
        o_pairs.append((o_p, o_s))
        lse_pairs.append((lse_p, lse_s))
        new_b.append(window_state(k, v, min(window, seq_len), dil))
        new_b.append(new_rows_state(k, v, dil))
    wr_hi, wr_lo, br = router(1)
    h3, xn, info, counts_tile = _out_project(o_pairs, lse_pairs, dils, h2, _heads_g4_major(b_w_o[0], 0).astype(BF16),
                                             norm_ffn[1:2], wr_hi, wr_lo, br, n_ptiles)
    ys, dest = moe(1, xn, info, counts_tile)

    gain = norm_final.reshape(1, D_MODEL)
    y_p = _final_norm(h3, info, ys, dest, gain, 0, n_ptiles).reshape(n_seq, seq_len, D_MODEL)
    y_s = _final_norm(h3, info, ys, dest, gain, n_ptiles, ts // ROW_TILE).reshape(n_dec, n_new, D_MODEL)
    return (y_p, y_s, new_a_p, new_a_s, *new_b)
```

```python
import functools

import numpy as np
import jax
import jax.numpy as jnp
from jax import lax
from jax.experimental import pallas as pl
from jax.experimental.pallas import tpu as pltpu

D_MODEL = 1024
HEAD_DIM = 64
N_KV_HEADS = 4
GQA_GROUP = 4
N_Q_HEADS = N_KV_HEADS * GQA_GROUP
Q_WIDTH = N_Q_HEADS * HEAD_DIM
KV_WIDTH = N_KV_HEADS * HEAD_DIM
QKV_WIDTH = Q_WIDTH + 2 * KV_WIDTH
ROT_DIM = HEAD_DIM // 4
ROT_HALF = ROT_DIM // 2
ROPE_THETA = 500000.0
ATTN_SCALE = HEAD_DIM ** -0.5
WINDOW_A = 128
B_PATTERNS = ((128, 1), (512, 4), (2048, 16))
BAND_BLOCK = 128
N_EXPERT_GROUPS = 8
EXPERTS_PER_GROUP = 8
N_EXPERTS = N_EXPERT_GROUPS * EXPERTS_PER_GROUP
TOP_K_INNER = 2
D_FF_EXPERT = 512
RMS_EPS = 1e-5
PAST_LEN = 16384

LANES = 128
SUBLANES = 8
ROW_TILE = 256
EXPERT_BLOCK = 256
VMEM_LIMIT_BYTES = 56 * 1024 * 1024

BF16 = jnp.bfloat16
F32 = jnp.float32
NEG_INF = float("-inf")


def _params(n_axes, vmem=VMEM_LIMIT_BYTES):
    return pltpu.CompilerParams(dimension_semantics=("arbitrary",) * n_axes,
                                vmem_limit_bytes=vmem)


def _rms(x, g):
    var = jnp.mean(x * x, axis=-1, keepdims=True)
    return x * lax.rsqrt(var + RMS_EPS) * g


def _dot(a, b):
    return jnp.dot(a, b, preferred_element_type=F32)


def _dot_nt(a, b):
    return lax.dot_general(a, b, (((1,), (1,)), ((), ())), preferred_element_type=F32)


def _proj_body(dils, mode, n_ptiles, *refs):
    n_groups = len(dils)
    it = iter(refs)
    if mode == "split":
        xp_ref, xs_ref = next(it), next(it)
    else:
        dest_ref, h_ref, info_ref, ys_hbm = next(it), next(it), next(it), next(it)
    g_ref, w_ref, cos_ref, s1_ref, s2_ref = (next(it) for _ in range(5))
    hout_ref = next(it)
    outs = [(next(it), next(it), next(it)) for _ in range(n_groups)]
    stage = next(it) if max(dils) > 1 else None

    if mode == "split":
        is_p = pl.program_id(0) < n_ptiles
        x = jnp.where(is_p, xp_ref[...], xs_ref[...])
    else:
        ybuf, sem = next(it), next(it)
        y, drain = _moe_output_tile(dest_ref, info_ref, ys_hbm, ybuf, sem, 0)
        x = h_ref[...] + y
    hout_ref[...] = x
    xn = _rms(x, g_ref[...]).astype(BF16)
    c, s1, s2 = cos_ref[...], s1_ref[...], s2_ref[...]

    def rope(y):
        parts = []
        for j in range(y.shape[1] // LANES):
            yb = y[:, LANES * j:LANES * (j + 1)]
            parts.append(yb * c + pltpu.roll(yb, LANES - ROT_HALF, 1) * s1
                         + pltpu.roll(yb, ROT_HALF, 1) * s2)
        return jnp.concatenate(parts, axis=1)

    def emit(out_ref, y, dil, width, col):
        n = y.shape[1]
        if dil == 1:
            out_ref[:, col:col + n] = y.astype(out_ref.dtype)
            return
        rows = y.shape[0] // dil
        for j in range(n // LANES):
            stage[j] = y[:, LANES * j:LANES * (j + 1)]
        for r in range(dil):
            for j in range(n // LANES):
                c0 = r * width + col + LANES * j
                out_ref[:, c0:c0 + LANES] = stage[j, pl.ds(r, rows, stride=dil), :].astype(out_ref.dtype)

    half = Q_WIDTH // 2
    for g in range(n_groups):
        base = g * QKV_WIDTH
        q_ref, k_ref, v_ref = outs[g]
        for hf in range(2):
            y = _dot(xn, w_ref[:, base + half * hf:base + half * (hf + 1)])
            emit(q_ref, rope(y), dils[g], Q_WIDTH, half * hf)
        y = _dot(xn, w_ref[:, base + Q_WIDTH:base + Q_WIDTH + KV_WIDTH])
        emit(k_ref, rope(y), dils[g], KV_WIDTH, 0)
        y = _dot(xn, w_ref[:, base + Q_WIDTH + KV_WIDTH:base + QKV_WIDTH])
        emit(v_ref, y, dils[g], KV_WIDTH, 0)
    if mode == "combine":
        drain()


def _project(mode, xs, gain, w, tables, dils, n_tok, n_ptiles, tiles_per_seq):
    tm = ROW_TILE
    n_tiles = n_tok // tm
    row = lambda i, *_: (i, 0)
    const = lambda i, *_: (0, 0)
    tab = lambda i, *_: (jnp.where(i < n_ptiles, i % tiles_per_seq, tiles_per_seq), 0)
    if mode == "split":
        xp, xsamp = xs
        in_specs = [pl.BlockSpec((tm, D_MODEL), lambda i: (jnp.minimum(i, n_ptiles - 1), 0)),
                    pl.BlockSpec((tm, D_MODEL), lambda i: (jnp.maximum(i - n_ptiles, 0), 0))]
        args = [xp, xsamp]
        prefetch, scratch = [], []
    else:
        h, info, ys, dest = xs
        in_specs = [pl.BlockSpec((tm, D_MODEL), row), pl.BlockSpec((tm, LANES), row),
                    pl.BlockSpec(memory_space=pl.ANY)]
        args = [h, info, ys]
        prefetch, scratch = [dest], _COMBINE_SCRATCH
    if max(dils) > 1:
        scratch = [pltpu.VMEM((Q_WIDTH // 2 // LANES, tm, LANES), F32)] + list(scratch)
    in_specs += [pl.BlockSpec((1, D_MODEL), const),
                 pl.BlockSpec(w.shape, const),
                 pl.BlockSpec((tm, LANES), tab), pl.BlockSpec((tm, LANES), tab),
                 pl.BlockSpec((tm, LANES), tab)]
    args += [gain, w, *tables]
    out_shape = [jax.ShapeDtypeStruct((n_tok, D_MODEL), F32)]
    out_specs = [pl.BlockSpec((tm, D_MODEL), row)]
    for d in dils:
        for width, dtype in ((Q_WIDTH, BF16), (KV_WIDTH, F32), (KV_WIDTH, F32)):
            out_shape.append(jax.ShapeDtypeStruct((n_tok // d, d * width), dtype))
            out_specs.append(pl.BlockSpec((tm // d, d * width), row))
    grid_spec = pltpu.PrefetchScalarGridSpec(
        num_scalar_prefetch=len(prefetch), grid=(n_tiles,), in_specs=in_specs, out_specs=out_specs,
        scratch_shapes=scratch)
    res = pl.pallas_call(
        functools.partial(_proj_body, tuple(dils), mode, n_ptiles), grid_spec=grid_spec, out_shape=out_shape,
        compiler_params=_params(1), name=f"proj_{mode}")(*prefetch, *args)
    h_out = res[0]
    groups = [tuple(res[1 + 3 * g:4 + 3 * g]) for g in range(len(dils))]
    return h_out, groups


BAND_STEP_BLOCKS = 2


def _band_body(has_sink, want_lse, *refs):
    it = iter(refs)
    sink_ref = next(it) if has_sink else None
    q_ref, kc_ref, kp_ref, vc_ref, vp_ref, bias_first_ref, bias_ref, hm_ref = (next(it) for _ in range(8))
    o_ref = next(it)
    lse_ref = next(it) if want_lse else None
    blk = BAND_BLOCK
    for sub in range(BAND_STEP_BLOCKS):
        rows = slice(blk * sub, blk * (sub + 1))
        if sub == 0:
            k = jnp.concatenate([kp_ref[...], kc_ref[rows, :]], axis=0)
            v = jnp.concatenate([vp_ref[...], vc_ref[rows, :]], axis=0)
            bias = bias_first_ref[0]
        else:
            k = kc_ref[blk * (sub - 1):blk * (sub + 1), :]
            v = vc_ref[blk * (sub - 1):blk * (sub + 1), :]
            bias = bias_ref[0]
        _band_block(has_sink, want_lse, sink_ref, q_ref[rows, :], k.astype(BF16), v.astype(BF16), bias, hm_ref,
                    o_ref, lse_ref, rows)


def _band_block(has_sink, want_lse, sink_ref, q, k, v, bias, hm_ref, o_ref, lse_ref, rows):
    blk = BAND_BLOCK
    bias4 = jnp.concatenate([bias] * N_KV_HEADS, axis=0)
    lane_head = lax.broadcasted_iota(jnp.int32, (blk, KV_WIDTH), 1) // HEAD_DIM
    lane = lax.broadcasted_iota(jnp.int32, (blk, LANES), 1)
    lse_tile = jnp.zeros((blk, LANES), F32)
    row_head = lax.broadcasted_iota(jnp.int32, (N_KV_HEADS * blk, 1), 0) // blk

    for g4 in range(GQA_GROUP):
        qg = q[:, KV_WIDTH * g4:KV_WIDTH * (g4 + 1)]
        lhs = jnp.concatenate([qg * hm_ref[h] for h in range(N_KV_HEADS)], axis=0)
        s = _dot_nt(lhs, k) + bias4
        m = jnp.max(s, axis=1, keepdims=True)
        if has_sink:
            sink_col = jnp.zeros((N_KV_HEADS * blk, 1), F32)
            for h in range(N_KV_HEADS):
                sink_col = jnp.where(row_head == h, sink_ref[h * GQA_GROUP + g4], sink_col)
            m = jnp.maximum(m, sink_col)
        p = jnp.exp(s - m)
        pv = _dot(p.astype(BF16), v)
        l = jnp.sum(p, axis=1, keepdims=True)
        if has_sink:
            l = l + jnp.exp(sink_col - m)
        on = pv * (1.0 / l)
        og = jnp.zeros((blk, KV_WIDTH), F32)
        for h in range(N_KV_HEADS):
            og = jnp.where(lane_head == h, on[blk * h:blk * (h + 1)], og)
            if want_lse:
                lse_h = m[blk * h:blk * (h + 1)] + jnp.log(l[blk * h:blk * (h + 1)])
                lse_tile = jnp.where(lane == g4 * N_KV_HEADS + h, lse_h, lse_tile)
        o_ref[rows, KV_WIDTH * g4:KV_WIDTH * (g4 + 1)] = og.astype(o_ref.dtype)
    if want_lse:
        lse_ref[rows, :] = lse_tile


def _band_bias(max_dist):
    qi = np.arange(BAND_BLOCK)[:, None]
    mi = np.arange(2 * BAND_BLOCK)[None, :]
    dist = BAND_BLOCK + qi - mi
    ok = (dist >= 0) & (dist <= max_dist)
    first = ok & (mi >= BAND_BLOCK)
    return np.where(np.stack([first, ok]), 0.0, NEG_INF).astype(np.float32)


def _head_mask(rows, dtype):
    lane_head = np.arange(KV_WIDTH)[None, None, :] // HEAD_DIM
    hm = (lane_head == np.arange(N_KV_HEADS)[:, None, None]).astype(np.float32)
    return jnp.asarray(np.broadcast_to(hm, (N_KV_HEADS, rows, KV_WIDTH)), dtype)


def _band_attention(q, k, v, sinks, n_seq, seq_len, dil, max_dist, out_dtype, want_lse):
    blk = BAND_BLOCK
    step = BAND_STEP_BLOCKS * blk
    tp = n_seq * seq_len
    n_steps = seq_len // dil // step
    q2, k2, v2 = q, k, v
    cur = lambda b, r, n: (b * n_steps + n, r)
    prev = lambda b, r, n: (BAND_STEP_BLOCKS * (b * n_steps + n) - jnp.minimum(n, 1), r)
    in_specs, args = [], []
    if sinks is not None:
        in_specs.append(pl.BlockSpec(memory_space=pltpu.SMEM))
        args.append(sinks)
    bias = jnp.asarray(_band_bias(max_dist))
    in_specs += [pl.BlockSpec((step, Q_WIDTH), cur),
                 pl.BlockSpec((step, KV_WIDTH), cur), pl.BlockSpec((blk, KV_WIDTH), prev),
                 pl.BlockSpec((step, KV_WIDTH), cur), pl.BlockSpec((blk, KV_WIDTH), prev),
                 pl.BlockSpec((1, blk, 2 * blk), lambda b, r, n: (jnp.minimum(n, 1), 0, 0)),
                 pl.BlockSpec((1, blk, 2 * blk), lambda b, r, n: (1, 0, 0)),
                 pl.BlockSpec((N_KV_HEADS, blk, KV_WIDTH), lambda b, r, n: (0, 0, 0))]
    args += [q2, k2, k2, v2, v2, bias, bias, _head_mask(blk, BF16)]
    out_shape = [jax.ShapeDtypeStruct((tp // dil, dil * Q_WIDTH), out_dtype)]
    out_specs = [pl.BlockSpec((step, Q_WIDTH), cur)]
    if want_lse:
        out_shape.append(jax.ShapeDtypeStruct((tp // dil, dil * LANES), F32))
        out_specs.append(pl.BlockSpec((step, LANES), cur))
    res = pl.pallas_call(
        functools.partial(_band_body, sinks is not None, want_lse),
        grid=(n_seq, dil, n_steps), in_specs=in_specs, out_specs=out_specs, out_shape=out_shape,
        compiler_params=_params(3), name=f"band_d{dil}")(*args)
    return res[0], (res[1] if want_lse else None)


def _sample_body(has_sink, want_lse, n_new, seqs, *refs):
    it = iter(refs)
    sink_ref = next(it) if has_sink else None
    q_ref, kn_ref, vn_ref, cache_ref, bp_ref, bn_ref, hm_ref = (next(it) for _ in range(7))
    o_ref = next(it)
    lse_ref = next(it) if want_lse else None
    n_past = cache_ref.shape[-1]
    rows = N_Q_HEADS * n_new
    lane_head = lax.broadcasted_iota(jnp.int32, (n_new, KV_WIDTH), 1) // HEAD_DIM
    lane = lax.broadcasted_iota(jnp.int32, (n_new, LANES), 1)
    zpad = jnp.zeros((LANES - n_new, KV_WIDTH), F32)
    row_head = lax.broadcasted_iota(jnp.int32, (rows, 1), 0) // n_new

    for sb in range(seqs):
        rs = slice(n_new * sb, n_new * (sb + 1))
        q = q_ref[rs, :]
        lhs = jnp.concatenate(
            [q[:, KV_WIDTH * g4:KV_WIDTH * (g4 + 1)] * hm_ref[h]
             for g4 in range(GQA_GROUP) for h in range(N_KV_HEADS)], axis=0).astype(BF16)
        kt = cache_ref[sb, 0].reshape(KV_WIDTH, n_past).astype(BF16)
        vt = cache_ref[sb, 1].reshape(KV_WIDTH, n_past).astype(BF16)
        kn = jnp.concatenate([kn_ref[rs, :], zpad], axis=0).astype(BF16)
        vn = jnp.concatenate([vn_ref[rs, :], zpad], axis=0).astype(BF16)
        s1 = _dot(lhs, kt) + bp_ref[...]
        s2 = _dot_nt(lhs, kn) + bn_ref[...]
        m = jnp.maximum(jnp.max(s1, axis=1, keepdims=True), jnp.max(s2, axis=1, keepdims=True))
        if has_sink:
            sink_col = jnp.zeros((rows, 1), F32)
            for g4 in range(GQA_GROUP):
                for h in range(N_KV_HEADS):
                    sink_col = jnp.where(row_head == g4 * N_KV_HEADS + h, sink_ref[h * GQA_GROUP + g4], sink_col)
            m = jnp.maximum(m, sink_col)
        p1 = jnp.exp(s1 - m)
        p2 = jnp.exp(s2 - m)
        l = jnp.sum(p1, axis=1, keepdims=True) + jnp.sum(p2, axis=1, keepdims=True)
        if has_sink:
            l = l + jnp.exp(sink_col - m)
        pv = _dot_nt(p1.astype(BF16), vt) + _dot(p2.astype(BF16), vn)
        on = pv * (1.0 / l)
        lse = m + jnp.log(l)
        lse_tile = jnp.zeros((n_new, LANES), F32)
        for g4 in range(GQA_GROUP):
            og = jnp.zeros((n_new, KV_WIDTH), F32)
            for h in range(N_KV_HEADS):
                r0 = (g4 * N_KV_HEADS + h) * n_new
                og = jnp.where(lane_head == h, on[r0:r0 + n_new], og)
                if want_lse:
                    lse_tile = jnp.where(lane == g4 * N_KV_HEADS + h, lse[r0:r0 + n_new], lse_tile)
            o_ref[rs, KV_WIDTH * g4:KV_WIDTH * (g4 + 1)] = og.astype(o_ref.dtype)
        if want_lse:
            lse_ref[rs, :] = lse_tile


def _sample_bias(n_new, n_past, n_keys, dil):
    j = (np.arange(N_Q_HEADS * n_new) % n_new)[:, None]
    span = dil * (n_keys - 1)

    def ok(delta):
        return (delta >= 0) & (delta % dil == 0) & (delta <= span)

    past = ok(n_past + j - np.arange(n_past)[None, :])
    i_new = np.arange(LANES)[None, :]
    new = ok(j - i_new) & (i_new < n_new)
    to_bias = lambda m: jnp.asarray(np.where(m, 0.0, NEG_INF).astype(np.float32))
    return to_bias(past), to_bias(new)


def _sample_attention(q, k_new, v_new, cache, sinks, n_new, n_keys, dil, out_dtype, want_lse):
    n_seq, n_past = cache.shape[0], cache.shape[-1]
    seqs = 8 if n_past <= 128 else 4 if n_past <= 512 else 2
    rows = seqs * n_new
    bias_past, bias_new = _sample_bias(n_new, n_past, n_keys, dil)
    n_rows = N_Q_HEADS * n_new
    row = lambda i: (i, 0)
    const = lambda i: (0, 0)
    in_specs, args = [], []
    if sinks is not None:
        in_specs.append(pl.BlockSpec(memory_space=pltpu.SMEM))
        args.append(sinks)
    in_specs += [pl.BlockSpec((rows, Q_WIDTH), row), pl.BlockSpec((rows, KV_WIDTH), row),
                 pl.BlockSpec((rows, KV_WIDTH), row),
                 pl.BlockSpec((seqs, 2, N_KV_HEADS, HEAD_DIM, n_past), lambda i: (i, 0, 0, 0, 0)),
                 pl.BlockSpec((n_rows, n_past), const), pl.BlockSpec((n_rows, LANES), const),
                 pl.BlockSpec((N_KV_HEADS, n_new, KV_WIDTH), lambda i: (0, 0, 0))]
    args += [q, k_new, v_new, cache, bias_past, bias_new, _head_mask(n_new, F32)]
    out_shape = [jax.ShapeDtypeStruct((n_seq * n_new, Q_WIDTH), out_dtype)]
    out_specs = [pl.BlockSpec((rows, Q_WIDTH), row)]
    if want_lse:
        out_shape.append(jax.ShapeDtypeStruct((n_seq * n_new, LANES), F32))
        out_specs.append(pl.BlockSpec((rows, LANES), row))
    res = pl.pallas_call(
        functools.partial(_sample_body, sinks is not None, want_lse, n_new, seqs),
        grid=(n_seq // seqs,), in_specs=in_specs, out_specs=out_specs, out_shape=out_shape,
        compiler_params=_params(1), name=f"sample_d{dil}")(*args)
    return res[0], (res[1] if want_lse else None)


def _route_tile(logits, tri_ref, counts_ref):
    tm = logits.shape[0]
    lane = lax.broadcasted_iota(jnp.int32, (tm, LANES), 1)
    lane_f = lane.astype(F32)
    none = float(LANES)
    first_lane = lambda hit: jnp.min(jnp.where(hit, lane_f, none), axis=1, keepdims=True)

    gl = jnp.where(lane < N_EXPERT_GROUPS, logits, NEG_INF)
    gmax = jnp.max(gl, axis=1, keepdims=True)
    g_idx = first_lane(gl == gmax)
    g_w = 1.0 / jnp.sum(jnp.exp(gl - gmax), axis=1, keepdims=True)

    e_lane = lane - N_EXPERT_GROUPS
    lane_group = jnp.where((e_lane >= 0) & (e_lane < N_EXPERTS), (e_lane // EXPERTS_PER_GROUP).astype(F32), none)
    el = jnp.where(lane_group == g_idx, logits, NEG_INF)
    v1 = jnp.max(el, axis=1, keepdims=True)
    i1 = first_lane(el == v1)
    el2 = jnp.where(lane_f == i1, NEG_INF, el)
    v2 = jnp.max(el2, axis=1, keepdims=True)
    i2 = first_lane(el2 == v2)
    t = jnp.exp(v2 - v1)
    den = 1.0 + t
    w1 = g_w * (1.0 / den)
    w2 = g_w * (t / den)
    e1 = i1 - float(N_EXPERT_GROUPS)
    e2 = i2 - float(N_EXPERT_GROUPS)

    hot1 = lane_f == e1
    hot2 = lane_f == e2
    both = jnp.where(hot1, 1.0, 0.0) + jnp.where(hot2, 1.0, 0.0)
    before = _dot(tri_ref[...], both.astype(BF16)) + counts_ref[0:1, :]
    r1 = jnp.sum(jnp.where(hot1, before, 0.0), axis=1, keepdims=True)
    r2 = jnp.sum(jnp.where(hot2, before, 0.0), axis=1, keepdims=True)
    counts_ref[...] = counts_ref[...] + jnp.sum(both, axis=0, keepdims=True)

    info = jnp.zeros((tm, LANES), F32)
    for j, col in enumerate((e1, e2, r1, r2, w1, w2)):
        info = jnp.where(lane == j, col, info)
    return info


def _outproj_body(dils, n_ptiles, *refs):
    n_groups = len(dils)
    it = iter(refs)
    o_refs = [(next(it), next(it)) for _ in range(n_groups)]
    lse_refs = [(next(it), next(it)) for _ in range(n_groups)] if n_groups > 1 else None
    expand_ref = next(it) if n_groups > 1 else None
    h_ref, wo_ref, g_ref, wr_ref, br_ref, tri_ref = (next(it) for _ in range(6))
    hout_ref, xn_ref, info_ref, counts_ref = next(it), next(it), next(it), next(it)
    stages = {g: (next(it), next(it)) for g in range(n_groups) if dils[g] > 1}

    @pl.when(pl.program_id(0) == 0)
    def _():
        counts_ref[...] = jnp.zeros(counts_ref.shape, F32)

    is_p = pl.program_id(0) < n_ptiles

    def pick(pair, g, which):
        prompt_ref, sample_ref = pair
        dil, width = dils[g], sample_ref.shape[1]
        if dil == 1:
            prompt = prompt_ref[...]
        else:
            stage = stages[g][which]
            rows = prompt_ref.shape[0]
            for r in range(dil):
                for j in range(width // LANES):
                    c0 = r * width + LANES * j
                    stage[j, pl.ds(r, rows, stride=dil), :] = prompt_ref[:, c0:c0 + LANES]
            prompt = jnp.concatenate([stage[j] for j in range(width // LANES)], axis=1)
        return jnp.where(is_p, prompt, sample_ref[...])

    if n_groups == 1:
        o = pick(o_refs[0], 0, 0)
    else:
        lses = [pick(p, g, 1) for g, p in enumerate(lse_refs)]
        mx = functools.reduce(jnp.maximum, lses)
        es = [jnp.exp(x - mx) for x in lses]
        inv = 1.0 / functools.reduce(lambda a, b: a + b, es)
        acc, wide_sum = None, None
        for g in range(n_groups):
            if g < n_groups - 1:
                w = es[g] * inv
                w_hi = w.astype(BF16)
                w_lo = (w - w_hi.astype(F32)).astype(BF16)
                wide = _dot(w_hi, expand_ref[...]) + _dot(w_lo, expand_ref[...])
                wide_sum = wide if wide_sum is None else wide_sum + wide
            else:
                wide = 1.0 - wide_sum
            term = wide * pick(o_refs[g], g, 0)
            acc = term if acc is None else acc + term
        o = acc.astype(BF16)
    h_new = h_ref[...] + _dot(o, wo_ref[...])
    hout_ref[...] = h_new
    xn = _rms(h_new, g_ref[...])
    xn_ref[...] = xn
    x_hi = xn.astype(BF16)
    x_lo = (xn - x_hi.astype(F32)).astype(BF16)
    hi_terms = _dot(x_hi, wr_ref[...])
    logits = (hi_terms[:, :LANES] + hi_terms[:, LANES:]
              + _dot(x_lo, wr_ref[:, :LANES]) + br_ref[...])
    info_ref[...] = _route_tile(logits, tri_ref, counts_ref)


def _out_project(o_pairs, lse_pairs, dils, h, wo, gain, wr, br, n_ptiles):
    tm = ROW_TILE
    n_tok = h.shape[0]
    n_groups = len(o_pairs)
    row = lambda i: (i, 0)
    const = lambda i: (0, 0)
    prow = lambda i: (jnp.minimum(i, n_ptiles - 1), 0)
    srow = lambda i: (jnp.maximum(i - n_ptiles, 0), 0)
    in_specs, args, scratch = [], [], []
    for (op, os_), d in zip(o_pairs, dils):
        in_specs += [pl.BlockSpec((tm // d, d * Q_WIDTH), prow), pl.BlockSpec((tm, Q_WIDTH), srow)]
        args += [op, os_]
        if d > 1:
            scratch += [pltpu.VMEM((Q_WIDTH // LANES, tm, LANES), F32), pltpu.VMEM((1, tm, LANES), F32)]
    if n_groups > 1:
        for (lp, ls), d in zip(lse_pairs, dils):
            in_specs += [pl.BlockSpec((tm // d, d * LANES), prow), pl.BlockSpec((tm, LANES), srow)]
            args += [lp, ls]
        expand = (np.arange(LANES)[:, None] == np.arange(Q_WIDTH)[None, :] // HEAD_DIM)
        in_specs.append(pl.BlockSpec((LANES, Q_WIDTH), const))
        args.append(jnp.asarray(expand.astype(np.float32), BF16))
    in_specs += [pl.BlockSpec((tm, D_MODEL), row), pl.BlockSpec((Q_WIDTH, D_MODEL), const),
                 pl.BlockSpec((1, D_MODEL), const), pl.BlockSpec((D_MODEL, 2 * LANES), const),
                 pl.BlockSpec((1, LANES), const),
                 pl.BlockSpec((tm, tm), const)]
    strictly_lower = np.tril(np.ones((tm, tm), np.float32), -1)
    args += [h, wo, gain, wr, br, jnp.asarray(strictly_lower, BF16)]
    out_shape = [jax.ShapeDtypeStruct((n_tok, D_MODEL), F32), jax.ShapeDtypeStruct((n_tok, D_MODEL), F32),
                 jax.ShapeDtypeStruct((n_tok, LANES), F32), jax.ShapeDtypeStruct((SUBLANES, LANES), F32)]
    out_specs = [pl.BlockSpec((tm, D_MODEL), row), pl.BlockSpec((tm, D_MODEL), row),
                 pl.BlockSpec((tm, LANES), row), pl.BlockSpec((SUBLANES, LANES), const)]
    return pl.pallas_call(
        functools.partial(_outproj_body, tuple(dils), n_ptiles),
        grid=(n_tok // tm,), in_specs=in_specs, out_specs=out_specs, out_shape=out_shape,
        scratch_shapes=scratch, compiler_params=_params(1), name=f"outproj_{n_groups}")(*args)


GATHER_AHEAD = 2
GATHER_BUFFERS = GATHER_AHEAD + 1


def _moe_body(layer, block_e_ref, next_e_ref, parity_ref, n_used_ref, first_ref, token_ref,
              xn_hbm, wg_hbm, wu_hbm, wd_hbm, y_ref,
              xbuf, wg_f32, wu_f32, wd_f32, wg_bf, wu_bf, wd_bf, sem_x, sem):
    i = pl.program_id(0)
    n_used = n_used_ref[0]
    last_used = jnp.maximum(n_used - 1, 0)

    def fetch(e, sl):
        return [pltpu.make_async_copy(src.at[layer, e], dst.at[sl], sem.at[sl])
                for src, dst in ((wg_hbm, wg_f32), (wu_hbm, wu_f32), (wd_hbm, wd_f32))]

    def gather(block, sl):
        first = first_ref[block]
        for r in range(EXPERT_BLOCK):
            tok = token_ref[first + r]
            pltpu.make_async_copy(xn_hbm.at[pl.ds(tok, 1)], xbuf.at[sl, pl.ds(r, 1)], sem_x.at[sl]).start()

    def wait_rows(sl):
        pltpu.make_async_copy(xn_hbm.at[pl.ds(0, EXPERT_BLOCK)], xbuf.at[sl], sem_x.at[sl]).wait()

    @pl.when((i == 0) & (n_used > 0))
    def _():
        for copy in fetch(block_e_ref[0], parity_ref[0]):
            copy.start()
        for b in range(GATHER_AHEAD):
            gather(jnp.minimum(b, last_used), b)

    @pl.when(i < n_used)
    def _():
        slot = i % GATHER_BUFFERS
        wait_rows(slot)
        changed = (i == 0) | (block_e_ref[i] != block_e_ref[jnp.maximum(i - 1, 0)])

        @pl.when(changed)
        def _():
            sl = parity_ref[i]
            for copy in fetch(block_e_ref[i], sl):
                copy.wait()

            @pl.when(next_e_ref[i] >= 0)
            def _():
                for copy in fetch(next_e_ref[i], 1 - sl):
                    copy.start()

            wg_bf[...] = wg_f32[sl].astype(BF16)
            wu_bf[...] = wu_f32[sl].astype(BF16)
            wd_bf[...] = wd_f32[sl].astype(BF16)

        x = xbuf[slot].astype(BF16)
        gather(jnp.minimum(i + GATHER_AHEAD, last_used), (i + GATHER_AHEAD) % GATHER_BUFFERS)
        gate = _dot(x, wg_bf[...])
        up = _dot(x, wu_bf[...])
        mid = (gate * jax.nn.sigmoid(gate) * up).astype(BF16)
        y_ref[...] = _dot(mid, wd_bf[...])

        @pl.when(i == last_used)
        def _():
            for b in range(1, GATHER_BUFFERS):
                wait_rows((i + b) % GATHER_BUFFERS)

    @pl.when(i >= n_used)
    def _():
        y_ref[...] = jnp.zeros(y_ref.shape, F32)


def _moe_experts(xn, tokens, first, block_e, next_e, parity, n_used, n_slots, w_gate, w_up, w_down, layer):
    grid_spec = pltpu.PrefetchScalarGridSpec(
        num_scalar_prefetch=6, grid=(n_slots // EXPERT_BLOCK,),
        in_specs=[pl.BlockSpec(memory_space=pl.ANY), pl.BlockSpec(memory_space=pl.ANY),
                  pl.BlockSpec(memory_space=pl.ANY), pl.BlockSpec(memory_space=pl.ANY)],
        out_specs=pl.BlockSpec((EXPERT_BLOCK, D_MODEL), lambda i, *_: (i, 0)),
        scratch_shapes=[pltpu.VMEM((GATHER_BUFFERS, EXPERT_BLOCK, D_MODEL), F32),
                        pltpu.VMEM((2, D_MODEL, D_FF_EXPERT), F32),
                        pltpu.VMEM((2, D_MODEL, D_FF_EXPERT), F32),
                        pltpu.VMEM((2, D_FF_EXPERT, D_MODEL), F32),
                        pltpu.VMEM((D_MODEL, D_FF_EXPERT), BF16),
                        pltpu.VMEM((D_MODEL, D_FF_EXPERT), BF16),
                        pltpu.VMEM((D_FF_EXPERT, D_MODEL), BF16),
                        pltpu.SemaphoreType.DMA((GATHER_BUFFERS,)), pltpu.SemaphoreType.DMA((2,))])
    return pl.pallas_call(
        functools.partial(_moe_body, layer), grid_spec=grid_spec,
        out_shape=jax.ShapeDtypeStruct((n_slots, D_MODEL), F32),
        compiler_params=_params(1), name=f"moe_{layer}")(
            block_e, next_e, parity, n_used, first, tokens, xn, w_gate, w_up, w_down)


def _slot_layout(info, counts_tile, n_slots):
    counts = counts_tile[0, :N_EXPERTS].astype(jnp.int32)
    expert = info[:, 0:TOP_K_INNER].astype(jnp.int32)
    rank = info[:, TOP_K_INNER:2 * TOP_K_INNER].astype(jnp.int32)
    padded = (counts + EXPERT_BLOCK - 1) // EXPERT_BLOCK * EXPERT_BLOCK
    pad_end = jnp.cumsum(padded)
    pad_start = pad_end - padded
    hot = expert[..., None] == jnp.arange(N_EXPERTS, dtype=jnp.int32)
    dest = (jnp.sum(jnp.where(hot, pad_start, 0), axis=-1) + rank).reshape(-1)
    block_start = jnp.arange(n_slots // EXPERT_BLOCK, dtype=jnp.int32) * EXPERT_BLOCK
    block_e = jnp.minimum(jnp.sum(pad_end[None, :] <= block_start[:, None], axis=1), N_EXPERTS - 1)
    n_used = pad_end[-1] // EXPERT_BLOCK
    own_end = jnp.sum(jnp.where(block_e[:, None] == jnp.arange(N_EXPERTS), pad_end, 0), axis=1)
    after = own_end // EXPERT_BLOCK
    next_e = jnp.where(after < n_used, jnp.sum(pad_end[None, :] <= (after * EXPERT_BLOCK)[:, None], axis=1), -1)
    changed = jnp.concatenate([jnp.zeros((1,), jnp.int32), (block_e[1:] != block_e[:-1]).astype(jnp.int32)])
    parity = jnp.cumsum(changed) % 2
    order = jnp.argsort(expert.reshape(-1), stable=True)
    tokens = jnp.concatenate([order // TOP_K_INNER, jnp.zeros((EXPERT_BLOCK,), order.dtype)])
    seg_start = jnp.cumsum(counts) - counts
    own = lambda table: jnp.sum(jnp.where(block_e[:, None] == jnp.arange(N_EXPERTS), table, 0), axis=1)
    first = own(seg_start) + block_start - own(pad_start)
    i32 = lambda a: a.astype(jnp.int32)
    return (i32(dest), i32(tokens), i32(first), i32(block_e), i32(next_e), i32(parity), i32(n_used).reshape(1))


def _start_row_gather(dest_ref, ys_hbm, ybuf, sem, tile, sl):
    for t in range(ROW_TILE):
        for k in range(TOP_K_INNER):
            d = dest_ref[TOP_K_INNER * (tile * ROW_TILE + t) + k]
            pltpu.make_async_copy(ys_hbm.at[pl.ds(d, 1)], ybuf.at[sl, k, pl.ds(t, 1)], sem.at[sl]).start()


def _moe_output_tile(dest_ref, info_ref, ys_hbm, ybuf, sem, tile0):
    i = pl.program_id(0)
    last = pl.num_programs(0) - 1
    sl = i % GATHER_BUFFERS

    def wait(slot):
        for k in range(TOP_K_INNER):
            pltpu.make_async_copy(ys_hbm.at[pl.ds(0, ROW_TILE)], ybuf.at[slot, k], sem.at[slot]).wait()

    @pl.when(i == 0)
    def _():
        for b in range(GATHER_AHEAD):
            _start_row_gather(dest_ref, ys_hbm, ybuf, sem, tile0 + jnp.minimum(b, last), b)

    wait(sl)
    info = info_ref[...]
    w_lane = 2 * TOP_K_INNER
    y = ybuf[sl, 0] * info[:, w_lane:w_lane + 1] + ybuf[sl, 1] * info[:, w_lane + 1:w_lane + 2]
    _start_row_gather(dest_ref, ys_hbm, ybuf, sem, tile0 + jnp.minimum(i + GATHER_AHEAD, last),
                      (i + GATHER_AHEAD) % GATHER_BUFFERS)

    def drain():
        @pl.when(i == last)
        def _():
            for b in range(1, GATHER_BUFFERS):
                wait((i + b) % GATHER_BUFFERS)
    return y, drain


_COMBINE_SCRATCH = [pltpu.VMEM((GATHER_BUFFERS, TOP_K_INNER, ROW_TILE, D_MODEL), F32),
                    pltpu.SemaphoreType.DMA((GATHER_BUFFERS,))]


def _final_body(tile0, dest_ref, h_ref, info_ref, ys_hbm, g_ref, o_ref, ybuf, sem):
    y, drain = _moe_output_tile(dest_ref, info_ref, ys_hbm, ybuf, sem, tile0)
    o_ref[...] = _rms(h_ref[...] + y, g_ref[...])
    drain()


def _final_norm(h, info, ys, dest, gain, tile0, n_tiles):
    tm = ROW_TILE
    grid_spec = pltpu.PrefetchScalarGridSpec(
        num_scalar_prefetch=1, grid=(n_tiles,),
        in_specs=[pl.BlockSpec((tm, D_MODEL), lambda i, d: (i + tile0, 0)),
                  pl.BlockSpec((tm, LANES), lambda i, d: (i + tile0, 0)),
                  pl.BlockSpec(memory_space=pl.ANY),
                  pl.BlockSpec((1, D_MODEL), lambda i, d: (0, 0))],
        out_specs=pl.BlockSpec((tm, D_MODEL), lambda i, d: (i, 0)),
        scratch_shapes=_COMBINE_SCRATCH)
    return pl.pallas_call(
        functools.partial(_final_body, tile0), grid_spec=grid_spec,
        out_shape=jax.ShapeDtypeStruct((n_tiles * tm, D_MODEL), F32),
        compiler_params=_params(1), name="final_norm")(dest, h, info, ys, gain)


def _heads_g4_major(w, axis):
    shape = w.shape
    split = shape[:axis] + (N_KV_HEADS, GQA_GROUP, HEAD_DIM) + shape[axis + 1:]
    return jnp.swapaxes(w.reshape(split), axis, axis + 1).reshape(shape)


def _prep_qkv_weight(w, n_groups):
    cols = []
    for g in range(n_groups):
        base = g * QKV_WIDTH
        cols.append(_heads_g4_major(w[:, base:base + Q_WIDTH], 1) * ATTN_SCALE)
        cols.append(w[:, base + Q_WIDTH:base + QKV_WIDTH])
    return jnp.concatenate(cols, axis=1).astype(BF16)


def _rope_tables(seq_len, past_len, n_new):
    inv_freq = ROPE_THETA ** (-jnp.arange(0, ROT_DIM, 2, dtype=F32) / ROT_DIM)
    pos_p = jnp.arange(seq_len, dtype=jnp.int32)
    pos_s = past_len + (jnp.arange(ROW_TILE, dtype=jnp.int32) % n_new)
    ang = jnp.concatenate([pos_p, pos_s]).astype(F32)[:, None] * inv_freq[None, :]
    cos, sin = jnp.cos(ang), jnp.sin(ang)
    in_head = jnp.arange(LANES, dtype=jnp.int32) % HEAD_DIM
    spread = lambda t: jnp.tile(t, (1, LANES // ROT_HALF))
    c = jnp.where(in_head < ROT_DIM, spread(cos), 1.0)
    s1 = jnp.where(in_head < ROT_HALF, -spread(sin), 0.0)
    s2 = jnp.where((in_head >= ROT_HALF) & (in_head < ROT_DIM), spread(sin), 0.0)
    return c, s1, s2


def _split_hi_lo(w):
    hi = w.astype(BF16)
    return hi, (w - hi.astype(F32)).astype(BF16)


def _state_rows(k, v, lead):
    kv = jnp.stack([k, v], axis=-2)
    return kv.reshape(lead + (k.shape[-2], 2, N_KV_HEADS, HEAD_DIM))


def kernel(x_prompt, x_sample, cache_a_kv, cache_b_kv0, cache_b_kv1, cache_b_kv2, norm_mix, norm_ffn, norm_final, a_w_qkv, a_sinks, a_w_o, b_w_qkv, b_w_o, moe_router_g_w, moe_router_g_b, moe_router_e_w, moe_router_e_b, moe_w_gate, moe_w_up, moe_w_down):
    n_seq, seq_len, _ = x_prompt.shape
    n_dec, n_new, _ = x_sample.shape
    tp, ts = n_seq * seq_len, n_dec * n_new
    n_tok = tp + ts
    n_ptiles = tp // ROW_TILE
    tiles_per_seq = seq_len // ROW_TILE
    n_slots = (-(-(n_tok * TOP_K_INNER) // EXPERT_BLOCK) + N_EXPERTS) * EXPERT_BLOCK
    caches_b = (cache_b_kv0, cache_b_kv1, cache_b_kv2)

    tables = _rope_tables(seq_len, PAST_LEN, n_new)
    xp = x_prompt.reshape(tp, D_MODEL)
    xs = x_sample.reshape(ts, D_MODEL)

    def keys_on_lanes(cache):
        return jnp.transpose(cache, (0, 2, 3, 4, 1))

    def router(i):
        w = jnp.concatenate([moe_router_g_w[i], moe_router_e_w[i],
                             jnp.zeros((D_MODEL, LANES - N_EXPERT_GROUPS - N_EXPERTS), F32)], axis=1)
        b = jnp.concatenate([moe_router_g_b[i], moe_router_e_b[i],
                             jnp.zeros((LANES - N_EXPERT_GROUPS - N_EXPERTS,), F32)]).reshape(1, LANES)
        return jnp.concatenate(_split_hi_lo(w), axis=1), b

    def moe(i, xn, info, counts_tile):
        dest, tokens, first, block_e, next_e, parity, n_used = _slot_layout(info, counts_tile, n_slots)
        ys = _moe_experts(xn, tokens, first, block_e, next_e, parity, n_used, n_slots,
                          moe_w_gate, moe_w_up, moe_w_down, i)
        return ys, dest

    def sample_rows(t, d, width):
        return t[tp // d:].reshape(ts, width)

    def window_state(k, v, n_keep, d):
        tail = lambda t: jnp.stack([t[((b + 1) * seq_len - n_keep) // d:(b + 1) * seq_len // d].reshape(n_keep, KV_WIDTH)
                                    for b in range(n_seq)])
        return _state_rows(tail(k), tail(v), (n_seq,))[None]

    def new_rows_state(k, v, d):
        rows = lambda t: sample_rows(t, d, KV_WIDTH).reshape(n_dec, n_new, KV_WIDTH)
        return _state_rows(rows(k), rows(v), (n_dec,))[None]

    def decode(q, k, v, d, cache, sinks, n_keys, out_dtype, want_lse):
        return _sample_attention(sample_rows(q, d, Q_WIDTH).astype(F32), sample_rows(k, d, KV_WIDTH),
                                 sample_rows(v, d, KV_WIDTH), keys_on_lanes(cache), sinks,
                                 n_new, n_keys, d, out_dtype, want_lse)

    w_a = _prep_qkv_weight(a_w_qkv[0], 1)
    h0, [(q, k, v)] = _project("split", (xp, xs), norm_mix[0:1], w_a, tables, (1,), n_tok, n_ptiles, tiles_per_seq)
    sinks = a_sinks[0].astype(F32)
    o_p, _ = _band_attention(q, k, v, sinks, n_seq, seq_len, 1, WINDOW_A - 1, BF16, False)
    o_s, _ = decode(q, k, v, 1, cache_a_kv[0], sinks, WINDOW_A, BF16, False)
    new_a_p = window_state(k, v, min(WINDOW_A, seq_len), 1)
    new_a_s = new_rows_state(k, v, 1)
    wr, br = router(0)
    h1, xn, info, counts_tile = _out_project([(o_p, o_s)], None, (1,), h0, _heads_g4_major(a_w_o[0], 0).astype(BF16),
                                             norm_ffn[0:1], wr, br, n_ptiles)
    ys, dest = moe(0, xn, info, counts_tile)

    w_b = _prep_qkv_weight(b_w_qkv[0], len(B_PATTERNS))
    dils = tuple(d for _, d in B_PATTERNS)
    h2, groups = _project("combine", (h1, info, ys, dest), norm_mix[1:2], w_b, tables, dils,
                          n_tok, n_ptiles, tiles_per_seq)
    o_pairs, lse_pairs, new_b = [], [], []
    for g, (window, dil) in enumerate(B_PATTERNS):
        q, k, v = groups[g]
        o_p, lse_p = _band_attention(q, k, v, None, n_seq, seq_len, dil, window // dil, F32, True)
        o_s, lse_s = decode(q, k, v, dil, caches_b[g][0], None, window // dil + 1, F32, True)
        o_pairs.append((o_p, o_s))
        lse_pairs.append((lse_p, lse_s))
        new_b.append(window_state(k, v, min(window, seq_len), dil))
        new_b.append(new_rows_state(k, v, dil))
    wr, br = router(1)
    h3, xn, info, counts_tile = _out_project(o_pairs, lse_pairs, dils, h2, _heads_g4_major(b_w_o[0], 0).astype(BF16),
                                             norm_ffn[1:2], wr, br, n_ptiles)
    ys, dest = moe(1, xn, info, counts_tile)

    gain = norm_final.reshape(1, D_MODEL)
    y_p = _final_norm(h3, info, ys, dest, gain, 0, n_ptiles).reshape(n_seq, seq_len, D_MODEL)
    y_s = _final_norm(h3, info, ys, dest, gain, n_ptiles, ts // ROW_TILE).reshape(n_dec, n_new, D_MODEL)
    return (y_p, y_s, new_a_p, new_a_s, *new_b)
```
